```python
import jax, jax.numpy as jnp
from jax import lax
import numpy as np

D_MODEL = 2048
BATCH = 8
SEQ = 2048
DEPTH = 1
DEC_BATCH = 128
DEC_SEQ = 4
PAST_LEN = 16384
PAGE_SIZE = 128

D_RNN = D_MODEL
RNN_BLOCKS = 16
RNN_BLOCK_W = D_RNN // RNN_BLOCKS
CONV_W = 4
RG_C = 8.0
N_HEADS = 16
QK_NOPE = 128
QK_ROPE = 64
QK_DIM = QK_NOPE + QK_ROPE
V_DIM = D_MODEL // N_HEADS
Q_LORA = 512
KV_LORA = 512
ROPE_THETA = 10000.0
ATTN_SCALE = QK_DIM ** -0.5
Q_BLOCK = 128
IN_COLS = 2 * D_RNN + Q_LORA + KV_LORA + QK_ROPE + 2 * D_MODEL
N_GROUPS = 4
EXPERTS_PER_GROUP = 8
N_EXPERTS = N_GROUPS * EXPERTS_PER_GROUP
TOP_K = 2
D_EXPERT = 512
EPS = 1e-6

kernel_name = 'hawk_mla_hier_moe_adaln_step'


def rms_norm(x, g):
    xf = x.astype(jnp.float32)
    xf = xf * lax.rsqrt(jnp.mean(xf * xf, axis=-1, keepdims=True) + EPS)
    return (xf * g.astype(jnp.float32)).astype(x.dtype)


def rope(x, pos):
    half = x.shape[-1] // 2
    freqs = ROPE_THETA ** (-jnp.arange(half, dtype=jnp.float32) / half)
    ang = pos[:, None] * freqs[None, :]
    shape = (1, pos.shape[0]) + (1,) * (x.ndim - 3) + (half,)
    cos = jnp.cos(ang).reshape(shape)
    sin = jnp.sin(ang).reshape(shape)
    xf = x.astype(jnp.float32)
    x1, x2 = xf[..., :half], xf[..., half:]
    return jnp.concatenate([x1 * cos - x2 * sin, x1 * sin + x2 * cos], axis=-1).astype(x.dtype)


def causal_conv(x, prev, w, b):
    full = jnp.concatenate([prev.astype(x.dtype), x], axis=1)
    t = x.shape[1]
    y = b
    for k in range(CONV_W):
        y = y + full[:, k:k + t] * w[k]
    return y, full[:, -(CONV_W - 1):]


def rglru(xc, h_prev, w_a, b_a, w_x, b_x, lam):
    bsz, t, _ = xc.shape
    xb = xc.reshape(bsz, t, RNN_BLOCKS, RNN_BLOCK_W)
    r = jax.nn.sigmoid((jnp.einsum('btnd,nde->btne', xb, w_a) + b_a).astype(jnp.float32)).reshape(bsz, t, D_RNN)
    i = jax.nn.sigmoid((jnp.einsum('btnd,nde->btne', xb, w_x) + b_x).astype(jnp.float32)).reshape(bsz, t, D_RNN)
    log_a = -RG_C * r * jax.nn.softplus(-lam.astype(jnp.float32))
    a = jnp.exp(log_a)
    b = jnp.sqrt(-jnp.expm1(2.0 * log_a)) * i * xc.astype(jnp.float32)
    b = b.at[:, 0].add(a[:, 0] * h_prev.astype(jnp.float32))

    def combine(left, right):
        a1, b1 = left
        a2, b2 = right
        return a1 * a2, a2 * b1 + b2

    _, h = lax.associative_scan(combine, (a, b), axis=1)
    return h.astype(xc.dtype), h[:, -1].astype(h_prev.dtype)


def attend_prompt(q_lat, q_rope, ckv, krope):
    bsz, s, h, c = q_lat.shape
    nb = s // Q_BLOCK
    ql = q_lat.reshape(bsz, nb, Q_BLOCK, h, c).transpose(1, 0, 2, 3, 4)
    qr = q_rope.reshape(bsz, nb, Q_BLOCK, h, QK_ROPE).transpose(1, 0, 2, 3, 4)
    kpos = jnp.arange(s)

    def block(args):
        qlb, qrb, bi = args
        sc = (jnp.einsum('bqhc,bkc->bhqk', qlb, ckv, preferred_element_type=jnp.float32)
              + jnp.einsum('bqhr,bkr->bhqk', qrb, krope, preferred_element_type=jnp.float32)) * ATTN_SCALE
        qpos = bi * Q_BLOCK + jnp.arange(Q_BLOCK)
        sc = jnp.where(kpos[None, :] <= qpos[:, None], sc, -jnp.inf)
        p = jax.nn.softmax(sc, axis=-1).astype(ckv.dtype)
        return jnp.einsum('bhqk,bkc->bqhc', p, ckv)

    o = lax.map(block, (ql, qr, jnp.arange(nb)))
    return o.transpose(1, 0, 2, 3, 4).reshape(bsz, s, h, c)


def attend_sample(q_lat, q_rope, ckv_new, krope_new, pool_ckv, pool_krope, page_table):
    t = q_lat.shape[1]
    kidx = jnp.arange(PAST_LEN + t)
    qidx = PAST_LEN + jnp.arange(t)
    mask = kidx[None, :] <= qidx[:, None]

    def one(args):
        ql, qr, cn, kn, pt = args
        ck = jnp.concatenate([pool_ckv[pt].reshape(-1, KV_LORA), cn.astype(pool_ckv.dtype)], axis=0)
        kr = jnp.concatenate([pool_krope[pt].reshape(-1, QK_ROPE), kn.astype(pool_krope.dtype)], axis=0)
        sc = (jnp.einsum('qhc,kc->hqk', ql, ck, preferred_element_type=jnp.float32)
              + jnp.einsum('qhr,kr->hqk', qr, kr, preferred_element_type=jnp.float32)) * ATTN_SCALE
        sc = jnp.where(mask[None], sc, -jnp.inf)
        p = jax.nn.softmax(sc, axis=-1).astype(ck.dtype)
        return jnp.einsum('hqk,kc->qhc', p, ck).astype(ql.dtype)

    return lax.map(one, (q_lat, q_rope, ckv_new, krope_new, page_table))


def token_mixers(hn, pos, conv_prev, h_prev, attend, lw):
    bsz, t, _ = hn.shape
    proj = hn @ lw['w_in']
    o1 = D_RNN
    o2 = o1 + D_RNN
    o3 = o2 + Q_LORA
    o4 = o3 + KV_LORA
    o5 = o4 + QK_ROPE
    o6 = o5 + D_MODEL
    xr, gr, cq, ckv, kr, gate_a, gate_b = jnp.split(proj, [o1, o2, o3, o4, o5, o6], axis=-1)
    xc, conv_new = causal_conv(xr, conv_prev, lw['w_conv'], lw['b_conv'])
    y_rnn, h_new = rglru(xc, h_prev, lw['w_rg_a'], lw['b_rg_a'], lw['w_rg_x'], lw['b_rg_x'], lw['rg_lambda'])
    y_a = y_rnn * jax.nn.gelu(gr)
    cq = rms_norm(cq, lw['g_q'])
    q = jnp.einsum('btc,chd->bthd', cq, lw['w_uq'])
    q_nope = q[..., :QK_NOPE]
    q_rope = rope(q[..., QK_NOPE:], pos)
    ckv = rms_norm(ckv, lw['g_kv'])
    krope = rope(kr, pos)
    q_lat = jnp.einsum('bthn,chn->bthc', q_nope, lw['w_uk'])
    o_lat = attend(q_lat, q_rope, ckv, krope)
    y_b = jnp.einsum('bthc,chv->bthv', o_lat, lw['w_uv']).reshape(bsz, t, N_HEADS * V_DIM)
    merged = jax.nn.sigmoid(gate_a) * y_a + jax.nn.sigmoid(gate_b) * y_b
    return merged @ lw['w_o'], ckv, krope, conv_new, h_new


def hier_moe(h, lw):
    hf = h.astype(jnp.float32)
    p_group = jax.nn.softmax(hf @ lw['w_group'].astype(jnp.float32) + lw['b_group'].astype(jnp.float32), axis=-1)
    p_top, g_idx = lax.top_k(p_group, 1)
    logits = (hf @ lw['w_router'].astype(jnp.float32) + lw['b_router'].astype(jnp.float32))
    logits = logits.reshape(h.shape[:-1] + (N_GROUPS, EXPERTS_PER_GROUP))
    l_in = jnp.take_along_axis(logits, g_idx[..., None], axis=-2)[..., 0, :]
    w_top, e_idx = lax.top_k(jax.nn.softmax(l_in, axis=-1), TOP_K)
    w_top = w_top / jnp.sum(w_top, axis=-1, keepdims=True) * p_top
    e_global = g_idx * EXPERTS_PER_GROUP + e_idx
    combine = jnp.sum(jax.nn.one_hot(e_global, N_EXPERTS, dtype=jnp.float32) * w_top[..., None], axis=-2).astype(h.dtype)
    out = jnp.zeros_like(h)
    for e in range(N_EXPERTS):
        ff = (jax.nn.silu(h @ lw['w_gate'][e]) * (h @ lw['w_up'][e])) @ lw['w_down'][e]
        out = out + combine[..., e:e + 1] * ff
    return out


def decoder_layer(x, c, pos, conv_prev, h_prev, attend, lw):
    mod = (jax.nn.silu(c) @ lw['w_ada'] + lw['b_ada'])[:, None, :]
    sh1, sc1, g1, sh2, sc2, g2 = jnp.split(mod, 6, axis=-1)
    hn = rms_norm(x, lw['g_norm1']) * (1.0 + sc1) + sh1
    mix, ckv, krope, conv_new, h_new = token_mixers(hn, pos, conv_prev, h_prev, attend, lw)
    x = x + g1 * mix
    hn2 = rms_norm(x, lw['g_norm2']) * (1.0 + sc2) + sh2
    x = x + g2 * hier_moe(hn2, lw)
    return x, ckv, krope, conv_new, h_new


def setup_inputs(seed: int = 0) -> dict:
    key = jax.random.key(seed)
    ks = jax.random.split(key, 40)
    f32 = jnp.float32
    n_pages = PAST_LEN // PAGE_SIZE
    n_phys = (DEC_BATCH * n_pages * 5) // 4

    def nrm(k, shape, scale):
        return jax.random.normal(k, shape, f32) * scale

    def gain(k, shape):
        return 1.0 + 0.05 * jax.random.normal(k, shape, f32)

    page_table = jax.random.permutation(ks[7], n_phys)[: DEC_BATCH * n_pages].reshape(DEC_BATCH, n_pages).astype(jnp.int32)
    u = jax.random.uniform(ks[20], (DEPTH, D_RNN), f32, 0.9, 0.999)
    s = u ** (1.0 / RG_C)
    rg_lambda = jnp.log(s) - jnp.log1p(-s)
    return {
        'x_prompt': nrm(ks[0], (BATCH, SEQ, D_MODEL), 1.0),
        'x_sample': nrm(ks[1], (DEC_BATCH, DEC_SEQ, D_MODEL), 1.0),
        'cache_ckv': nrm(ks[2], (DEPTH, n_phys, PAGE_SIZE, KV_LORA), 1.0),
        'cache_krope': nrm(ks[3], (DEPTH, n_phys, PAGE_SIZE, QK_ROPE), 1.0),
        'state_conv': nrm(ks[4], (DEPTH, DEC_BATCH, CONV_W - 1, D_RNN), 1.0),
        'state_h': nrm(ks[5], (DEPTH, DEC_BATCH, D_RNN), 0.5),
        'page_table': page_table,
        'c_prompt': nrm(ks[6], (BATCH, D_MODEL), 1.0),
        'c_sample': nrm(ks[8], (DEC_BATCH, D_MODEL), 1.0),
        'w_ada': nrm(ks[9], (DEPTH, D_MODEL, 6 * D_MODEL), 0.5 * D_MODEL ** -0.5),
        'b_ada': nrm(ks[10], (DEPTH, 6 * D_MODEL), 0.02),
        'g_norm1': gain(ks[11], (DEPTH, D_MODEL)),
        'g_norm2': gain(ks[12], (DEPTH, D_MODEL)),
        'w_in': nrm(ks[13], (DEPTH, D_MODEL, IN_COLS), D_MODEL ** -0.5),
        'w_conv': nrm(ks[14], (DEPTH, CONV_W, D_RNN), CONV_W ** -0.5),
        'b_conv': nrm(ks[15], (DEPTH, D_RNN), 0.02),
        'w_rg_a': nrm(ks[16], (DEPTH, RNN_BLOCKS, RNN_BLOCK_W, RNN_BLOCK_W), RNN_BLOCK_W ** -0.5),
        'b_rg_a': nrm(ks[17], (DEPTH, RNN_BLOCKS, RNN_BLOCK_W), 0.02),
        'w_rg_x': nrm(ks[18], (DEPTH, RNN_BLOCKS, RNN_BLOCK_W, RNN_BLOCK_W), RNN_BLOCK_W ** -0.5),
        'b_rg_x': nrm(ks[19], (DEPTH, RNN_BLOCKS, RNN_BLOCK_W), 0.02),
        'rg_lambda': rg_lambda,
        'g_q': gain(ks[21], (DEPTH, Q_LORA)),
        'w_uq': nrm(ks[22], (DEPTH, Q_LORA, N_HEADS, QK_DIM), Q_LORA ** -0.5),
        'g_kv': gain(ks[23], (DEPTH, KV_LORA)),
        'w_uk': nrm(ks[24], (DEPTH, KV_LORA, N_HEADS, QK_NOPE), KV_LORA ** -0.5),
        'w_uv': nrm(ks[25], (DEPTH, KV_LORA, N_HEADS, V_DIM), KV_LORA ** -0.5),
        'w_o': nrm(ks[26], (DEPTH, D_MODEL, D_MODEL), D_MODEL ** -0.5),
        'w_group': nrm(ks[27], (DEPTH, D_MODEL, N_GROUPS), D_MODEL ** -0.5),
        'b_group': nrm(ks[28], (DEPTH, N_GROUPS), 0.01),
        'w_router': nrm(ks[29], (DEPTH, D_MODEL, N_EXPERTS), D_MODEL ** -0.5),
        'b_router': nrm(ks[30], (DEPTH, N_EXPERTS), 0.01),
        'w_gate': nrm(ks[31], (DEPTH, N_EXPERTS, D_MODEL, D_EXPERT), D_MODEL ** -0.5),
        'w_up': nrm(ks[32], (DEPTH, N_EXPERTS, D_MODEL, D_EXPERT), D_MODEL ** -0.5),
        'w_down': nrm(ks[33], (DEPTH, N_EXPERTS, D_EXPERT, D_MODEL), D_EXPERT ** -0.5),
        'g_final': gain(ks[34], (D_MODEL,)),
    }


def reference(x_prompt, x_sample, cache_ckv, cache_krope, state_conv, state_h, page_table, c_prompt, c_sample,
              w_ada, b_ada, g_norm1, g_norm2, w_in, w_conv, b_conv, w_rg_a, b_rg_a, w_rg_x, b_rg_x, rg_lambda,
              g_q, w_uq, g_kv, w_uk, w_uv, w_o, w_group, b_group, w_router, b_router, w_gate, w_up, w_down, g_final):
    pos_p = jnp.arange(x_prompt.shape[1], dtype=jnp.float32)
    pos_s = PAST_LEN + jnp.arange(x_sample.shape[1], dtype=jnp.float32)
    xp, xs = x_prompt, x_sample
    ckv_ps, kr_ps, cv_ps, h_ps = [], [], [], []
    ckv_ss, kr_ss, cv_ss, h_ss = [], [], [], []
    for l in range(DEPTH):
        lw = {
            'w_ada': w_ada[l], 'b_ada': b_ada[l], 'g_norm1': g_norm1[l], 'g_norm2': g_norm2[l],
            'w_in': w_in[l], 'w_conv': w_conv[l], 'b_conv': b_conv[l],
            'w_rg_a': w_rg_a[l], 'b_rg_a': b_rg_a[l], 'w_rg_x': w_rg_x[l], 'b_rg_x': b_rg_x[l],
            'rg_lambda': rg_lambda[l], 'g_q': g_q[l], 'w_uq': w_uq[l], 'g_kv': g_kv[l],
            'w_uk': w_uk[l], 'w_uv': w_uv[l], 'w_o': w_o[l],
            'w_group': w_group[l], 'b_group': b_group[l], 'w_router': w_router[l], 'b_router': b_router[l],
            'w_gate': w_gate[l], 'w_up': w_up[l], 'w_down': w_down[l],
        }
        conv0 = jnp.zeros((xp.shape[0], CONV_W - 1, D_RNN), xp.dtype)
        h0 = jnp.zeros((xp.shape[0], D_RNN), state_h.dtype)
        xp, ckv_p, kr_p, cv_p, h_p = decoder_layer(xp, c_prompt, pos_p, conv0, h0, attend_prompt, lw)
        attend_s = lambda ql, qr, cn, kn, l=l: attend_sample(ql, qr, cn, kn, cache_ckv[l], cache_krope[l], page_table)
        xs, ckv_s, kr_s, cv_s, h_s = decoder_layer(xs, c_sample, pos_s, state_conv[l], state_h[l], attend_s, lw)
        ckv_ps.append(ckv_p); kr_ps.append(kr_p); cv_ps.append(cv_p); h_ps.append(h_p)
        ckv_ss.append(ckv_s); kr_ss.append(kr_s); cv_ss.append(cv_s); h_ss.append(h_s)
    y_prompt = rms_norm(xp, g_final)
    y_sample = rms_norm(xs, g_final)
    return (y_prompt, y_sample,
            jnp.stack(ckv_ps), jnp.stack(kr_ps), jnp.stack(cv_ps), jnp.stack(h_ps),
            jnp.stack(ckv_ss), jnp.stack(kr_ss), jnp.stack(cv_ss), jnp.stack(h_ss))
```

```python
import functools

import jax
import jax.numpy as jnp
import numpy as np
from jax import lax
from jax.experimental import pallas as pl
from jax.experimental.pallas import tpu as pltpu

F32 = jnp.float32
BF16 = jnp.bfloat16

EPS = 1e-6
RG_C = 8.0
RNN_BLOCK_W = 128
CONV_W = 4
QK_NOPE = 128
QK_ROPE = 64
ROPE_THETA = 10000.0
N_GROUPS = 4
EXPERTS_PER_GROUP = 8
N_EXPERTS = N_GROUPS * EXPERTS_PER_GROUP
ROUTE_LANES = 128
NEG_BIG = -1e30
VMEM_LIMIT = 56 * 1024 * 1024


def _cparams(sem, vmem=VMEM_LIMIT):
    return pltpu.CompilerParams(dimension_semantics=sem, vmem_limit_bytes=vmem)


def _rms(x, g):
    return x * lax.rsqrt(jnp.mean(x * x, axis=-1, keepdims=True) + EPS) * g


def _gelu_tanh(x):
    return 0.5 * x * (1.0 + jnp.tanh(0.7978845608028654 * (x + 0.044715 * x * x * x)))


def _dot(a, b):
    return jnp.dot(a, b, preferred_element_type=F32)


def _dot_nt(a, b):
    return lax.dot_general(a, b, (((1,), (1,)), ((), ())), preferred_element_type=F32)


def _ada_kernel(c_ref, w_ref, b_ref, o_ref):
    c = c_ref[...]
    s = (c * jax.nn.sigmoid(c)).astype(BF16)
    o_ref[...] = _dot(s, w_ref[...].astype(BF16)) + b_ref[...]


def ada_mod(c_all, w_ada, b_ada, tn=1024):
    m, d = c_all.shape
    n = w_ada.shape[1]
    return pl.pallas_call(
        _ada_kernel,
        grid=(n // tn,),
        in_specs=[pl.BlockSpec((m, d), lambda j: (0, 0)),
                  pl.BlockSpec((d, tn), lambda j: (0, j)),
                  pl.BlockSpec((1, tn), lambda j: (0, j))],
        out_specs=pl.BlockSpec((m, tn), lambda j: (0, j)),
        out_shape=jax.ShapeDtypeStruct((m, n), F32),
        compiler_params=_cparams(("arbitrary",)),
        name="ada_mod",
    )(c_all, w_ada, b_ada.reshape(1, n))


def _inproj_kernel(x_ref, sc_ref, sh_ref, g_ref, wa_ref, wb_ref, gq_ref, gkv_ref, cos_ref, sin_ref,
                   pa_ref, cq_ref, ckv_ref, kr_ref, hn_ref, *, q_lora, kv_lora):
    @pl.when(pl.program_id(1) == 0)
    def _():
        hn = _rms(x_ref[...], g_ref[...]) * (1.0 + sc_ref[...]) + sh_ref[...]
        hb = hn.astype(BF16)
        hn_ref[...] = hb
        pb = _dot(hb, wb_ref[...])
        o1 = q_lora
        o2 = o1 + kv_lora
        cq_ref[...] = _rms(pb[:, :o1], gq_ref[...]).astype(BF16)
        ckv_ref[...] = _rms(pb[:, o1:o2], gkv_ref[...])
        kr_ref[...] = pb[:, o2:o2 + QK_ROPE] * cos_ref[...] + pb[:, o2 + QK_ROPE:o2 + 2 * QK_ROPE] * sin_ref[...]

    pa_ref[...] = _dot(hn_ref[...], wa_ref[...]).astype(BF16)


def in_proj(x, sc, sh, g1, wa, wb, gq, gkv, cos64, sin64, *, per_seq, tm, tn=1024):
    t, d = x.shape
    na = wa.shape[1]
    q_lora, kv_lora = gq.shape[1], gkv.shape[1]
    if per_seq:
        tps = per_seq // tm
        mod_spec = pl.BlockSpec((None, 1, d), lambda i, j: (i // tps, 0, 0))
        tab_spec = pl.BlockSpec((tm, QK_ROPE), lambda i, j: (i % tps, 0))
    else:
        mod_spec = pl.BlockSpec((tm, d), lambda i, j: (i, 0))
        tab_spec = pl.BlockSpec((tm, QK_ROPE), lambda i, j: (i, 0))
    row = lambda w: pl.BlockSpec((tm, w), lambda i, j: (i, 0))
    full = lambda a: pl.BlockSpec(a.shape, lambda i, j: (0,) * a.ndim)
    return pl.pallas_call(
        functools.partial(_inproj_kernel, q_lora=q_lora, kv_lora=kv_lora),
        grid=(t // tm, na // tn),
        in_specs=[row(d), mod_spec, mod_spec, full(g1),
                  pl.BlockSpec((d, tn), lambda i, j: (0, j)), full(wb), full(gq), full(gkv), tab_spec, tab_spec],
        out_specs=[pl.BlockSpec((tm, tn), lambda i, j: (i, j)), row(q_lora), row(kv_lora), row(QK_ROPE)],
        out_shape=[jax.ShapeDtypeStruct((t, na), BF16), jax.ShapeDtypeStruct((t, q_lora), BF16),
                   jax.ShapeDtypeStruct((t, kv_lora), F32), jax.ShapeDtypeStruct((t, QK_ROPE), F32)],
        scratch_shapes=[pltpu.VMEM((tm, d), BF16)],
        compiler_params=_cparams(("arbitrary", "arbitrary")),
        name="in_proj",
    )(x, sc, sh, g1, wa, wb, gq, gkv, cos64, sin64)


def _rg_coeffs(xc, wrg_ref, brg_ref, sp):
    a_parts, b_parts = [], []
    for n in range(xc.shape[1] // RNN_BLOCK_W):
        sl = slice(n * RNN_BLOCK_W, (n + 1) * RNN_BLOCK_W)
        xb = xc[:, sl]
        z = _dot(xb.astype(BF16), wrg_ref[n]) + brg_ref[n]
        r = jax.nn.sigmoid(z[:, :RNN_BLOCK_W])
        i = jax.nn.sigmoid(z[:, RNN_BLOCK_W:])
        a = jnp.exp(-RG_C * r * sp[:, sl])
        a_parts.append(a)
        b_parts.append(jnp.sqrt(1.0 - a * a) * i * xb)
    return jnp.concatenate(a_parts, axis=1), jnp.concatenate(b_parts, axis=1)


def _softplus_neg(lam):
    return jnp.maximum(-lam, 0.0) + jnp.log1p(jnp.exp(-jnp.abs(lam)))


def _rglru_seq_kernel(x_ref, g_ref, wc_ref, bc_ref, wrg_ref, brg_ref, lam_ref,
                      y_ref, conv_ref, h_ref, tail_ref, hc_ref, a_scr, b_scr):
    t = pl.program_id(2)
    tc, dc = x_ref.shape

    @pl.when(t == 0)
    def _():
        tail_ref[...] = jnp.zeros_like(tail_ref)
        hc_ref[...] = jnp.zeros_like(hc_ref)

    x = x_ref[...].astype(F32)
    xw = jnp.concatenate([tail_ref[...], x], axis=0)
    xc = bc_ref[...] + wc_ref[CONV_W - 1:CONV_W, :] * x
    for k in range(CONV_W - 1):
        d = CONV_W - 1 - k
        xc = xc + wc_ref[k:k + 1, :] * xw[8 - d:8 - d + tc, :]
    tail_ref[...] = x[tc - 8:, :]

    a, b = _rg_coeffs(xc, wrg_ref, brg_ref, _softplus_neg(lam_ref[...]))
    a_scr[...] = a
    b_scr[...] = b

    row = lax.broadcasted_iota(jnp.int32, (8, dc), 0)

    def body(k, hc):
        r0 = pl.multiple_of(k * 8, 8)
        av = a_scr[pl.ds(r0, 8), :]
        bv = b_scr[pl.ds(r0, 8), :]
        for s in (1, 2, 4):
            a_sh = jnp.where(row >= s, pltpu.roll(av, s, 0), 1.0)
            b_sh = jnp.where(row >= s, pltpu.roll(bv, s, 0), 0.0)
            bv = av * b_sh + bv
            av = av * a_sh
        h = av * hc + bv
        b_scr[pl.ds(r0, 8), :] = h
        return jnp.broadcast_to(h[7:8, :], (8, dc))

    hc = lax.fori_loop(0, tc // 8, body, hc_ref[...])
    hc_ref[...] = hc
    y_ref[...] = (b_scr[...] * _gelu_tanh(g_ref[...].astype(F32))).astype(BF16)

    @pl.when(t == pl.num_programs(2) - 1)
    def _():
        conv_ref[...] = x[tc - (CONV_W - 1):, :]
        h_ref[...] = hc[0:1, :]


def rglru_seq(pa, nb, seq, w_conv, b_conv, wrg, brg, lam, *, tc=512, dc=512):
    d = w_conv.shape[1]
    nc, nt = d // dc, seq // tc
    bpc = dc // RNN_BLOCK_W
    return pl.pallas_call(
        _rglru_seq_kernel,
        grid=(nb, nc, nt),
        in_specs=[pl.BlockSpec((tc, dc), lambda b, c, t: (b * nt + t, c)),
                  pl.BlockSpec((tc, dc), lambda b, c, t: (b * nt + t, nc + c)),
                  pl.BlockSpec((CONV_W, dc), lambda b, c, t: (0, c)),
                  pl.BlockSpec((1, dc), lambda b, c, t: (0, c)),
                  pl.BlockSpec((bpc, RNN_BLOCK_W, 2 * RNN_BLOCK_W), lambda b, c, t: (c, 0, 0)),
                  pl.BlockSpec((bpc, 1, 2 * RNN_BLOCK_W), lambda b, c, t: (c, 0, 0)),
                  pl.BlockSpec((1, dc), lambda b, c, t: (0, c))],
        out_specs=[pl.BlockSpec((tc, dc), lambda b, c, t: (b * nt + t, c)),
                   pl.BlockSpec((None, CONV_W - 1, dc), lambda b, c, t: (b, 0, c)),
                   pl.BlockSpec((None, 1, dc), lambda b, c, t: (b, 0, c))],
        out_shape=[jax.ShapeDtypeStruct((nb * seq, d), BF16),
                   jax.ShapeDtypeStruct((nb, CONV_W - 1, d), F32),
                   jax.ShapeDtypeStruct((nb, 1, d), F32)],
        scratch_shapes=[pltpu.VMEM((8, dc), F32), pltpu.VMEM((8, dc), F32),
                        pltpu.VMEM((tc, dc), F32), pltpu.VMEM((tc, dc), F32)],
        compiler_params=_cparams(("arbitrary", "arbitrary", "arbitrary")),
        name="rglru_seq",
    )(pa, pa, w_conv, b_conv, wrg, brg, lam)


def _rglru_step_kernel(x_ref, g_ref, prev_ref, h0_ref, wc_ref, bc_ref, wrg_ref, brg_ref, lam_ref,
                       y_ref, conv_ref, h_ref):
    nt = x_ref.shape[0]
    full = [prev_ref[k] for k in range(CONV_W - 1)] + [x_ref[k].astype(F32) for k in range(nt)]
    sp = _softplus_neg(lam_ref[...])
    h = h0_ref[...]
    for t in range(nt):
        xc = bc_ref[...] + wc_ref[0:1, :] * full[t]
        for k in range(1, CONV_W):
            xc = xc + wc_ref[k:k + 1, :] * full[t + k]
        a, b = _rg_coeffs(xc, wrg_ref, brg_ref, sp)
        h = a * h + b
        y_ref[t] = (h * _gelu_tanh(g_ref[t].astype(F32))).astype(BF16)
    for k in range(CONV_W - 1):
        conv_ref[k] = full[nt + k]
    h_ref[...] = h


def rglru_step(x4, g4, prev, h0, w_conv, b_conv, wrg, brg, lam, *, dc=512):
    nt, bs, d = x4.shape
    bpc = dc // RNN_BLOCK_W
    return pl.pallas_call(
        _rglru_step_kernel,
        grid=(d // dc,),
        in_specs=[pl.BlockSpec((nt, bs, dc), lambda c: (0, 0, c)),
                  pl.BlockSpec((nt, bs, dc), lambda c: (0, 0, c)),
                  pl.BlockSpec((CONV_W - 1, bs, dc), lambda c: (0, 0, c)),
                  pl.BlockSpec((bs, dc), lambda c: (0, c)),
                  pl.BlockSpec((CONV_W, dc), lambda c: (0, c)),
                  pl.BlockSpec((1, dc), lambda c: (0, c)),
                  pl.BlockSpec((bpc, RNN_BLOCK_W, 2 * RNN_BLOCK_W), lambda c: (c, 0, 0)),
                  pl.BlockSpec((bpc, 1, 2 * RNN_BLOCK_W), lambda c: (c, 0, 0)),
                  pl.BlockSpec((1, dc), lambda c: (0, c))],
        out_specs=[pl.BlockSpec((nt, bs, dc), lambda c: (0, 0, c)),
                   pl.BlockSpec((CONV_W - 1, bs, dc), lambda c: (0, 0, c)),
                   pl.BlockSpec((bs, dc), lambda c: (0, c))],
        out_shape=[jax.ShapeDtypeStruct((nt, bs, d), BF16),
                   jax.ShapeDtypeStruct((CONV_W - 1, bs, d), F32),
                   jax.ShapeDtypeStruct((bs, d), F32)],
        compiler_params=_cparams(("arbitrary",)),
        name="rglru_step",
    )(x4, g4, prev, h0, w_conv, b_conv, wrg, brg, lam)


def _rope_heads(qr, qrot, cos128, sin128, scale, out_ref, lane0):
    for hp in range(qr.shape[1] // 128):
        sl = slice(hp * 128, (hp + 1) * 128)
        v = ((qr[:, sl] * cos128 + qrot[:, sl] * sin128) * scale).astype(BF16)
        out_ref[2 * hp, :, lane0:lane0 + QK_ROPE] = v[:, :QK_ROPE]
        out_ref[2 * hp + 1, :, lane0:lane0 + QK_ROPE] = v[:, QK_ROPE:]


def _qkv_prompt_kernel(cq_ref, ckv_ref, wqn_ref, wqr_ref, wqrot_ref, wuk_ref, wuv_ref, cos_ref, sin_ref,
                       qn_ref, qr_ref, kn_ref, v_ref, *, scale):
    cq = cq_ref[...]
    qn_ref[...] = (_dot(cq, wqn_ref[...]) * scale).astype(BF16)
    _rope_heads(_dot(cq, wqr_ref[...]), _dot(cq, wqrot_ref[...]), cos_ref[...], sin_ref[...], scale, qr_ref, 0)
    cb = ckv_ref[...].astype(BF16)
    kn_ref[...] = _dot(cb, wuk_ref[...]).astype(BF16)
    v_ref[...] = _dot(cb, wuv_ref[...]).astype(BF16)


def qkv_prompt(cq, ckv, wqn, wqr, wqrot, wuk, wuv, cos128, sin128, *, seq, scale, tm=512):
    t, ql = cq.shape
    kvl = ckv.shape[1]
    hn, hr = wqn.shape[1], wqr.shape[1]
    nh = hr // QK_ROPE
    tps = seq // tm
    full = lambda a: pl.BlockSpec(a.shape, lambda i: (0,) * a.ndim)
    tab = pl.BlockSpec((tm, 128), lambda i: (i % tps, 0))
    return pl.pallas_call(
        functools.partial(_qkv_prompt_kernel, scale=scale),
        grid=(t // tm,),
        in_specs=[pl.BlockSpec((tm, ql), lambda i: (i, 0)), pl.BlockSpec((tm, kvl), lambda i: (i, 0)),
                  full(wqn), full(wqr), full(wqrot), full(wuk), full(wuv), tab, tab],
        out_specs=[pl.BlockSpec((tm, hn), lambda i: (i, 0)),
                   pl.BlockSpec((nh, tm, QK_ROPE), lambda i: (0, i, 0)),
                   pl.BlockSpec((tm, hn), lambda i: (i, 0)),
                   pl.BlockSpec((tm, wuv.shape[1]), lambda i: (i, 0))],
        out_shape=[jax.ShapeDtypeStruct((t, hn), BF16), jax.ShapeDtypeStruct((nh, t, QK_ROPE), BF16),
                   jax.ShapeDtypeStruct((t, hn), BF16), jax.ShapeDtypeStruct((t, wuv.shape[1]), BF16)],
        compiler_params=_cparams(("arbitrary",)),
        name="qkv_prompt",
    )(cq, ckv, wqn, wqr, wqrot, wuk, wuv, cos128, sin128)


def _q_sample_kernel(cq_ref, wqn_ref, wqr_ref, wqrot_ref, wukt_ref, cos_ref, sin_ref, q_ref, *, scale):
    cq = cq_ref[...]
    qn = _dot(cq, wqn_ref[...])
    kvl = wukt_ref.shape[2]
    for h in range(wukt_ref.shape[0]):
        ql = _dot(qn[:, h * QK_NOPE:(h + 1) * QK_NOPE].astype(BF16), wukt_ref[h]) * scale
        q_ref[h, :, :kvl] = ql.astype(BF16)
    _rope_heads(_dot(cq, wqr_ref[...]), _dot(cq, wqrot_ref[...]), cos_ref[...], sin_ref[...], scale, q_ref, kvl)


def q_sample(cq, wqn, wqr, wqrot, wukt, cos128, sin128, *, scale):
    t = cq.shape[0]
    nh, _, kvl = wukt.shape
    full = lambda a: pl.BlockSpec(a.shape, lambda i: (0,) * a.ndim)
    return pl.pallas_call(
        functools.partial(_q_sample_kernel, scale=scale),
        grid=(1,),
        in_specs=[full(cq), full(wqn), full(wqr), full(wqrot), full(wukt), full(cos128), full(sin128)],
        out_specs=pl.BlockSpec((nh, t, kvl + QK_ROPE), lambda i: (0, 0, 0)),
        out_shape=jax.ShapeDtypeStruct((nh, t, kvl + QK_ROPE), BF16),
        compiler_params=_cparams(("arbitrary",)),
        name="q_sample",
    )(cq, wqn, wqr, wqrot, wukt, cos128, sin128)


def _softmax_step(s, v, m_ref, l_ref, acc_ref):
    m_prev = m_ref[...]
    m_new = jnp.maximum(m_prev, jnp.max(s, axis=-1, keepdims=True))
    alpha = jnp.exp(m_prev - m_new)
    p = jnp.exp(s - m_new)
    l_ref[...] = alpha * l_ref[...] + jnp.sum(p, axis=-1, keepdims=True)
    acc_ref[...] = alpha * acc_ref[...] + _dot(p.astype(BF16), v)
    m_ref[...] = m_new


def _attn_prompt_kernel(qi_ref, kj_ref, qn_ref, qr_ref, kn_ref, kr_ref, v_ref, o_ref, m_ref, l_ref, acc_ref):
    p = pl.program_id(2)
    i, j = qi_ref[p], kj_ref[p]

    @pl.when(j == 0)
    def _():
        m_ref[...] = jnp.full_like(m_ref, NEG_BIG)
        l_ref[...] = jnp.zeros_like(l_ref)
        acc_ref[...] = jnp.zeros_like(acc_ref)

    q = jnp.concatenate([qn_ref[...], qr_ref[...]], axis=1)
    k = jnp.concatenate([kn_ref[...], kr_ref[...].astype(BF16)], axis=1)
    s = _dot_nt(q, k)

    @pl.when(j < i)
    def _():
        _softmax_step(s, v_ref[...], m_ref, l_ref, acc_ref)

    @pl.when(j == i)
    def _():
        row = lax.broadcasted_iota(jnp.int32, s.shape, 0)
        col = lax.broadcasted_iota(jnp.int32, s.shape, 1)
        _softmax_step(jnp.where(col <= row, s, NEG_BIG), v_ref[...], m_ref, l_ref, acc_ref)
        o_ref[...] = (acc_ref[...] / l_ref[...]).astype(BF16)


def attn_prompt(qn, qr, kn, kr, v, *, nb, seq, tq=512):
    t, hn = qn.shape
    nh = hn // QK_NOPE
    nq = seq // tq
    pairs = [(i, j) for i in range(nq) for j in range(i + 1)]
    qi = jnp.asarray([p[0] for p in pairs], jnp.int32)
    kj = jnp.asarray([p[1] for p in pairs], jnp.int32)
    grid_spec = pltpu.PrefetchScalarGridSpec(
        num_scalar_prefetch=2,
        grid=(nb, nh, len(pairs)),
        in_specs=[pl.BlockSpec((tq, QK_NOPE), lambda b, h, p, qi, kj: (b * nq + qi[p], h)),
                  pl.BlockSpec((None, tq, QK_ROPE), lambda b, h, p, qi, kj: (h, b * nq + qi[p], 0)),
                  pl.BlockSpec((tq, QK_NOPE), lambda b, h, p, qi, kj: (b * nq + kj[p], h)),
                  pl.BlockSpec((tq, QK_ROPE), lambda b, h, p, qi, kj: (b * nq + kj[p], 0)),
                  pl.BlockSpec((tq, QK_NOPE), lambda b, h, p, qi, kj: (b * nq + kj[p], h))],
        out_specs=pl.BlockSpec((tq, QK_NOPE), lambda b, h, p, qi, kj: (b * nq + qi[p], h)),
        scratch_shapes=[pltpu.VMEM((tq, 1), F32), pltpu.VMEM((tq, 1), F32), pltpu.VMEM((tq, QK_NOPE), F32)],
    )
    return pl.pallas_call(
        _attn_prompt_kernel,
        grid_spec=grid_spec,
        out_shape=jax.ShapeDtypeStruct((t, hn), BF16),
        compiler_params=_cparams(("arbitrary", "arbitrary", "arbitrary")),
        name="attn_prompt",
    )(qi, kj, qn, qr, kn, kr, v)


def _attn_sample_kernel(pt_ref, q_ref, *refs, pg, page, kvl, heads):
    ck_refs, kr_refs = refs[:pg], refs[pg:2 * pg]
    cn_ref, kn_ref, o_ref, kbuf, m_ref, l_ref, acc_ref = refs[2 * pg:]
    s_idx = pl.program_id(1)

    @pl.when(s_idx == 0)
    def _():
        m_ref[...] = jnp.full_like(m_ref, NEG_BIG)
        l_ref[...] = jnp.zeros_like(l_ref)
        acc_ref[...] = jnp.zeros_like(acc_ref)

    for g in range(pg):
        kbuf[g * page:(g + 1) * page, :kvl] = ck_refs[g][...].astype(BF16)
        kbuf[g * page:(g + 1) * page, kvl:] = kr_refs[g][...].astype(BF16)
    q = q_ref[...]
    _softmax_step(_dot_nt(q, kbuf[...]), kbuf[:, :kvl], m_ref, l_ref, acc_ref)

    @pl.when(s_idx == pl.num_programs(1) - 1)
    def _():
        nt = cn_ref.shape[0]
        knew = jnp.concatenate([cn_ref[...], kn_ref[...]], axis=1)
        knew = jnp.concatenate([knew, jnp.zeros((8 - nt, knew.shape[1]), F32)], axis=0).astype(BF16)
        s = _dot_nt(q, knew)
        row = lax.broadcasted_iota(jnp.int32, s.shape, 0)
        col = lax.broadcasted_iota(jnp.int32, s.shape, 1)
        s = jnp.where(col * heads <= row, s, NEG_BIG)
        _softmax_step(s, knew[:, :kvl], m_ref, l_ref, acc_ref)
        o_ref[...] = (acc_ref[...] / l_ref[...]).astype(BF16)


def attn_sample(q, cache_ckv, cache_krope, ckv_new, kr_new, page_table, *, layer, heads, pg=8):
    bs, nq, qd = q.shape
    npages = page_table.shape[1]
    page, kvl = cache_ckv.shape[2], cache_ckv.shape[3]
    nt = ckv_new.shape[1]
    assert nt <= 8 and npages % pg == 0

    def page_spec(width, g):
        return pl.BlockSpec((None, None, page, width),
                            lambda b, s, pt, g=g: (layer, pt[b * npages + s * pg + g], 0, 0))

    grid_spec = pltpu.PrefetchScalarGridSpec(
        num_scalar_prefetch=1,
        grid=(bs, npages // pg),
        in_specs=[pl.BlockSpec((None, nq, qd), lambda b, s, pt: (b, 0, 0))]
                 + [page_spec(kvl, g) for g in range(pg)]
                 + [page_spec(QK_ROPE, g) for g in range(pg)]
                 + [pl.BlockSpec((None, nt, kvl), lambda b, s, pt: (b, 0, 0)),
                    pl.BlockSpec((None, nt, QK_ROPE), lambda b, s, pt: (b, 0, 0))],
        out_specs=pl.BlockSpec((None, nq, kvl), lambda b, s, pt: (b, 0, 0)),
        scratch_shapes=[pltpu.VMEM((pg * page, qd), BF16), pltpu.VMEM((nq, 1), F32), pltpu.VMEM((nq, 1), F32),
                        pltpu.VMEM((nq, kvl), F32)],
    )
    return pl.pallas_call(
        functools.partial(_attn_sample_kernel, pg=pg, page=page, kvl=kvl, heads=heads),
        grid_spec=grid_spec,
        out_shape=jax.ShapeDtypeStruct((bs, nq, kvl), BF16),
        compiler_params=_cparams(("arbitrary", "arbitrary")),
        name="attn_sample",
    )(page_table.reshape(-1), q, *([cache_ckv] * pg), *([cache_krope] * pg), ckv_new, kr_new)


def _uv_kernel(o_ref, w_ref, y_ref):
    y_ref[...] = _dot(o_ref[...], w_ref[...]).astype(BF16)


def uv_sample(o, wuv_h):
    t = o.shape[0]
    nh, kvl, vd = wuv_h.shape
    return pl.pallas_call(
        _uv_kernel,
        grid=(nh,),
        in_specs=[pl.BlockSpec((t, kvl), lambda h: (0, h)), pl.BlockSpec((None, kvl, vd), lambda h: (h, 0, 0))],
        out_specs=pl.BlockSpec((t, vd), lambda h: (0, h)),
        out_shape=jax.ShapeDtypeStruct((t, nh * vd), BF16),
        compiler_params=_cparams(("arbitrary",)),
        name="uv_sample",
    )(o, wuv_h)


def _merge_route_kernel(x_ref, ya_ref, yb_ref, ga_ref, gb_ref, g1_ref, sc_ref, sh_ref, wo_ref, gn_ref,
                        wrh_ref, wrl_ref, br_ref, cin_ref, x1_ref, hn_ref, rt_ref, cnt_ref):
    @pl.when(pl.program_id(0) == 0)
    def _():
        cnt_ref[...] = cin_ref[...]

    merged = (jax.nn.sigmoid(ga_ref[...].astype(F32)) * ya_ref[...].astype(F32)
              + jax.nn.sigmoid(gb_ref[...].astype(F32)) * yb_ref[...].astype(F32))
    x1 = x_ref[...] + g1_ref[...] * _dot(merged.astype(BF16), wo_ref[...])
    x1_ref[...] = x1
    hn = _rms(x1, gn_ref[...]) * (1.0 + sc_ref[...]) + sh_ref[...]
    hn_ref[...] = hn

    hh = hn.astype(BF16)
    hl = (hn - hh.astype(F32)).astype(BF16)
    logits = _dot(hh, wrh_ref[...]) + (_dot(hh, wrl_ref[...]) + _dot(hl, wrh_ref[...])) + br_ref[...]
    tm = logits.shape[0]
    lane = lax.broadcasted_iota(jnp.int32, logits.shape, 1)

    def first_argmax(v):
        mx = jnp.max(v, axis=-1, keepdims=True)
        return mx, jnp.min(jnp.where(v == mx, lane, ROUTE_LANES), axis=-1, keepdims=True)

    gl = jnp.where(lane < N_GROUPS, logits, NEG_BIG)
    gmax, gidx = first_argmax(gl)
    p_top = 1.0 / jnp.sum(jnp.exp(gl - gmax), axis=-1, keepdims=True)
    lo = N_GROUPS + gidx * EXPERTS_PER_GROUP
    el = jnp.where((lane >= lo) & (lane < lo + EXPERTS_PER_GROUP), logits, NEG_BIG)
    m1, l1 = first_argmax(el)
    m2, l2 = first_argmax(jnp.where(lane == l1, NEG_BIG, el))
    e21 = jnp.exp(m2 - m1)
    w1 = p_top / (1.0 + e21)
    w2 = p_top * e21 / (1.0 + e21)
    e1, e2 = l1 - N_GROUPS, l2 - N_GROUPS

    oh1, oh2 = lane == e1, lane == e2
    ohs = jnp.where(oh1 | oh2, 1.0, 0.0)
    tri = jnp.where(lax.broadcasted_iota(jnp.int32, (tm, tm), 0) > lax.broadcasted_iota(jnp.int32, (tm, tm), 1),
                    1.0, 0.0).astype(BF16)
    before = _dot(tri, ohs.astype(BF16)) + cnt_ref[...]
    r1 = jnp.sum(jnp.where(oh1, before, 0.0), axis=-1, keepdims=True)
    r2 = jnp.sum(jnp.where(oh2, before, 0.0), axis=-1, keepdims=True)
    cnt_ref[...] = cnt_ref[...] + jnp.sum(ohs, axis=0, keepdims=True)

    rt = jnp.where(lane == 0, e1.astype(F32), 0.0)
    rt = jnp.where(lane == 1, e2.astype(F32), rt)
    rt = jnp.where(lane == 2, w1, rt)
    rt = jnp.where(lane == 3, w2, rt)
    rt = jnp.where(lane == 4, r1, rt)
    rt_ref[...] = jnp.where(lane == 5, r2, rt)


def merge_route(x, ya, yb, pa, g1, sc2, sh2, wo, gn2, wrh, wrl, br, cnt_in, *, per_seq, tm=256):
    t, d = x.shape
    if per_seq:
        tps = per_seq // tm
        mod_spec = pl.BlockSpec((None, 1, d), lambda i: (i // tps, 0, 0))
    else:
        mod_spec = pl.BlockSpec((tm, d), lambda i: (i, 0))
    row = pl.BlockSpec((tm, d), lambda i: (i, 0))
    full = lambda a: pl.BlockSpec(a.shape, lambda i: (0,) * a.ndim)
    return pl.pallas_call(
        _merge_route_kernel,
        grid=(t // tm,),
        in_specs=[row, row, row, pl.BlockSpec((tm, d), lambda i: (i, 2)), pl.BlockSpec((tm, d), lambda i: (i, 3)),
                  mod_spec, mod_spec, mod_spec, full(wo), full(gn2), full(wrh), full(wrl), full(br), full(cnt_in)],
        out_specs=[row, row, pl.BlockSpec((tm, ROUTE_LANES), lambda i: (i, 0)),
                   pl.BlockSpec((1, ROUTE_LANES), lambda i: (0, 0))],
        out_shape=[jax.ShapeDtypeStruct((t, d), F32), jax.ShapeDtypeStruct((t, d), F32),
                   jax.ShapeDtypeStruct((t, ROUTE_LANES), F32), jax.ShapeDtypeStruct((1, ROUTE_LANES), F32)],
        compiler_params=_cparams(("arbitrary",)),
        name="merge_route",
    )(x, ya, yb, pa, pa, g1, sc2, sh2, wo, gn2, wrh, wrl, br, cnt_in)


def _row_copy(src, dst, sem):
    return pltpu.make_async_copy(src, dst, sem)


def _dispatch_kernel(pos_ref, h_ref, xs_in_ref, xs_ref, sem):
    del xs_in_ref
    i = pl.program_id(0)
    tm = h_ref.shape[0]

    def issue(r, c):
        for k in range(2):
            p = pos_ref[(i * tm + r) * 2 + k]
            _row_copy(h_ref.at[pl.ds(r, 1), :], xs_ref.at[pl.ds(p, 1), :], sem).start()
        return c

    lax.fori_loop(0, tm, issue, 0)

    def wait(r, c):
        for k in range(2):
            _row_copy(h_ref.at[pl.ds(0, 1), :], xs_ref.at[pl.ds(0, 1), :], sem).wait()
        return c

    lax.fori_loop(0, tm, wait, 0)


def dispatch(pos, hn2, xs, *, tm=256):
    t, d = hn2.shape
    grid_spec = pltpu.PrefetchScalarGridSpec(
        num_scalar_prefetch=1,
        grid=(t // tm,),
        in_specs=[pl.BlockSpec((tm, d), lambda i, pos: (i, 0)), pl.BlockSpec(memory_space=pl.ANY)],
        out_specs=pl.BlockSpec(memory_space=pl.ANY),
        scratch_shapes=[pltpu.SemaphoreType.DMA(())],
    )
    return pl.pallas_call(
        _dispatch_kernel,
        grid_spec=grid_spec,
        out_shape=jax.ShapeDtypeStruct(xs.shape, xs.dtype),
        input_output_aliases={2: 0},
        compiler_params=_cparams(("arbitrary",)),
        name="dispatch",
    )(pos, hn2, xs)


def _gmm_kernel(te_ref, first_ref, nt_ref, x_ref, wg_ref, wu_ref, wd_ref, y_ref, wgb, wub, wdb):
    i = pl.program_id(0)

    @pl.when(i < nt_ref[0])
    def _():
        @pl.when(first_ref[i] == 1)
        def _():
            wgb[...] = wg_ref[...].astype(BF16)
            wub[...] = wu_ref[...].astype(BF16)
            wdb[...] = wd_ref[...].astype(BF16)

        x = x_ref[...].astype(BF16)
        g = _dot(x, wgb[...])
        u = _dot(x, wub[...])
        y_ref[...] = _dot((g * jax.nn.sigmoid(g) * u).astype(BF16), wdb[...])

    @pl.when(i >= nt_ref[0])
    def _():
        y_ref[...] = jnp.zeros_like(y_ref)


def expert_ffn(tile_e, tile_first, n_tiles, xs, w_gate, w_up, w_down, *, layer, tm):
    s, d = xs.shape
    de = w_gate.shape[3]
    nt_max = s // tm
    last = lambda i, nt: jnp.minimum(i, nt[0] - 1)
    grid_spec = pltpu.PrefetchScalarGridSpec(
        num_scalar_prefetch=3,
        grid=(nt_max,),
        in_specs=[pl.BlockSpec((tm, d), lambda i, te, tf, nt: (last(i, nt), 0)),
                  pl.BlockSpec((None, None, d, de), lambda i, te, tf, nt: (layer, te[i], 0, 0)),
                  pl.BlockSpec((None, None, d, de), lambda i, te, tf, nt: (layer, te[i], 0, 0)),
                  pl.BlockSpec((None, None, de, d), lambda i, te, tf, nt: (layer, te[i], 0, 0))],
        out_specs=pl.BlockSpec((tm, d), lambda i, te, tf, nt: (i, 0)),
        scratch_shapes=[pltpu.VMEM((d, de), BF16), pltpu.VMEM((d, de), BF16), pltpu.VMEM((de, d), BF16)],
    )
    return pl.pallas_call(
        _gmm_kernel,
        grid_spec=grid_spec,
        out_shape=jax.ShapeDtypeStruct((s, d), F32),
        compiler_params=_cparams(("arbitrary",)),
        name="expert_ffn",
    )(tile_e, tile_first, n_tiles, xs, w_gate, w_up, w_down)


def _combine_kernel(pos_ref, x1_ref, g2_ref, rt_ref, gf_ref, ys_ref, o_ref, gbuf, sem):
    i = pl.program_id(0)
    tm = x1_ref.shape[0]

    def issue(r, c):
        for k in range(2):
            p = pos_ref[(i * tm + r) * 2 + k]
            _row_copy(ys_ref.at[pl.ds(p, 1), :], gbuf.at[k, pl.ds(r, 1), :], sem).start()
        return c

    lax.fori_loop(0, tm, issue, 0)

    def wait(r, c):
        for k in range(2):
            _row_copy(ys_ref.at[pl.ds(0, 1), :], gbuf.at[0, pl.ds(0, 1), :], sem).wait()
        return c

    lax.fori_loop(0, tm, wait, 0)
    rt = rt_ref[...]
    moe = rt[:, 2:3] * gbuf[0] + rt[:, 3:4] * gbuf[1]
    o_ref[...] = _rms(x1_ref[...] + g2_ref[...] * moe, gf_ref[...])


def combine_final(pos, x1, g2, rt, gf, ys, *, per_seq, tm=256):
    t, d = x1.shape
    if per_seq:
        tps = per_seq // tm
        mod_spec = pl.BlockSpec((None, 1, d), lambda i, pos: (i // tps, 0, 0))
    else:
        mod_spec = pl.BlockSpec((tm, d), lambda i, pos: (i, 0))
    grid_spec = pltpu.PrefetchScalarGridSpec(
        num_scalar_prefetch=1,
        grid=(t // tm,),
        in_specs=[pl.BlockSpec((tm, d), lambda i, pos: (i, 0)), mod_spec,
                  pl.BlockSpec((tm, ROUTE_LANES), lambda i, pos: (i, 0)),
                  pl.BlockSpec((1, d), lambda i, pos: (0, 0)), pl.BlockSpec(memory_space=pl.ANY)],
        out_specs=pl.BlockSpec((tm, d), lambda i, pos: (i, 0)),
        scratch_shapes=[pltpu.VMEM((2, tm, d), F32), pltpu.SemaphoreType.DMA(())],
    )
    return pl.pallas_call(
        _combine_kernel,
        grid_spec=grid_spec,
        out_shape=jax.ShapeDtypeStruct((t, d), F32),
        compiler_params=_cparams(("arbitrary",)),
        name="combine_final",
    )(pos, x1, g2, rt, gf, ys)


def _rope_tables(pos):
    half = QK_ROPE // 2
    freqs = ROPE_THETA ** (-jnp.arange(half, dtype=F32) / half)
    ang = pos[:, None] * freqs[None, :]
    cos, sin = jnp.cos(ang), jnp.sin(ang)
    return jnp.concatenate([cos, cos], axis=1), jnp.concatenate([sin, sin], axis=1)


def _rotate_half_cols(w):
    lead = w.shape[0]
    w3 = w.reshape(lead, -1, QK_ROPE)
    half = QK_ROPE // 2
    return jnp.concatenate([-w3[..., half:], w3[..., :half]], axis=-1).reshape(lead, -1)


def _pick_tile(n, pref):
    t = min(pref, n)
    while n % t:
        t //= 2
    return t


def kernel(x_prompt, x_sample, cache_ckv, cache_krope, state_conv, state_h, page_table, c_prompt, c_sample, w_ada, b_ada, g_norm1, g_norm2, w_in, w_conv, b_conv, w_rg_a, b_rg_a, w_rg_x, b_rg_x, rg_lambda, g_q, w_uq, g_kv, w_uk, w_uv, w_o, w_group, b_group, w_router, b_router, w_gate, w_up, w_down, g_final):
    nb, seq, d = x_prompt.shape
    bs, nt, _ = x_sample.shape
    depth = w_ada.shape[0]
    d_rnn = w_conv.shape[2]
    q_lora, nh, qk_dim = w_uq.shape[1:]
    kv_lora = w_uk.shape[1]
    v_dim = w_uv.shape[3]
    past = page_table.shape[1] * cache_ckv.shape[2]
    scale = float(qk_dim) ** -0.5
    tp, ts = nb * seq, bs * nt
    assert d_rnn == d and qk_dim == QK_NOPE + QK_ROPE and v_dim == QK_NOPE and nh * v_dim == d

    xp = x_prompt.reshape(tp, d)
    xs_tok = x_sample.reshape(ts, d)
    cos_p, sin_p = _rope_tables(jnp.arange(seq, dtype=F32))
    cos_s, sin_s = _rope_tables(jnp.tile(past + jnp.arange(nt, dtype=F32), bs))
    dbl = lambda a: jnp.concatenate([a, a], axis=1)
    c_all = jnp.concatenate([c_prompt, c_sample], axis=0)
    n_c = c_all.shape[0]
    c_all = jnp.pad(c_all, ((0, (-n_c) % 8), (0, 0)))

    outs_p, outs_s = [], []
    for l in range(depth):
        o1, o2, o3, o4, o5, o6 = np.cumsum([d_rnn, d_rnn, q_lora, kv_lora, QK_ROPE, d]).tolist()
        wi = w_in[l]
        wa = jnp.concatenate([wi[:, :o2], wi[:, o5:]], axis=1).astype(BF16)
        w_kr = wi[:, o4:o5]
        wb = jnp.concatenate([wi[:, o2:o4], w_kr, _rotate_half_cols(w_kr)], axis=1).astype(BF16)
        wrg = jnp.concatenate([w_rg_a[l], w_rg_x[l]], axis=2).astype(BF16)
        brg = jnp.concatenate([b_rg_a[l], b_rg_x[l]], axis=1)[:, None, :]
        wq = w_uq[l]
        wqn = wq[:, :, :QK_NOPE].reshape(q_lora, nh * QK_NOPE).astype(BF16)
        wqr_f = wq[:, :, QK_NOPE:].reshape(q_lora, nh * QK_ROPE)
        wqr, wqrot = wqr_f.astype(BF16), _rotate_half_cols(wqr_f).astype(BF16)
        wuk_flat = w_uk[l].reshape(kv_lora, nh * QK_NOPE).astype(BF16)
        wuv_flat = w_uv[l].reshape(kv_lora, nh * v_dim).astype(BF16)
        wukt = jnp.transpose(w_uk[l], (1, 2, 0)).astype(BF16)
        wuv_h = jnp.transpose(w_uv[l], (1, 0, 2)).astype(BF16)
        wo = w_o[l].astype(BF16)
        wr = jnp.concatenate([w_group[l], w_router[l]], axis=1)
        wr = jnp.pad(wr, ((0, 0), (0, ROUTE_LANES - wr.shape[1])))
        wrh = wr.astype(BF16)
        wrl = (wr - wrh.astype(F32)).astype(BF16)
        br = jnp.pad(jnp.concatenate([b_group[l], b_router[l]]), (0, ROUTE_LANES - N_GROUPS - N_EXPERTS))[None, :]
        row = lambda a: a[None, :]

        mod = ada_mod(c_all, w_ada[l], b_ada[l])
        mods_p = [m[:nb, None, :] for m in jnp.split(mod, 6, axis=1)]
        mods_s = [jnp.repeat(m[nb:nb + bs], nt, axis=0) for m in jnp.split(mod, 6, axis=1)]

        tm_p = _pick_tile(seq, 512)
        pa_p, cq_p, ckv_p, kr_p = in_proj(xp, mods_p[1], mods_p[0], row(g_norm1[l]), wa, wb, row(g_q[l]),
                                          row(g_kv[l]), cos_p, sin_p, per_seq=seq, tm=tm_p)
        tm_s = _pick_tile(ts, 512)
        pa_s, cq_s, ckv_s, kr_s = in_proj(xs_tok, mods_s[1], mods_s[0], row(g_norm1[l]), wa, wb, row(g_q[l]),
                                          row(g_kv[l]), cos_s, sin_s, per_seq=0, tm=tm_s)

        ya_p, conv_p, h_p = rglru_seq(pa_p, nb, seq, w_conv[l], row(b_conv[l]), wrg, brg, row(rg_lambda[l]),
                                      tc=_pick_tile(seq, 512))
        tmajor = lambda a: jnp.transpose(a.reshape(bs, nt, -1), (1, 0, 2))
        ya_s4, conv_s, h_s = rglru_step(tmajor(pa_s[:, :d_rnn]), tmajor(pa_s[:, d_rnn:2 * d_rnn]),
                                        jnp.transpose(state_conv[l], (1, 0, 2)), state_h[l],
                                        w_conv[l], row(b_conv[l]), wrg, brg, row(rg_lambda[l]))
        ya_s = jnp.transpose(ya_s4, (1, 0, 2)).reshape(ts, d)
        conv_s = jnp.transpose(conv_s, (1, 0, 2))

        qn_p, qr_p, kn_p, v_p = qkv_prompt(cq_p, ckv_p, wqn, wqr, wqrot, wuk_flat, wuv_flat, dbl(cos_p), dbl(sin_p),
                                           seq=seq, scale=scale, tm=tm_p)
        yb_p = attn_prompt(qn_p, qr_p, kn_p, kr_p, v_p, nb=nb, seq=seq, tq=tm_p)
        q_s = q_sample(cq_s, wqn, wqr, wqrot, wukt, dbl(cos_s), dbl(sin_s), scale=scale)
        q_s = jnp.transpose(q_s, (1, 0, 2)).reshape(bs, nt * nh, kv_lora + QK_ROPE)
        o_s = attn_sample(q_s, cache_ckv, cache_krope, ckv_s.reshape(bs, nt, kv_lora), kr_s.reshape(bs, nt, QK_ROPE),
                          page_table, layer=l, heads=nh, pg=_pick_tile(page_table.shape[1], 8))
        yb_s = uv_sample(o_s.reshape(ts, nh * kv_lora), wuv_h)

        tm_r = _pick_tile(seq, 256)
        cnt0 = jnp.zeros((1, ROUTE_LANES), F32)
        x1_p, hn_p, rt_p, cnt1 = merge_route(xp, ya_p, yb_p, pa_p, mods_p[2], mods_p[4], mods_p[3], wo,
                                             row(g_norm2[l]), wrh, wrl, br, cnt0, per_seq=seq, tm=tm_r)
        tm_rs = _pick_tile(ts, 256)
        x1_s, hn_s, rt_s, cnt2 = merge_route(xs_tok, ya_s, yb_s, pa_s, mods_s[2], mods_s[4], mods_s[3], wo,
                                             row(g_norm2[l]), wrh, wrl, br, cnt1, per_seq=0, tm=tm_rs)

        tmg = 256
        counts = cnt2[0, :N_EXPERTS].astype(jnp.int32)
        tiles_e = (counts + tmg - 1) // tmg
        tile_end = jnp.cumsum(tiles_e)
        offs = (tile_end - tiles_e) * tmg
        nt_max = (2 * (tp + ts)) // tmg + N_EXPERTS
        tile_ids = jnp.arange(nt_max, dtype=jnp.int32)
        tile_e = jnp.minimum(jnp.searchsorted(tile_end, tile_ids, side="right"), N_EXPERTS - 1).astype(jnp.int32)
        n_tiles = tile_end[-1:].astype(jnp.int32)
        tile_first = jnp.concatenate([jnp.ones((1,), jnp.int32), (tile_e[1:] != tile_e[:-1]).astype(jnp.int32)])

        def slot_pos(rt):
            e = rt[:, 0:2].astype(jnp.int32)
            return (offs[e] + rt[:, 4:6].astype(jnp.int32)).reshape(-1)

        pos_p, pos_s = slot_pos(rt_p), slot_pos(rt_s)
        xs_buf = jnp.zeros((nt_max * tmg, d), F32)
        xs_buf = dispatch(pos_p, hn_p, xs_buf, tm=tm_r)
        xs_buf = dispatch(pos_s, hn_s, xs_buf, tm=tm_rs)
        ys_buf = expert_ffn(tile_e, tile_first, n_tiles, xs_buf, w_gate, w_up, w_down, layer=l, tm=tmg)

        last = l == depth - 1
        gf = row(g_final) if last else None
        assert last, "final norm is fused into the last layer's combine; DEPTH > 1 needs an un-normed variant"
        xp = combine_final(pos_p, x1_p, mods_p[5], rt_p, gf, ys_buf, per_seq=seq, tm=tm_r)
        xs_tok = combine_final(pos_s, x1_s, mods_s[5], rt_s, gf, ys_buf, per_seq=0, tm=tm_rs)

        outs_p.append((ckv_p.reshape(nb, seq, kv_lora), kr_p.reshape(nb, seq, QK_ROPE), conv_p, h_p.reshape(nb, d)))
        outs_s.append((ckv_s.reshape(bs, nt, kv_lora), kr_s.reshape(bs, nt, QK_ROPE), conv_s, h_s))

    stack = lambda outs, k: jnp.stack([o[k] for o in outs])
    return (xp.reshape(nb, seq, d), xs_tok.reshape(bs, nt, d),
            stack(outs_p, 0), stack(outs_p, 1), stack(outs_p, 2), stack(outs_p, 3),
            stack(outs_s, 0), stack(outs_s, 1), stack(outs_s, 2), stack(outs_s, 3))
```

```python
import functools

import jax
import jax.numpy as jnp
import numpy as np
from jax import lax
from jax.experimental import pallas as pl
from jax.experimental.pallas import tpu as pltpu

F32 = jnp.float32
BF16 = jnp.bfloat16

EPS = 1e-6
RG_C = 8.0
RNN_BLOCK_W = 128
CONV_W = 4
QK_NOPE = 128
QK_ROPE = 64
ROPE_THETA = 10000.0
N_GROUPS = 4
EXPERTS_PER_GROUP = 8
N_EXPERTS = N_GROUPS * EXPERTS_PER_GROUP
ROUTE_LANES = 128
NEG_BIG = -1e30
LOG2_E = 1.4426950408889634
VMEM_LIMIT = 56 * 1024 * 1024


def _cparams(sem, vmem=VMEM_LIMIT):
    return pltpu.CompilerParams(dimension_semantics=sem, vmem_limit_bytes=vmem)


def _rms(x, g):
    return x * lax.rsqrt(jnp.mean(x * x, axis=-1, keepdims=True) + EPS) * g


def _gelu_tanh(x):
    return 0.5 * x * (1.0 + jnp.tanh(0.7978845608028654 * (x + 0.044715 * x * x * x)))


def _dot(a, b):
    return jnp.dot(a, b, preferred_element_type=F32)


def _dot_nt(a, b):
    return lax.dot_general(a, b, (((1,), (1,)), ((), ())), preferred_element_type=F32)


def _ada_kernel(c_ref, w_ref, b_ref, o_ref):
    c = c_ref[...]
    s = (c * jax.nn.sigmoid(c)).astype(BF16)
    o_ref[...] = _dot(s, w_ref[...].astype(BF16)) + b_ref[...]


def ada_mod(c_all, w_ada, b_ada, tn=1024):
    m, d = c_all.shape
    n = w_ada.shape[1]
    return pl.pallas_call(
        _ada_kernel,
        grid=(n // tn,),
        in_specs=[pl.BlockSpec((m, d), lambda j: (0, 0)),
                  pl.BlockSpec((d, tn), lambda j: (0, j)),
                  pl.BlockSpec((1, tn), lambda j: (0, j))],
        out_specs=pl.BlockSpec((m, tn), lambda j: (0, j)),
        out_shape=jax.ShapeDtypeStruct((m, n), F32),
        compiler_params=_cparams(("arbitrary",)),
        name="ada_mod",
    )(c_all, w_ada, b_ada.reshape(1, n))


def _inproj_kernel(x_ref, sc_ref, sh_ref, g_ref, wa_ref, wb_ref, gq_ref, gkv_ref, cos_ref, sin_ref,
                   pa_ref, cq_ref, ckv_ref, kr_ref, hn_ref, *, q_lora, kv_lora):
    @pl.when(pl.program_id(1) == 0)
    def _():
        hn = _rms(x_ref[...], g_ref[...]) * (1.0 + sc_ref[...]) + sh_ref[...]
        hb = hn.astype(BF16)
        hn_ref[...] = hb
        pb = _dot(hb, wb_ref[...])
        o1 = q_lora
        o2 = o1 + kv_lora
        cq_ref[...] = _rms(pb[:, :o1], gq_ref[...]).astype(BF16)
        ckv_ref[...] = _rms(pb[:, o1:o2], gkv_ref[...])
        kr_ref[...] = pb[:, o2:o2 + QK_ROPE] * cos_ref[...] + pb[:, o2 + QK_ROPE:o2 + 2 * QK_ROPE] * sin_ref[...]

    pa_ref[...] = _dot(hn_ref[...], wa_ref[...]).astype(BF16)


def in_proj(x, sc, sh, g1, wa, wb, gq, gkv, cos64, sin64, *, per_seq, tm, tn=1024):
    t, d = x.shape
    na = wa.shape[1]
    q_lora, kv_lora = gq.shape[1], gkv.shape[1]
    if per_seq:
        tps = per_seq // tm
        mod_spec = pl.BlockSpec((None, 1, d), lambda i, j: (i // tps, 0, 0))
        tab_spec = pl.BlockSpec((tm, QK_ROPE), lambda i, j: (i % tps, 0))
    else:
        mod_spec = pl.BlockSpec((tm, d), lambda i, j: (i, 0))
        tab_spec = pl.BlockSpec((tm, QK_ROPE), lambda i, j: (i, 0))
    row = lambda w: pl.BlockSpec((tm, w), lambda i, j: (i, 0))
    full = lambda a: pl.BlockSpec(a.shape, lambda i, j: (0,) * a.ndim)
    return pl.pallas_call(
        functools.partial(_inproj_kernel, q_lora=q_lora, kv_lora=kv_lora),
        grid=(t // tm, na // tn),
        in_specs=[row(d), mod_spec, mod_spec, full(g1),
                  pl.BlockSpec((d, tn), lambda i, j: (0, j)), full(wb), full(gq), full(gkv), tab_spec, tab_spec],
        out_specs=[pl.BlockSpec((tm, tn), lambda i, j: (i, j)), row(q_lora), row(kv_lora), row(QK_ROPE)],
        out_shape=[jax.ShapeDtypeStruct((t, na), BF16), jax.ShapeDtypeStruct((t, q_lora), BF16),
                   jax.ShapeDtypeStruct((t, kv_lora), F32), jax.ShapeDtypeStruct((t, QK_ROPE), F32)],
        scratch_shapes=[pltpu.VMEM((tm, d), BF16)],
        compiler_params=_cparams(("arbitrary", "arbitrary")),
        name="in_proj",
    )(x, sc, sh, g1, wa, wb, gq, gkv, cos64, sin64)


def _rg_coeffs(xc, wrg_ref, brg_ref, sp):
    a_parts, b_parts = [], []
    for n in range(xc.shape[1] // RNN_BLOCK_W):
        sl = slice(n * RNN_BLOCK_W, (n + 1) * RNN_BLOCK_W)
        xb = xc[:, sl]
        z = _dot(xb.astype(BF16), wrg_ref[n]) + brg_ref[n]
        r = jax.nn.sigmoid(z[:, :RNN_BLOCK_W])
        i = jax.nn.sigmoid(z[:, RNN_BLOCK_W:])
        a = jnp.exp(-RG_C * r * sp[:, sl])
        a_parts.append(a)
        b_parts.append(jnp.sqrt(1.0 - a * a) * i * xb)
    return jnp.concatenate(a_parts, axis=1), jnp.concatenate(b_parts, axis=1)


def _softplus_neg(lam):
    return jnp.maximum(-lam, 0.0) + jnp.log1p(jnp.exp(-jnp.abs(lam)))


def _rglru_seq_kernel(x_ref, g_ref, wc_ref, bc_ref, wrg_ref, brg_ref, lam_ref,
                      y_ref, conv_ref, h_ref, tail_ref, hc_ref, a_scr, b_scr):
    t = pl.program_id(2)
    tc, dc = x_ref.shape

    @pl.when(t == 0)
    def _():
        tail_ref[...] = jnp.zeros_like(tail_ref)
        hc_ref[...] = jnp.zeros_like(hc_ref)

    x = x_ref[...].astype(F32)
    xw = jnp.concatenate([tail_ref[...], x], axis=0)
    xc = bc_ref[...] + wc_ref[CONV_W - 1:CONV_W, :] * x
    for k in range(CONV_W - 1):
        d = CONV_W - 1 - k
        xc = xc + wc_ref[k:k + 1, :] * xw[8 - d:8 - d + tc, :]
    tail_ref[...] = x[tc - 8:, :]

    a, b = _rg_coeffs(xc, wrg_ref, brg_ref, _softplus_neg(lam_ref[...]))
    a_scr[...] = a
    b_scr[...] = b

    row = lax.broadcasted_iota(jnp.int32, (8, dc), 0)

    def body(k, hc):
        r0 = pl.multiple_of(k * 8, 8)
        av = a_scr[pl.ds(r0, 8), :]
        bv = b_scr[pl.ds(r0, 8), :]
        for s in (1, 2, 4):
            a_sh = jnp.where(row >= s, pltpu.roll(av, s, 0), 1.0)
            b_sh = jnp.where(row >= s, pltpu.roll(bv, s, 0), 0.0)
            bv = av * b_sh + bv
            av = av * a_sh
        h = av * hc + bv
        b_scr[pl.ds(r0, 8), :] = h
        return jnp.broadcast_to(h[7:8, :], (8, dc))

    hc = lax.fori_loop(0, tc // 8, body, hc_ref[...])
    hc_ref[...] = hc
    y_ref[...] = (b_scr[...] * _gelu_tanh(g_ref[...].astype(F32))).astype(BF16)

    @pl.when(t == pl.num_programs(2) - 1)
    def _():
        conv_ref[...] = x[tc - (CONV_W - 1):, :]
        h_ref[...] = hc[0:1, :]


def rglru_seq(pa, nb, seq, w_conv, b_conv, wrg, brg, lam, *, tc=512, dc=512):
    d = w_conv.shape[1]
    nc, nt = d // dc, seq // tc
    bpc = dc // RNN_BLOCK_W
    return pl.pallas_call(
        _rglru_seq_kernel,
        grid=(nb, nc, nt),
        in_specs=[pl.BlockSpec((tc, dc), lambda b, c, t: (b * nt + t, c)),
                  pl.BlockSpec((tc, dc), lambda b, c, t: (b * nt + t, nc + c)),
                  pl.BlockSpec((CONV_W, dc), lambda b, c, t: (0, c)),
                  pl.BlockSpec((1, dc), lambda b, c, t: (0, c)),
                  pl.BlockSpec((bpc, RNN_BLOCK_W, 2 * RNN_BLOCK_W), lambda b, c, t: (c, 0, 0)),
                  pl.BlockSpec((bpc, 1, 2 * RNN_BLOCK_W), lambda b, c, t: (c, 0, 0)),
                  pl.BlockSpec((1, dc), lambda b, c, t: (0, c))],
        out_specs=[pl.BlockSpec((tc, dc), lambda b, c, t: (b * nt + t, c)),
                   pl.BlockSpec((None, CONV_W - 1, dc), lambda b, c, t: (b, 0, c)),
                   pl.BlockSpec((None, 1, dc), lambda b, c, t: (b, 0, c))],
        out_shape=[jax.ShapeDtypeStruct((nb * seq, d), BF16),
                   jax.ShapeDtypeStruct((nb, CONV_W - 1, d), F32),
                   jax.ShapeDtypeStruct((nb, 1, d), F32)],
        scratch_shapes=[pltpu.VMEM((8, dc), F32), pltpu.VMEM((8, dc), F32),
                        pltpu.VMEM((tc, dc), F32), pltpu.VMEM((tc, dc), F32)],
        compiler_params=_cparams(("arbitrary", "arbitrary", "arbitrary")),
        name="rglru_seq",
    )(pa, pa, w_conv, b_conv, wrg, brg, lam)


def _rglru_step_kernel(x_ref, g_ref, prev_ref, h0_ref, wc_ref, bc_ref, wrg_ref, brg_ref, lam_ref,
                       y_ref, conv_ref, h_ref):
    nt = x_ref.shape[0]
    full = [prev_ref[k] for k in range(CONV_W - 1)] + [x_ref[k].astype(F32) for k in range(nt)]
    sp = _softplus_neg(lam_ref[...])
    h = h0_ref[...]
    for t in range(nt):
        xc = bc_ref[...] + wc_ref[0:1, :] * full[t]
        for k in range(1, CONV_W):
            xc = xc + wc_ref[k:k + 1, :] * full[t + k]
        a, b = _rg_coeffs(xc, wrg_ref, brg_ref, sp)
        h = a * h + b
        y_ref[t] = (h * _gelu_tanh(g_ref[t].astype(F32))).astype(BF16)
    for k in range(CONV_W - 1):
        conv_ref[k] = full[nt + k]
    h_ref[...] = h


def rglru_step(x4, g4, prev, h0, w_conv, b_conv, wrg, brg, lam, *, dc=512):
    nt, bs, d = x4.shape
    bpc = dc // RNN_BLOCK_W
    return pl.pallas_call(
        _rglru_step_kernel,
        grid=(d // dc,),
        in_specs=[pl.BlockSpec((nt, bs, dc), lambda c: (0, 0, c)),
                  pl.BlockSpec((nt, bs, dc), lambda c: (0, 0, c)),
                  pl.BlockSpec((CONV_W - 1, bs, dc), lambda c: (0, 0, c)),
                  pl.BlockSpec((bs, dc), lambda c: (0, c)),
                  pl.BlockSpec((CONV_W, dc), lambda c: (0, c)),
                  pl.BlockSpec((1, dc), lambda c: (0, c)),
                  pl.BlockSpec((bpc, RNN_BLOCK_W, 2 * RNN_BLOCK_W), lambda c: (c, 0, 0)),
                  pl.BlockSpec((bpc, 1, 2 * RNN_BLOCK_W), lambda c: (c, 0, 0)),
                  pl.BlockSpec((1, dc), lambda c: (0, c))],
        out_specs=[pl.BlockSpec((nt, bs, dc), lambda c: (0, 0, c)),
                   pl.BlockSpec((CONV_W - 1, bs, dc), lambda c: (0, 0, c)),
                   pl.BlockSpec((bs, dc), lambda c: (0, c))],
        out_shape=[jax.ShapeDtypeStruct((nt, bs, d), BF16),
                   jax.ShapeDtypeStruct((CONV_W - 1, bs, d), F32),
                   jax.ShapeDtypeStruct((bs, d), F32)],
        compiler_params=_cparams(("arbitrary",)),
        name="rglru_step",
    )(x4, g4, prev, h0, w_conv, b_conv, wrg, brg, lam)


def _rope_heads(qr, qrot, cos128, sin128, scale, out_ref, lane0):
    for hp in range(qr.shape[1] // 128):
        sl = slice(hp * 128, (hp + 1) * 128)
        v = ((qr[:, sl] * cos128 + qrot[:, sl] * sin128) * scale).astype(BF16)
        out_ref[2 * hp, :, lane0:lane0 + QK_ROPE] = v[:, :QK_ROPE]
        out_ref[2 * hp + 1, :, lane0:lane0 + QK_ROPE] = v[:, QK_ROPE:]


def _qkv_prompt_kernel(cq_ref, ckv_ref, wqn_ref, wqr_ref, wqrot_ref, wuk_ref, wuvt_ref, cos_ref, sin_ref,
                       qn_ref, qr_ref, kn_ref, vt_ref, *, scale):
    cq = cq_ref[...]
    qn_ref[...] = (_dot(cq, wqn_ref[...]) * scale).astype(BF16)
    _rope_heads(_dot(cq, wqr_ref[...]), _dot(cq, wqrot_ref[...]), cos_ref[...], sin_ref[...], scale, qr_ref, 0)
    cb = ckv_ref[...].astype(BF16)
    kn_ref[...] = _dot(cb, wuk_ref[...]).astype(BF16)
    vt_ref[...] = _dot_nt(wuvt_ref[...], cb).astype(BF16)


def qkv_prompt(cq, ckv, wqn, wqr, wqrot, wuk, wuvt, cos128, sin128, *, seq, scale, tm=512):
    t, ql = cq.shape
    kvl = ckv.shape[1]
    hn, hr, hv = wqn.shape[1], wqr.shape[1], wuvt.shape[0]
    nh = hr // QK_ROPE
    tps = seq // tm
    full = lambda a: pl.BlockSpec(a.shape, lambda i: (0,) * a.ndim)
    tab = pl.BlockSpec((tm, 128), lambda i: (i % tps, 0))
    return pl.pallas_call(
        functools.partial(_qkv_prompt_kernel, scale=scale),
        grid=(t // tm,),
        in_specs=[pl.BlockSpec((tm, ql), lambda i: (i, 0)), pl.BlockSpec((tm, kvl), lambda i: (i, 0)),
                  full(wqn), full(wqr), full(wqrot), full(wuk), full(wuvt), tab, tab],
        out_specs=[pl.BlockSpec((tm, hn), lambda i: (i, 0)),
                   pl.BlockSpec((nh, tm, QK_ROPE), lambda i: (0, i, 0)),
                   pl.BlockSpec((tm, hn), lambda i: (i, 0)),
                   pl.BlockSpec((hv, tm), lambda i: (0, i))],
        out_shape=[jax.ShapeDtypeStruct((t, hn), BF16), jax.ShapeDtypeStruct((nh, t, QK_ROPE), BF16),
                   jax.ShapeDtypeStruct((t, hn), BF16), jax.ShapeDtypeStruct((hv, t), BF16)],
        compiler_params=_cparams(("arbitrary",)),
        name="qkv_prompt",
    )(cq, ckv, wqn, wqr, wqrot, wuk, wuvt, cos128, sin128)


def _q_sample_kernel(cq_ref, wqn_ref, wqr_ref, wqrot_ref, wukt_ref, cos_ref, sin_ref, q_ref, *, scale):
    cq = cq_ref[...]
    qn = _dot(cq, wqn_ref[...])
    kvl = wukt_ref.shape[2]
    for h in range(wukt_ref.shape[0]):
        ql = _dot(qn[:, h * QK_NOPE:(h + 1) * QK_NOPE].astype(BF16), wukt_ref[h]) * scale
        q_ref[h, :, :kvl] = ql.astype(BF16)
    _rope_heads(_dot(cq, wqr_ref[...]), _dot(cq, wqrot_ref[...]), cos_ref[...], sin_ref[...], scale, q_ref, kvl)


def q_sample(cq, wqn, wqr, wqrot, wukt, cos128, sin128, *, scale):
    t = cq.shape[0]
    nh, _, kvl = wukt.shape
    full = lambda a: pl.BlockSpec(a.shape, lambda i: (0,) * a.ndim)
    return pl.pallas_call(
        functools.partial(_q_sample_kernel, scale=scale),
        grid=(1,),
        in_specs=[full(cq), full(wqn), full(wqr), full(wqrot), full(wukt), full(cos128), full(sin128)],
        out_specs=pl.BlockSpec((nh, t, kvl + QK_ROPE), lambda i: (0, 0, 0)),
        out_shape=jax.ShapeDtypeStruct((nh, t, kvl + QK_ROPE), BF16),
        compiler_params=_cparams(("arbitrary",)),
        name="q_sample",
    )(cq, wqn, wqr, wqrot, wukt, cos128, sin128)


def _softmax_step(s, v, m_ref, l_ref, acc_ref):
    m_prev = m_ref[...]
    m_new = jnp.maximum(m_prev, jnp.max(s, axis=-1, keepdims=True))
    alpha = jnp.exp(m_prev - m_new)
    p = jnp.exp(s - m_new)
    l_ref[...] = alpha * l_ref[...] + jnp.sum(p, axis=-1, keepdims=True)
    acc_ref[...] = alpha * acc_ref[...] + _dot(p.astype(BF16), v)
    m_ref[...] = m_new


def _attn_prompt_kernel(qi_ref, kj_ref, qn_ref, qr_ref, kn_ref, kr_ref, vt_ref, o_ref, m_ref, l_ref, acc_ref, *, hb):
    p = pl.program_id(2)
    i, j = qi_ref[p], kj_ref[p]
    hs = [slice(h * QK_NOPE, (h + 1) * QK_NOPE) for h in range(hb)]

    @pl.when(j == 0)
    def _():
        m_ref[...] = jnp.full_like(m_ref, NEG_BIG)
        l_ref[...] = jnp.zeros_like(l_ref)
        acc_ref[...] = jnp.zeros_like(acc_ref)

    kr = kr_ref[...].astype(BF16)
    st = [_dot_nt(jnp.concatenate([kn_ref[:, hs[h]], kr], axis=1),
                  jnp.concatenate([qn_ref[:, hs[h]], qr_ref[h]], axis=1)) for h in range(hb)]

    def step(st):
        m_prev = [m_ref[h] for h in range(hb)]
        m_new = [jnp.maximum(m_prev[h], jnp.max(st[h], axis=0, keepdims=True)) for h in range(hb)]
        alpha = [jnp.exp2(m_prev[h] - m_new[h]) for h in range(hb)]
        pt = [jnp.exp2(st[h] - m_new[h]) for h in range(hb)]
        pv = [_dot(vt_ref[hs[h], :], pt[h].astype(BF16)) for h in range(hb)]
        for h in range(hb):
            l_ref[h] = alpha[h] * l_ref[h] + jnp.sum(pt[h], axis=0, keepdims=True)
            acc_ref[h] = alpha[h] * acc_ref[h] + pv[h]
            m_ref[h] = m_new[h]

    @pl.when(j < i)
    def _():
        step(st)

    @pl.when(j == i)
    def _():
        key = lax.broadcasted_iota(jnp.int32, st[0].shape, 0)
        qry = lax.broadcasted_iota(jnp.int32, st[0].shape, 1)
        step([jnp.where(key <= qry, s, NEG_BIG) for s in st])
        for h in range(hb):
            o_ref[:, hs[h]] = (acc_ref[h] / l_ref[h]).T.astype(BF16)


def attn_prompt(qn, qr, kn, kr, vt, *, nb, seq, tq=512, hb=2):
    t, hn = qn.shape
    nh = hn // QK_NOPE
    nq = seq // tq
    wb = hb * QK_NOPE
    pairs = [(i, j) for i in range(nq) for j in range(i + 1)]
    qi = jnp.asarray([p[0] for p in pairs], jnp.int32)
    kj = jnp.asarray([p[1] for p in pairs], jnp.int32)
    grid_spec = pltpu.PrefetchScalarGridSpec(
        num_scalar_prefetch=2,
        grid=(nb, nh // hb, len(pairs)),
        in_specs=[pl.BlockSpec((tq, wb), lambda b, h, p, qi, kj: (b * nq + qi[p], h)),
                  pl.BlockSpec((hb, tq, QK_ROPE), lambda b, h, p, qi, kj: (h, b * nq + qi[p], 0)),
                  pl.BlockSpec((tq, wb), lambda b, h, p, qi, kj: (b * nq + kj[p], h)),
                  pl.BlockSpec((tq, QK_ROPE), lambda b, h, p, qi, kj: (b * nq + kj[p], 0)),
                  pl.BlockSpec((wb, tq), lambda b, h, p, qi, kj: (h, b * nq + kj[p]))],
        out_specs=pl.BlockSpec((tq, wb), lambda b, h, p, qi, kj: (b * nq + qi[p], h)),
        scratch_shapes=[pltpu.VMEM((hb, 1, tq), F32), pltpu.VMEM((hb, 1, tq), F32),
                        pltpu.VMEM((hb, QK_NOPE, tq), F32)],
    )
    return pl.pallas_call(
        functools.partial(_attn_prompt_kernel, hb=hb),
        grid_spec=grid_spec,
        out_shape=jax.ShapeDtypeStruct((t, hn), BF16),
        compiler_params=_cparams(("arbitrary", "arbitrary", "arbitrary")),
        name="attn_prompt",
    )(qi, kj, qn, qr, kn, kr, vt)


NEW_ROWS_PAD = 16


def _attn_sample_kernel(pt_ref, q_ref, ck_hbm, kr_hbm, cn_ref, kn_ref, o_ref,
                        ckraw, krraw, ckbuf, krbuf, m_ref, l_ref, acc_ref, sem, *, layer, gs, pg, page, kvl, heads):
    b, s_idx = pl.program_id(0), pl.program_id(1)
    nb, ns = pl.num_programs(0), pl.num_programs(1)
    step = b * ns + s_idx
    slot = step % 2
    npages = ns * pg

    def page_copies(bb, ss, sl, fetch):
        out = []
        for g in range(gs):
            for p in range(pg):
                pid = pt_ref[(bb * gs + g) * npages + ss * pg + p] if fetch else 0
                out.append(pltpu.make_async_copy(ck_hbm.at[layer, pid], ckraw.at[sl, g * pg + p], sem.at[sl]))
                out.append(pltpu.make_async_copy(kr_hbm.at[layer, pid], krraw.at[sl, g * pg + p], sem.at[sl]))
        return out

    @pl.when(step == 0)
    def _():
        for c in page_copies(b, s_idx, slot, True):
            c.start()

    @pl.when(step + 1 < nb * ns)
    def _():
        wrap = s_idx + 1 == ns
        for c in page_copies(jnp.where(wrap, b + 1, b), jnp.where(wrap, 0, s_idx + 1), 1 - slot, True):
            c.start()

    @pl.when(s_idx == 0)
    def _():
        m_ref[...] = jnp.full_like(m_ref, NEG_BIG)
        l_ref[...] = jnp.zeros_like(l_ref)
        acc_ref[...] = jnp.zeros_like(acc_ref)

    for c in page_copies(b, s_idx, slot, False):
        c.wait()

    for g in range(gs):
        for p in range(pg):
            ckbuf[g, p * page:(p + 1) * page, :] = ckraw[slot, g * pg + p].astype(BF16)
            krbuf[g, :, p * page:(p + 1) * page] = krraw[slot, g * pg + p].astype(BF16)
    s = [_dot_nt(q_ref[g][:, :kvl], ckbuf[g]) + _dot(q_ref[g][:, kvl:], krbuf[g]) for g in range(gs)]
    m_prev = [m_ref[g] for g in range(gs)]
    m_new = [jnp.maximum(m_prev[g], jnp.max(s[g], axis=-1, keepdims=True)) for g in range(gs)]
    alpha = [jnp.exp(m_prev[g] - m_new[g]) for g in range(gs)]
    pr = [jnp.exp(s[g] - m_new[g]) for g in range(gs)]
    pv = [_dot(pr[g].astype(BF16), ckbuf[g]) for g in range(gs)]
    for g in range(gs):
        l_ref[g] = alpha[g] * l_ref[g] + jnp.sum(pr[g], axis=-1, keepdims=True)
        acc_ref[g] = alpha[g] * acc_ref[g] + pv[g]
        m_ref[g] = m_new[g]

    @pl.when(s_idx == pl.num_programs(1) - 1)
    def _():
        nt = cn_ref.shape[1]
        zpad = lambda a: jnp.concatenate([a, jnp.zeros((NEW_ROWS_PAD - nt, a.shape[1]), F32)], axis=0).astype(BF16)
        for g in range(gs):
            q = q_ref[g]
            cn = zpad(cn_ref[g])
            s = _dot_nt(q[:, :kvl], cn) + _dot_nt(q[:, kvl:], zpad(kn_ref[g]))
            row = lax.broadcasted_iota(jnp.int32, s.shape, 0)
            col = lax.broadcasted_iota(jnp.int32, s.shape, 1)
            s = jnp.where(col * heads <= row, s, NEG_BIG)
            _softmax_step(s, cn, m_ref.at[g], l_ref.at[g], acc_ref.at[g])
            o_ref[g] = (acc_ref[g] / l_ref[g]).astype(BF16)


def attn_sample(q, cache_ckv, cache_krope_t, ckv_new, kr_new, page_table, *, layer, heads, gs=2, pg=8):
    bs, nq, qd = q.shape
    npages = page_table.shape[1]
    page, kvl = cache_ckv.shape[2], cache_ckv.shape[3]
    nt = ckv_new.shape[1]
    assert nt <= NEW_ROWS_PAD and npages % pg == 0 and bs % gs == 0

    grid_spec = pltpu.PrefetchScalarGridSpec(
        num_scalar_prefetch=1,
        grid=(bs // gs, npages // pg),
        in_specs=[pl.BlockSpec((gs, nq, qd), lambda b, s, pt: (b, 0, 0)),
                  pl.BlockSpec(memory_space=pl.ANY), pl.BlockSpec(memory_space=pl.ANY),
                  pl.BlockSpec((gs, nt, kvl), lambda b, s, pt: (b, 0, 0)),
                  pl.BlockSpec((gs, nt, QK_ROPE), lambda b, s, pt: (b, 0, 0))],
        out_specs=pl.BlockSpec((gs, nq, kvl), lambda b, s, pt: (b, 0, 0)),
        scratch_shapes=[pltpu.VMEM((2, gs * pg, page, kvl), F32), pltpu.VMEM((2, gs * pg, QK_ROPE, page), F32),
                        pltpu.VMEM((gs, pg * page, kvl), BF16), pltpu.VMEM((gs, QK_ROPE, pg * page), BF16),
                        pltpu.VMEM((gs, nq, 1), F32), pltpu.VMEM((gs, nq, 1), F32), pltpu.VMEM((gs, nq, kvl), F32),
                        pltpu.SemaphoreType.DMA((2,))],
    )
    return pl.pallas_call(
        functools.partial(_attn_sample_kernel, layer=layer, gs=gs, pg=pg, page=page, kvl=kvl, heads=heads),
        grid_spec=grid_spec,
        out_shape=jax.ShapeDtypeStruct((bs, nq, kvl), BF16),
        compiler_params=_cparams(("arbitrary", "arbitrary")),
        name="attn_sample",
    )(page_table.reshape(-1), q, cache_ckv, cache_krope_t, ckv_new, kr_new)


def _uv_kernel(o_ref, w_ref, y_ref):
    y_ref[...] = _dot(o_ref[...], w_ref[...]).astype(BF16)


def uv_sample(o, wuv_h):
    t = o.shape[0]
    nh, kvl, vd = wuv_h.shape
    return pl.pallas_call(
        _uv_kernel,
        grid=(nh,),
        in_specs=[pl.BlockSpec((t, kvl), lambda h: (0, h)), pl.BlockSpec((None, kvl, vd), lambda h: (h, 0, 0))],
        out_specs=pl.BlockSpec((t, vd), lambda h: (0, h)),
        out_shape=jax.ShapeDtypeStruct((t, nh * vd), BF16),
        compiler_params=_cparams(("arbitrary",)),
        name="uv_sample",
    )(o, wuv_h)


def _merge_route_kernel(x_ref, ya_ref, yb_ref, ga_ref, gb_ref, g1_ref, sc_ref, sh_ref, wo_ref, gn_ref,
                        wrh_ref, wrl_ref, br_ref, cin_ref, x1_ref, hn_ref, rt_ref, cnt_ref):
    @pl.when(pl.program_id(0) == 0)
    def _():
        cnt_ref[...] = cin_ref[...]

    merged = (jax.nn.sigmoid(ga_ref[...].astype(F32)) * ya_ref[...].astype(F32)
              + jax.nn.sigmoid(gb_ref[...].astype(F32)) * yb_ref[...].astype(F32))
    x1 = x_ref[...] + g1_ref[...] * _dot(merged.astype(BF16), wo_ref[...])
    x1_ref[...] = x1
    hn = _rms(x1, gn_ref[...]) * (1.0 + sc_ref[...]) + sh_ref[...]
    hn_ref[...] = hn

    hh = hn.astype(BF16)
    hl = (hn - hh.astype(F32)).astype(BF16)
    logits = _dot(hh, wrh_ref[...]) + (_dot(hh, wrl_ref[...]) + _dot(hl, wrh_ref[...])) + br_ref[...]
    tm = logits.shape[0]
    lane = lax.broadcasted_iota(jnp.int32, logits.shape, 1)

    def first_argmax(v):
        mx = jnp.max(v, axis=-1, keepdims=True)
        return mx, jnp.min(jnp.where(v == mx, lane, ROUTE_LANES), axis=-1, keepdims=True)

    gl = jnp.where(lane < N_GROUPS, logits, NEG_BIG)
    gmax, gidx = first_argmax(gl)
    p_top = 1.0 / jnp.sum(jnp.exp(gl - gmax), axis=-1, keepdims=True)
    lo = N_GROUPS + gidx * EXPERTS_PER_GROUP
    el = jnp.where((lane >= lo) & (lane < lo + EXPERTS_PER_GROUP), logits, NEG_BIG)
    m1, l1 = first_argmax(el)
    m2, l2 = first_argmax(jnp.where(lane == l1, NEG_BIG, el))
    e21 = jnp.exp(m2 - m1)
    w1 = p_top / (1.0 + e21)
    w2 = p_top * e21 / (1.0 + e21)
    e1, e2 = l1 - N_GROUPS, l2 - N_GROUPS

    oh1, oh2 = lane == e1, lane == e2
    ohs = jnp.where(oh1 | oh2, 1.0, 0.0)
    tri = jnp.where(lax.broadcasted_iota(jnp.int32, (tm, tm), 0) > lax.broadcasted_iota(jnp.int32, (tm, tm), 1),
                    1.0, 0.0).astype(BF16)
    before = _dot(tri, ohs.astype(BF16)) + cnt_ref[...]
    r1 = jnp.sum(jnp.where(oh1, before, 0.0), axis=-1, keepdims=True)
    r2 = jnp.sum(jnp.where(oh2, before, 0.0), axis=-1, keepdims=True)
    cnt_ref[...] = cnt_ref[...] + jnp.sum(ohs, axis=0, keepdims=True)

    rt = jnp.where(lane == 0, e1.astype(F32), 0.0)
    rt = jnp.where(lane == 1, e2.astype(F32), rt)
    rt = jnp.where(lane == 2, w1, rt)
    rt = jnp.where(lane == 3, w2, rt)
    rt = jnp.where(lane == 4, r1, rt)
    rt_ref[...] = jnp.where(lane == 5, r2, rt)


def merge_route(x, ya, yb, pa, g1, sc2, sh2, wo, gn2, wrh, wrl, br, cnt_in, *, per_seq, tm=256):
    t, d = x.shape
    if per_seq:
        tps = per_seq // tm
        mod_spec = pl.BlockSpec((None, 1, d), lambda i: (i // tps, 0, 0))
    else:
        mod_spec = pl.BlockSpec((tm, d), lambda i: (i, 0))
    row = pl.BlockSpec((tm, d), lambda i: (i, 0))
    full = lambda a: pl.BlockSpec(a.shape, lambda i: (0,) * a.ndim)
    return pl.pallas_call(
        _merge_route_kernel,
        grid=(t // tm,),
        in_specs=[row, row, row, pl.BlockSpec((tm, d), lambda i: (i, 2)), pl.BlockSpec((tm, d), lambda i: (i, 3)),
                  mod_spec, mod_spec, mod_spec, full(wo), full(gn2), full(wrh), full(wrl), full(br), full(cnt_in)],
        out_specs=[row, row, pl.BlockSpec((tm, ROUTE_LANES), lambda i: (i, 0)),
                   pl.BlockSpec((1, ROUTE_LANES), lambda i: (0, 0))],
        out_shape=[jax.ShapeDtypeStruct((t, d), F32), jax.ShapeDtypeStruct((t, d), F32),
                   jax.ShapeDtypeStruct((t, ROUTE_LANES), F32), jax.ShapeDtypeStruct((1, ROUTE_LANES), F32)],
        compiler_params=_cparams(("arbitrary",)),
        name="merge_route",
    )(x, ya, yb, pa, pa, g1, sc2, sh2, wo, gn2, wrh, wrl, br, cnt_in)


def _row_copy(src, dst, sem):
    return pltpu.make_async_copy(src, dst, sem)


def _dispatch_kernel(pos_ref, h_ref, xs_in_ref, xs_ref, sem):
    del xs_in_ref
    i = pl.program_id(0)
    tm = h_ref.shape[0]

    def issue(r, c):
        for k in range(2):
            p = pos_ref[(i * tm + r) * 2 + k]
            _row_copy(h_ref.at[pl.ds(r, 1), :], xs_ref.at[pl.ds(p, 1), :], sem).start()
        return c

    lax.fori_loop(0, tm, issue, 0)

    def wait(r, c):
        for k in range(2):
            _row_copy(h_ref.at[pl.ds(0, 1), :], xs_ref.at[pl.ds(0, 1), :], sem).wait()
        return c

    lax.fori_loop(0, tm, wait, 0)


def dispatch(pos, hn2, xs, *, tm=256):
    t, d = hn2.shape
    grid_spec = pltpu.PrefetchScalarGridSpec(
        num_scalar_prefetch=1,
        grid=(t // tm,),
        in_specs=[pl.BlockSpec((tm, d), lambda i, pos: (i, 0)), pl.BlockSpec(memory_space=pl.ANY)],
        out_specs=pl.BlockSpec(memory_space=pl.ANY),
        scratch_shapes=[pltpu.SemaphoreType.DMA(())],
    )
    return pl.pallas_call(
        _dispatch_kernel,
        grid_spec=grid_spec,
        out_shape=jax.ShapeDtypeStruct(xs.shape, xs.dtype),
        input_output_aliases={2: 0},
        compiler_params=_cparams(("arbitrary",)),
        name="dispatch",
    )(pos, hn2, xs)


def _gmm_kernel(te_ref, first_ref, nt_ref, x_ref, wg_ref, wu_ref, wd_ref, y_ref, wgb, wub, wdb):
    i = pl.program_id(0)

    @pl.when(i < nt_ref[0])
    def _():
        @pl.when(first_ref[i] == 1)
        def _():
            wgb[...] = wg_ref[...].astype(BF16)
            wub[...] = wu_ref[...].astype(BF16)
            wdb[...] = wd_ref[...].astype(BF16)

        x = x_ref[...].astype(BF16)
        g = _dot(x, wgb[...])
        u = _dot(x, wub[...])
        y_ref[...] = _dot((g * jax.nn.sigmoid(g) * u).astype(BF16), wdb[...])

    @pl.when(i >= nt_ref[0])
    def _():
        y_ref[...] = jnp.zeros_like(y_ref)


def expert_ffn(tile_e, tile_first, n_tiles, xs, w_gate, w_up, w_down, *, layer, tm):
    s, d = xs.shape
    de = w_gate.shape[3]
    nt_max = s // tm
    last = lambda i, nt: jnp.minimum(i, nt[0] - 1)
    grid_spec = pltpu.PrefetchScalarGridSpec(
        num_scalar_prefetch=3,
        grid=(nt_max,),
        in_specs=[pl.BlockSpec((tm, d), lambda i, te, tf, nt: (last(i, nt), 0)),
                  pl.BlockSpec((None, None, d, de), lambda i, te, tf, nt: (layer, te[i], 0, 0)),
                  pl.BlockSpec((None, None, d, de), lambda i, te, tf, nt: (layer, te[i], 0, 0)),
                  pl.BlockSpec((None, None, de, d), lambda i, te, tf, nt: (layer, te[i], 0, 0))],
        out_specs=pl.BlockSpec((tm, d), lambda i, te, tf, nt: (i, 0)),
        scratch_shapes=[pltpu.VMEM((d, de), BF16), pltpu.VMEM((d, de), BF16), pltpu.VMEM((de, d), BF16)],
    )
    return pl.pallas_call(
        _gmm_kernel,
        grid_spec=grid_spec,
        out_shape=jax.ShapeDtypeStruct((s, d), F32),
        compiler_params=_cparams(("arbitrary",)),
        name="expert_ffn",
    )(tile_e, tile_first, n_tiles, xs, w_gate, w_up, w_down)


def _combine_kernel(pos_ref, x1_ref, g2_ref, rt_ref, gf_ref, ys_ref, o_ref, gbuf, sem):
    i = pl.program_id(0)
    tm = x1_ref.shape[0]

    def issue(r, c):
        for k in range(2):
            p = pos_ref[(i * tm + r) * 2 + k]
            _row_copy(ys_ref.at[pl.ds(p, 1), :], gbuf.at[k, pl.ds(r, 1), :], sem).start()
        return c

    lax.fori_loop(0, tm, issue, 0)

    def wait(r, c):
        for k in range(2):
            _row_copy(ys_ref.at[pl.ds(0, 1), :], gbuf.at[0, pl.ds(0, 1), :], sem).wait()
        return c

    lax.fori_loop(0, tm, wait, 0)
    rt = rt_ref[...]
    moe = rt[:, 2:3] * gbuf[0] + rt[:, 3:4] * gbuf[1]
    o_ref[...] = _rms(x1_ref[...] + g2_ref[...] * moe, gf_ref[...])


def combine_final(pos, x1, g2, rt, gf, ys, *, per_seq, tm=256):
    t, d = x1.shape
    if per_seq:
        tps = per_seq // tm
        mod_spec = pl.BlockSpec((None, 1, d), lambda i, pos: (i // tps, 0, 0))
    else:
        mod_spec = pl.BlockSpec((tm, d), lambda i, pos: (i, 0))
    grid_spec = pltpu.PrefetchScalarGridSpec(
        num_scalar_prefetch=1,
        grid=(t // tm,),
        in_specs=[pl.BlockSpec((tm, d), lambda i, pos: (i, 0)), mod_spec,
                  pl.BlockSpec((tm, ROUTE_LANES), lambda i, pos: (i, 0)),
                  pl.BlockSpec((1, d), lambda i, pos: (0, 0)), pl.BlockSpec(memory_space=pl.ANY)],
        out_specs=pl.BlockSpec((tm, d), lambda i, pos: (i, 0)),
        scratch_shapes=[pltpu.VMEM((2, tm, d), F32), pltpu.SemaphoreType.DMA(())],
    )
    return pl.pallas_call(
        _combine_kernel,
        grid_spec=grid_spec,
        out_shape=jax.ShapeDtypeStruct((t, d), F32),
        compiler_params=_cparams(("arbitrary",)),
        name="combine_final",
    )(pos, x1, g2, rt, gf, ys)


def _rope_tables(pos):
    half = QK_ROPE // 2
    freqs = ROPE_THETA ** (-jnp.arange(half, dtype=F32) / half)
    ang = pos[:, None] * freqs[None, :]
    cos, sin = jnp.cos(ang), jnp.sin(ang)
    return jnp.concatenate([cos, cos], axis=1), jnp.concatenate([sin, sin], axis=1)


def _rotate_half_cols(w):
    lead = w.shape[0]
    w3 = w.reshape(lead, -1, QK_ROPE)
    half = QK_ROPE // 2
    return jnp.concatenate([-w3[..., half:], w3[..., :half]], axis=-1).reshape(lead, -1)


def _pick_tile(n, pref):
    t = min(pref, n)
    while n % t:
        t //= 2
    return t


def kernel(x_prompt, x_sample, cache_ckv, cache_krope, state_conv, state_h, page_table, c_prompt, c_sample, w_ada, b_ada, g_norm1, g_norm2, w_in, w_conv, b_conv, w_rg_a, b_rg_a, w_rg_x, b_rg_x, rg_lambda, g_q, w_uq, g_kv, w_uk, w_uv, w_o, w_group, b_group, w_router, b_router, w_gate, w_up, w_down, g_final):
    nb, seq, d = x_prompt.shape
    bs, nt, _ = x_sample.shape
    depth = w_ada.shape[0]
    d_rnn = w_conv.shape[2]
    q_lora, nh, qk_dim = w_uq.shape[1:]
    kv_lora = w_uk.shape[1]
    v_dim = w_uv.shape[3]
    past = page_table.shape[1] * cache_ckv.shape[2]
    scale = float(qk_dim) ** -0.5
    tp, ts = nb * seq, bs * nt
    assert d_rnn == d and qk_dim == QK_NOPE + QK_ROPE and v_dim == QK_NOPE and nh * v_dim == d

    xp = x_prompt.reshape(tp, d)
    xs_tok = x_sample.reshape(ts, d)
    cos_p, sin_p = _rope_tables(jnp.arange(seq, dtype=F32))
    cos_s, sin_s = _rope_tables(jnp.tile(past + jnp.arange(nt, dtype=F32), bs))
    dbl = lambda a: jnp.concatenate([a, a], axis=1)
    c_all = jnp.concatenate([c_prompt, c_sample], axis=0)
    n_c = c_all.shape[0]
    c_all = jnp.pad(c_all, ((0, (-n_c) % 8), (0, 0)))

    outs_p, outs_s = [], []
    for l in range(depth):
        o1, o2, o3, o4, o5, o6 = np.cumsum([d_rnn, d_rnn, q_lora, kv_lora, QK_ROPE, d]).tolist()
        wi = w_in[l]
        wa = jnp.concatenate([wi[:, :o2], wi[:, o5:]], axis=1).astype(BF16)
        w_kr = wi[:, o4:o5]
        wb = jnp.concatenate([wi[:, o2:o4], w_kr, _rotate_half_cols(w_kr)], axis=1).astype(BF16)
        wrg = jnp.concatenate([w_rg_a[l], w_rg_x[l]], axis=2).astype(BF16)
        brg = jnp.concatenate([b_rg_a[l], b_rg_x[l]], axis=1)[:, None, :]
        wq = w_uq[l]
        wqn = wq[:, :, :QK_NOPE].reshape(q_lora, nh * QK_NOPE).astype(BF16)
        wqr_f = wq[:, :, QK_NOPE:].reshape(q_lora, nh * QK_ROPE)
        wqr, wqrot = wqr_f.astype(BF16), _rotate_half_cols(wqr_f).astype(BF16)
        wuk_flat = w_uk[l].reshape(kv_lora, nh * QK_NOPE).astype(BF16)
        wuv_flat = w_uv[l].reshape(kv_lora, nh * v_dim).astype(BF16)
        wukt = jnp.transpose(w_uk[l], (1, 2, 0)).astype(BF16)
        wuv_h = jnp.transpose(w_uv[l], (1, 0, 2)).astype(BF16)
        wo = w_o[l].astype(BF16)
        wr = jnp.concatenate([w_group[l], w_router[l]], axis=1)
        wr = jnp.pad(wr, ((0, 0), (0, ROUTE_LANES - wr.shape[1])))
        wrh = wr.astype(BF16)
        wrl = (wr - wrh.astype(F32)).astype(BF16)
        br = jnp.pad(jnp.concatenate([b_group[l], b_router[l]]), (0, ROUTE_LANES - N_GROUPS - N_EXPERTS))[None, :]
        row = lambda a: a[None, :]

        mod = ada_mod(c_all, w_ada[l], b_ada[l])
        mods_p = [m[:nb, None, :] for m in jnp.split(mod, 6, axis=1)]
        mods_s = [jnp.repeat(m[nb:nb + bs], nt, axis=0) for m in jnp.split(mod, 6, axis=1)]

        tm_p = _pick_tile(seq, 512)
        pa_p, cq_p, ckv_p, kr_p = in_proj(xp, mods_p[1], mods_p[0], row(g_norm1[l]), wa, wb, row(g_q[l]),
                                          row(g_kv[l]), cos_p, sin_p, per_seq=seq, tm=_pick_tile(seq, 1024))
        tm_s = _pick_tile(ts, 512)
        pa_s, cq_s, ckv_s, kr_s = in_proj(xs_tok, mods_s[1], mods_s[0], row(g_norm1[l]), wa, wb, row(g_q[l]),
                                          row(g_kv[l]), cos_s, sin_s, per_seq=0, tm=tm_s)

        ya_p, conv_p, h_p = rglru_seq(pa_p, nb, seq, w_conv[l], row(b_conv[l]), wrg, brg, row(rg_lambda[l]),
                                      tc=_pick_tile(seq, 512))
        tmajor = lambda a: jnp.transpose(a.reshape(bs, nt, -1), (1, 0, 2))
        ya_s4, conv_s, h_s = rglru_step(tmajor(pa_s[:, :d_rnn]), tmajor(pa_s[:, d_rnn:2 * d_rnn]),
                                        jnp.transpose(state_conv[l], (1, 0, 2)), state_h[l],
                                        w_conv[l], row(b_conv[l]), wrg, brg, row(rg_lambda[l]))
        ya_s = jnp.transpose(ya_s4, (1, 0, 2)).reshape(ts, d)
        conv_s = jnp.transpose(conv_s, (1, 0, 2))

        qn_p, qr_p, kn_p, vt_p = qkv_prompt(cq_p, ckv_p, wqn, wqr, wqrot, wuk_flat, wuv_flat.T, dbl(cos_p), dbl(sin_p),
                                            seq=seq, scale=scale * LOG2_E, tm=tm_p)
        yb_p = attn_prompt(qn_p, qr_p, kn_p, kr_p, vt_p, nb=nb, seq=seq, tq=tm_p)
        q_s = q_sample(cq_s, wqn, wqr, wqrot, wukt, dbl(cos_s), dbl(sin_s), scale=scale)
        q_s = jnp.transpose(q_s, (1, 0, 2)).reshape(bs, nt * nh, kv_lora + QK_ROPE)
        o_s = attn_sample(q_s, cache_ckv, jnp.swapaxes(cache_krope, 2, 3), ckv_s.reshape(bs, nt, kv_lora),
                          kr_s.reshape(bs, nt, QK_ROPE), page_table, layer=l, heads=nh,
                          gs=_pick_tile(bs, 2), pg=_pick_tile(page_table.shape[1], 8))
        yb_s = uv_sample(o_s.reshape(ts, nh * kv_lora), wuv_h)

        tm_r = _pick_tile(seq, 256)
        cnt0 = jnp.zeros((1, ROUTE_LANES), F32)
        x1_p, hn_p, rt_p, cnt1 = merge_route(xp, ya_p, yb_p, pa_p, mods_p[2], mods_p[4], mods_p[3], wo,
                                             row(g_norm2[l]), wrh, wrl, br, cnt0, per_seq=seq, tm=tm_r)
        tm_rs = _pick_tile(ts, 256)
        x1_s, hn_s, rt_s, cnt2 = merge_route(xs_tok, ya_s, yb_s, pa_s, mods_s[2], mods_s[4], mods_s[3], wo,
                                             row(g_norm2[l]), wrh, wrl, br, cnt1, per_seq=0, tm=tm_rs)

        tmg = 256
        counts = cnt2[0, :N_EXPERTS].astype(jnp.int32)
        tiles_e = (counts + tmg - 1) // tmg
        tile_end = jnp.cumsum(tiles_e)
        offs = (tile_end - tiles_e) * tmg
        nt_max = (2 * (tp + ts)) // tmg + N_EXPERTS
        tile_ids = jnp.arange(nt_max, dtype=jnp.int32)
        tile_e = jnp.minimum(jnp.sum((tile_ids[:, None] >= tile_end[None, :]).astype(jnp.int32), axis=1), N_EXPERTS - 1)
        n_tiles = tile_end[-1:].astype(jnp.int32)
        tile_first = jnp.concatenate([jnp.ones((1,), jnp.int32), (tile_e[1:] != tile_e[:-1]).astype(jnp.int32)])

        def slot_pos(rt):
            e = rt[:, 0:2].astype(jnp.int32)
            return (offs[e] + rt[:, 4:6].astype(jnp.int32)).reshape(-1)

        pos_p, pos_s = slot_pos(rt_p), slot_pos(rt_s)
        xs_buf = jnp.zeros((nt_max * tmg, d), F32)
        xs_buf = dispatch(pos_p, hn_p, xs_buf, tm=tm_r)
        xs_buf = dispatch(pos_s, hn_s, xs_buf, tm=tm_rs)
        ys_buf = expert_ffn(tile_e, tile_first, n_tiles, xs_buf, w_gate, w_up, w_down, layer=l, tm=tmg)

        last = l == depth - 1
        gf = row(g_final) if last else None
        assert last, "final norm is fused into the last layer's combine; DEPTH > 1 needs an un-normed variant"
        xp = combine_final(pos_p, x1_p, mods_p[5], rt_p, gf, ys_buf, per_seq=seq, tm=tm_r)
        xs_tok = combine_final(pos_s, x1_s, mods_s[5], rt_s, gf, ys_buf, per_seq=0, tm=tm_rs)

        outs_p.append((ckv_p.reshape(nb, seq, kv_lora), kr_p.reshape(nb, seq, QK_ROPE), conv_p, h_p.reshape(nb, d)))
        outs_s.append((ckv_s.reshape(bs, nt, kv_lora), kr_s.reshape(bs, nt, QK_ROPE), conv_s, h_s))

    stack = lambda outs, k: jnp.stack([o[k] for o in outs])
    return (xp.reshape(nb, seq, d), xs_tok.reshape(bs, nt, d),
            stack(outs_p, 0), stack(outs_p, 1), stack(outs_p, 2), stack(outs_p, 3),
            stack(outs_s, 0), stack(outs_s, 1), stack(outs_s, 2), stack(outs_s, 3))
```

```python
import functools

import jax
import jax.numpy as jnp
import numpy as np
from jax import lax
from jax.experimental import pallas as pl
from jax.experimental.pallas import tpu as pltpu

F32 = jnp.float32
BF16 = jnp.bfloat16

EPS = 1e-6
RG_C = 8.0
RNN_BLOCK_W = 128
CONV_W = 4
QK_NOPE = 128
QK_ROPE = 64
ROPE_THETA = 10000.0
N_GROUPS = 4
EXPERTS_PER_GROUP = 8
N_EXPERTS = N_GROUPS * EXPERTS_PER_GROUP
ROUTE_LANES = 128
NEG_BIG = -1e30
LOG2_E = 1.4426950408889634
VMEM_LIMIT = 56 * 1024 * 1024


def _cparams(sem, vmem=VMEM_LIMIT):
    return pltpu.CompilerParams(dimension_semantics=sem, vmem_limit_bytes=vmem)


def _rms(x, g):
    return x * lax.rsqrt(jnp.mean(x * x, axis=-1, keepdims=True) + EPS) * g


def _gelu_tanh(x):
    return 0.5 * x * (1.0 + jnp.tanh(0.7978845608028654 * (x + 0.044715 * x * x * x)))


def _dot(a, b):
    return jnp.dot(a, b, preferred_element_type=F32)


def _dot_nt(a, b):
    return lax.dot_general(a, b, (((1,), (1,)), ((), ())), preferred_element_type=F32)


def _ada_kernel(c_ref, w_ref, b_ref, o_ref):
    c = c_ref[...]
    s = (c * jax.nn.sigmoid(c)).astype(BF16)
    o_ref[...] = _dot(s, w_ref[...].astype(BF16)) + b_ref[...]


def ada_mod(c_all, w_ada, b_ada, tn=1024):
    m, d = c_all.shape
    n = w_ada.shape[1]
    return pl.pallas_call(
        _ada_kernel,
        grid=(n // tn,),
        in_specs=[pl.BlockSpec((m, d), lambda j: (0, 0)),
                  pl.BlockSpec((d, tn), lambda j: (0, j)),
                  pl.BlockSpec((1, tn), lambda j: (0, j))],
        out_specs=pl.BlockSpec((m, tn), lambda j: (0, j)),
        out_shape=jax.ShapeDtypeStruct((m, n), F32),
        compiler_params=_cparams(("arbitrary",)),
        name="ada_mod",
    )(c_all, w_ada, b_ada.reshape(1, n))


def _inproj_kernel(x_ref, sc_ref, sh_ref, g_ref, wa_ref, wb_ref, gq_ref, gkv_ref, cos_ref, sin_ref,
                   pa_ref, cq_ref, ckv_ref, kr_ref, hn_ref, *, q_lora, kv_lora):
    @pl.when(pl.program_id(1) == 0)
    def _():
        hn = _rms(x_ref[...], g_ref[...]) * (1.0 + sc_ref[...]) + sh_ref[...]
        hb = hn.astype(BF16)
        hn_ref[...] = hb
        pb = _dot(hb, wb_ref[...])
        o1 = q_lora
        o2 = o1 + kv_lora
        cq_ref[...] = _rms(pb[:, :o1], gq_ref[...]).astype(BF16)
        ckv_ref[...] = _rms(pb[:, o1:o2], gkv_ref[...])
        kr_ref[...] = pb[:, o2:o2 + QK_ROPE] * cos_ref[...] + pb[:, o2 + QK_ROPE:o2 + 2 * QK_ROPE] * sin_ref[...]

    pa_ref[...] = _dot(hn_ref[...], wa_ref[...]).astype(BF16)


def in_proj(x, sc, sh, g1, wa, wb, gq, gkv, cos64, sin64, *, per_seq, tm, tn=1024):
    t, d = x.shape
    na = wa.shape[1]
    q_lora, kv_lora = gq.shape[1], gkv.shape[1]
    if per_seq:
        tps = per_seq // tm
        mod_spec = pl.BlockSpec((None, 1, d), lambda i, j: (i // tps, 0, 0))
        tab_spec = pl.BlockSpec((tm, QK_ROPE), lambda i, j: (i % tps, 0))
    else:
        mod_spec = pl.BlockSpec((tm, d), lambda i, j: (i, 0))
        tab_spec = pl.BlockSpec((tm, QK_ROPE), lambda i, j: (i, 0))
    row = lambda w: pl.BlockSpec((tm, w), lambda i, j: (i, 0))
    full = lambda a: pl.BlockSpec(a.shape, lambda i, j: (0,) * a.ndim)
    return pl.pallas_call(
        functools.partial(_inproj_kernel, q_lora=q_lora, kv_lora=kv_lora),
        grid=(t // tm, na // tn),
        in_specs=[row(d), mod_spec, mod_spec, full(g1),
                  pl.BlockSpec((d, tn), lambda i, j: (0, j)), full(wb), full(gq), full(gkv), tab_spec, tab_spec],
        out_specs=[pl.BlockSpec((tm, tn), lambda i, j: (i, j)), row(q_lora), row(kv_lora), row(QK_ROPE)],
        out_shape=[jax.ShapeDtypeStruct((t, na), BF16), jax.ShapeDtypeStruct((t, q_lora), BF16),
                   jax.ShapeDtypeStruct((t, kv_lora), F32), jax.ShapeDtypeStruct((t, QK_ROPE), F32)],
        scratch_shapes=[pltpu.VMEM((tm, d), BF16)],
        compiler_params=_cparams(("arbitrary", "arbitrary")),
        name="in_proj",
    )(x, sc, sh, g1, wa, wb, gq, gkv, cos64, sin64)


def _rg_coeffs(xc, wrg_ref, brg_ref, sp):
    a_parts, b_parts = [], []
    for n in range(xc.shape[1] // RNN_BLOCK_W):
        sl = slice(n * RNN_BLOCK_W, (n + 1) * RNN_BLOCK_W)
        xb = xc[:, sl]
        z = _dot(xb.astype(BF16), wrg_ref[n]) + brg_ref[n]
        g = 0.5 + 0.5 * jnp.tanh(0.5 * z)
        r, i = g[:, :RNN_BLOCK_W], g[:, RNN_BLOCK_W:]
        a = jnp.exp(-RG_C * r * sp[:, sl])
        a_parts.append(a)
        b_parts.append(jnp.sqrt(1.0 - a * a) * i * xb)
    return jnp.concatenate(a_parts, axis=1), jnp.concatenate(b_parts, axis=1)


def _softplus_neg(lam):
    return jnp.maximum(-lam, 0.0) + jnp.log1p(jnp.exp(-jnp.abs(lam)))


def _rglru_seq_kernel(x_ref, g_ref, wc_ref, bc_ref, wrg_ref, brg_ref, lam_ref,
                      y_ref, conv_ref, h_ref, tail_ref, hc_ref, a_scr, b_scr):
    t = pl.program_id(2)
    tc, dc = x_ref.shape

    @pl.when(t == 0)
    def _():
        tail_ref[...] = jnp.zeros_like(tail_ref)
        hc_ref[...] = jnp.zeros_like(hc_ref)

    x = x_ref[...].astype(F32)
    xw = jnp.concatenate([tail_ref[...], x], axis=0)
    xc = bc_ref[...] + wc_ref[CONV_W - 1:CONV_W, :] * x
    for k in range(CONV_W - 1):
        d = CONV_W - 1 - k
        xc = xc + wc_ref[k:k + 1, :] * xw[8 - d:8 - d + tc, :]
    tail_ref[...] = x[tc - 8:, :]

    a, b = _rg_coeffs(xc, wrg_ref, brg_ref, _softplus_neg(lam_ref[...]))
    a_scr[...] = a
    b_scr[...] = b

    row = lax.broadcasted_iota(jnp.int32, (8, dc), 0)

    def body(k, hc):
        r0 = pl.multiple_of(k * 8, 8)
        av = a_scr[pl.ds(r0, 8), :]
        bv = b_scr[pl.ds(r0, 8), :]
        for s in (1, 2, 4):
            a_sh = jnp.where(row >= s, pltpu.roll(av, s, 0), 1.0)
            b_sh = jnp.where(row >= s, pltpu.roll(bv, s, 0), 0.0)
            bv = av * b_sh + bv
            av = av * a_sh
        h = av * hc + bv
        b_scr[pl.ds(r0, 8), :] = h
        return jnp.broadcast_to(h[7:8, :], (8, dc))

    hc = lax.fori_loop(0, tc // 8, body, hc_ref[...])
    hc_ref[...] = hc
    y_ref[...] = (b_scr[...] * _gelu_tanh(g_ref[...].astype(F32))).astype(BF16)

    @pl.when(t == pl.num_programs(2) - 1)
    def _():
        conv_ref[...] = x[tc - (CONV_W - 1):, :]
        h_ref[...] = hc[0:1, :]


def rglru_seq(pa, nb, seq, w_conv, b_conv, wrg, brg, lam, *, tc=512, dc=512):
    d = w_conv.shape[1]
    nc, nt = d // dc, seq // tc
    bpc = dc // RNN_BLOCK_W
    return pl.pallas_call(
        _rglru_seq_kernel,
        grid=(nb, nc, nt),
        in_specs=[pl.BlockSpec((tc, dc), lambda b, c, t: (b * nt + t, c)),
                  pl.BlockSpec((tc, dc), lambda b, c, t: (b * nt + t, nc + c)),
                  pl.BlockSpec((CONV_W, dc), lambda b, c, t: (0, c)),
                  pl.BlockSpec((1, dc), lambda b, c, t: (0, c)),
                  pl.BlockSpec((bpc, RNN_BLOCK_W, 2 * RNN_BLOCK_W), lambda b, c, t: (c, 0, 0)),
                  pl.BlockSpec((bpc, 1, 2 * RNN_BLOCK_W), lambda b, c, t: (c, 0, 0)),
                  pl.BlockSpec((1, dc), lambda b, c, t: (0, c))],
        out_specs=[pl.BlockSpec((tc, dc), lambda b, c, t: (b * nt + t, c)),
                   pl.BlockSpec((None, CONV_W - 1, dc), lambda b, c, t: (b, 0, c)),
                   pl.BlockSpec((None, 1, dc), lambda b, c, t: (b, 0, c))],
        out_shape=[jax.ShapeDtypeStruct((nb * seq, d), BF16),
                   jax.ShapeDtypeStruct((nb, CONV_W - 1, d), F32),
                   jax.ShapeDtypeStruct((nb, 1, d), F32)],
        scratch_shapes=[pltpu.VMEM((8, dc), F32), pltpu.VMEM((8, dc), F32),
                        pltpu.VMEM((tc, dc), F32), pltpu.VMEM((tc, dc), F32)],
        compiler_params=_cparams(("arbitrary", "arbitrary", "arbitrary")),
        name="rglru_seq",
    )(pa, pa, w_conv, b_conv, wrg, brg, lam)


def _rglru_step_kernel(x_ref, g_ref, prev_ref, h0_ref, wc_ref, bc_ref, wrg_ref, brg_ref, lam_ref,
                       y_ref, conv_ref, h_ref):
    nt = x_ref.shape[0]
    full = [prev_ref[k] for k in range(CONV_W - 1)] + [x_ref[k].astype(F32) for k in range(nt)]
    sp = _softplus_neg(lam_ref[...])
    h = h0_ref[...]
    for t in range(nt):
        xc = bc_ref[...] + wc_ref[0:1, :] * full[t]
        for k in range(1, CONV_W):
            xc = xc + wc_ref[k:k + 1, :] * full[t + k]
        a, b = _rg_coeffs(xc, wrg_ref, brg_ref, sp)
        h = a * h + b
        y_ref[t] = (h * _gelu_tanh(g_ref[t].astype(F32))).astype(BF16)
    for k in range(CONV_W - 1):
        conv_ref[k] = full[nt + k]
    h_ref[...] = h


def rglru_step(x4, g4, prev, h0, w_conv, b_conv, wrg, brg, lam, *, dc=512):
    nt, bs, d = x4.shape
    bpc = dc // RNN_BLOCK_W
    return pl.pallas_call(
        _rglru_step_kernel,
        grid=(d // dc,),
        in_specs=[pl.BlockSpec((nt, bs, dc), lambda c: (0, 0, c)),
                  pl.BlockSpec((nt, bs, dc), lambda c: (0, 0, c)),
                  pl.BlockSpec((CONV_W - 1, bs, dc), lambda c: (0, 0, c)),
                  pl.BlockSpec((bs, dc), lambda c: (0, c)),
                  pl.BlockSpec((CONV_W, dc), lambda c: (0, c)),
                  pl.BlockSpec((1, dc), lambda c: (0, c)),
                  pl.BlockSpec((bpc, RNN_BLOCK_W, 2 * RNN_BLOCK_W), lambda c: (c, 0, 0)),
                  pl.BlockSpec((bpc, 1, 2 * RNN_BLOCK_W), lambda c: (c, 0, 0)),
                  pl.BlockSpec((1, dc), lambda c: (0, c))],
        out_specs=[pl.BlockSpec((nt, bs, dc), lambda c: (0, 0, c)),
                   pl.BlockSpec((CONV_W - 1, bs, dc), lambda c: (0, 0, c)),
                   pl.BlockSpec((bs, dc), lambda c: (0, c))],
        out_shape=[jax.ShapeDtypeStruct((nt, bs, d), BF16),
                   jax.ShapeDtypeStruct((CONV_W - 1, bs, d), F32),
                   jax.ShapeDtypeStruct((bs, d), F32)],
        compiler_params=_cparams(("arbitrary",)),
        name="rglru_step",
    )(x4, g4, prev, h0, w_conv, b_conv, wrg, brg, lam)


def _rope_heads(qr, qrot, cos128, sin128, scale, out_ref, lane0):
    for hp in range(qr.shape[1] // 128):
        sl = slice(hp * 128, (hp + 1) * 128)
        v = ((qr[:, sl] * cos128 + qrot[:, sl] * sin128) * scale).astype(BF16)
        out_ref[2 * hp, :, lane0:lane0 + QK_ROPE] = v[:, :QK_ROPE]
        out_ref[2 * hp + 1, :, lane0:lane0 + QK_ROPE] = v[:, QK_ROPE:]


def _qkv_prompt_kernel(cq_ref, ckv_ref, wqn_ref, wqr_ref, wqrot_ref, wuk_ref, wuvt_ref, cos_ref, sin_ref,
                       qn_ref, qr_ref, kn_ref, vt_ref, *, scale):
    cq = cq_ref[...]
    qn_ref[...] = (_dot(cq, wqn_ref[...]) * scale).astype(BF16)
    _rope_heads(_dot(cq, wqr_ref[...]), _dot(cq, wqrot_ref[...]), cos_ref[...], sin_ref[...], scale, qr_ref, 0)
    cb = ckv_ref[...].astype(BF16)
    kn_ref[...] = _dot(cb, wuk_ref[...]).astype(BF16)
    vt_ref[...] = _dot_nt(wuvt_ref[...], cb).astype(BF16)


def qkv_prompt(cq, ckv, wqn, wqr, wqrot, wuk, wuvt, cos128, sin128, *, seq, scale, tm=512):
    t, ql = cq.shape
    kvl = ckv.shape[1]
    hn, hr, hv = wqn.shape[1], wqr.shape[1], wuvt.shape[0]
    nh = hr // QK_ROPE
    tps = seq // tm
    full = lambda a: pl.BlockSpec(a.shape, lambda i: (0,) * a.ndim)
    tab = pl.BlockSpec((tm, 128), lambda i: (i % tps, 0))
    return pl.pallas_call(
        functools.partial(_qkv_prompt_kernel, scale=scale),
        grid=(t // tm,),
        in_specs=[pl.BlockSpec((tm, ql), lambda i: (i, 0)), pl.BlockSpec((tm, kvl), lambda i: (i, 0)),
                  full(wqn), full(wqr), full(wqrot), full(wuk), full(wuvt), tab, tab],
        out_specs=[pl.BlockSpec((tm, hn), lambda i: (i, 0)),
                   pl.BlockSpec((nh, tm, QK_ROPE), lambda i: (0, i, 0)),
                   pl.BlockSpec((tm, hn), lambda i: (i, 0)),
                   pl.BlockSpec((hv, tm), lambda i: (0, i))],
        out_shape=[jax.ShapeDtypeStruct((t, hn), BF16), jax.ShapeDtypeStruct((nh, t, QK_ROPE), BF16),
                   jax.ShapeDtypeStruct((t, hn), BF16), jax.ShapeDtypeStruct((hv, t), BF16)],
        compiler_params=_cparams(("arbitrary",)),
        name="qkv_prompt",
    )(cq, ckv, wqn, wqr, wqrot, wuk, wuvt, cos128, sin128)


def _q_sample_kernel(cq_ref, wqn_ref, wqr_ref, wqrot_ref, wukt_ref, cos_ref, sin_ref, q_ref, *, scale):
    cq = cq_ref[...]
    qn = _dot(cq, wqn_ref[...])
    kvl = wukt_ref.shape[2]
    for h in range(wukt_ref.shape[0]):
        ql = _dot(qn[:, h * QK_NOPE:(h + 1) * QK_NOPE].astype(BF16), wukt_ref[h]) * scale
        q_ref[h, :, :kvl] = ql.astype(BF16)
    _rope_heads(_dot(cq, wqr_ref[...]), _dot(cq, wqrot_ref[...]), cos_ref[...], sin_ref[...], scale, q_ref, kvl)


def q_sample(cq, wqn, wqr, wqrot, wukt, cos128, sin128, *, scale):
    t = cq.shape[0]
    nh, _, kvl = wukt.shape
    full = lambda a: pl.BlockSpec(a.shape, lambda i: (0,) * a.ndim)
    return pl.pallas_call(
        functools.partial(_q_sample_kernel, scale=scale),
        grid=(1,),
        in_specs=[full(cq), full(wqn), full(wqr), full(wqrot), full(wukt), full(cos128), full(sin128)],
        out_specs=pl.BlockSpec((nh, t, kvl + QK_ROPE), lambda i: (0, 0, 0)),
        out_shape=jax.ShapeDtypeStruct((nh, t, kvl + QK_ROPE), BF16),
        compiler_params=_cparams(("arbitrary",)),
        name="q_sample",
    )(cq, wqn, wqr, wqrot, wukt, cos128, sin128)


def _softmax_step(s, v, m_ref, l_ref, acc_ref):
    m_prev = m_ref[...]
    m_new = jnp.maximum(m_prev, jnp.max(s, axis=-1, keepdims=True))
    alpha = jnp.exp(m_prev - m_new)
    p = jnp.exp(s - m_new)
    l_ref[...] = alpha * l_ref[...] + jnp.sum(p, axis=-1, keepdims=True)
    acc_ref[...] = alpha * acc_ref[...] + _dot(p.astype(BF16), v)
    m_ref[...] = m_new


def _attn_prompt_kernel(qi_ref, kj_ref, qn_ref, qr_ref, kn_ref, kr_ref, vt_ref, o_ref, m_ref, l_ref, acc_ref, *, hb):
    p = pl.program_id(2)
    i, j = qi_ref[p], kj_ref[p]
    hs = [slice(h * QK_NOPE, (h + 1) * QK_NOPE) for h in range(hb)]

    @pl.when(j == 0)
    def _():
        m_ref[...] = jnp.full_like(m_ref, NEG_BIG)
        l_ref[...] = jnp.zeros_like(l_ref)
        acc_ref[...] = jnp.zeros_like(acc_ref)

    kr = kr_ref[...].astype(BF16)
    st = [_dot_nt(jnp.concatenate([kn_ref[:, hs[h]], kr], axis=1),
                  jnp.concatenate([qn_ref[:, hs[h]], qr_ref[h]], axis=1)) for h in range(hb)]

    def step(st):
        m_prev = [m_ref[h] for h in range(hb)]
        m_new = [jnp.maximum(m_prev[h], jnp.max(st[h], axis=0, keepdims=True)) for h in range(hb)]
        alpha = [jnp.exp2(m_prev[h] - m_new[h]) for h in range(hb)]
        pt = [jnp.exp2(st[h] - m_new[h]) for h in range(hb)]
        pv = [_dot(vt_ref[hs[h], :], pt[h].astype(BF16)) for h in range(hb)]
        for h in range(hb):
            l_ref[h] = alpha[h] * l_ref[h] + jnp.sum(pt[h], axis=0, keepdims=True)
            acc_ref[h] = alpha[h] * acc_ref[h] + pv[h]
            m_ref[h] = m_new[h]

    @pl.when(j < i)
    def _():
        step(st)

    @pl.when(j == i)
    def _():
        key = lax.broadcasted_iota(jnp.int32, st[0].shape, 0)
        qry = lax.broadcasted_iota(jnp.int32, st[0].shape, 1)
        step([jnp.where(key <= qry, s, NEG_BIG) for s in st])
        for h in range(hb):
            o_ref[:, hs[h]] = (acc_ref[h] / l_ref[h]).T.astype(BF16)


def attn_prompt(qn, qr, kn, kr, vt, *, nb, seq, tq=512, hb=2):
    t, hn = qn.shape
    nh = hn // QK_NOPE
    nq = seq // tq
    wb = hb * QK_NOPE
    pairs = [(i, j) for i in range(nq) for j in range(i + 1)]
    qi = jnp.asarray([p[0] for p in pairs], jnp.int32)
    kj = jnp.asarray([p[1] for p in pairs], jnp.int32)
    grid_spec = pltpu.PrefetchScalarGridSpec(
        num_scalar_prefetch=2,
        grid=(nb, nh // hb, len(pairs)),
        in_specs=[pl.BlockSpec((tq, wb), lambda b, h, p, qi, kj: (b * nq + qi[p], h)),
                  pl.BlockSpec((hb, tq, QK_ROPE), lambda b, h, p, qi, kj: (h, b * nq + qi[p], 0)),
                  pl.BlockSpec((tq, wb), lambda b, h, p, qi, kj: (b * nq + kj[p], h)),
                  pl.BlockSpec((tq, QK_ROPE), lambda b, h, p, qi, kj: (b * nq + kj[p], 0)),
                  pl.BlockSpec((wb, tq), lambda b, h, p, qi, kj: (h, b * nq + kj[p]))],
        out_specs=pl.BlockSpec((tq, wb), lambda b, h, p, qi, kj: (b * nq + qi[p], h)),
        scratch_shapes=[pltpu.VMEM((hb, 1, tq), F32), pltpu.VMEM((hb, 1, tq), F32),
                        pltpu.VMEM((hb, QK_NOPE, tq), F32)],
    )
    return pl.pallas_call(
        functools.partial(_attn_prompt_kernel, hb=hb),
        grid_spec=grid_spec,
        out_shape=jax.ShapeDtypeStruct((t, hn), BF16),
        compiler_params=_cparams(("arbitrary", "arbitrary", "arbitrary")),
        name="attn_prompt",
    )(qi, kj, qn, qr, kn, kr, vt)


NEW_ROWS_PAD = 16


def _attn_sample_kernel(pt_ref, q_ref, ck_hbm, kr_hbm, cn_ref, kn_ref, o_ref,
                        ckraw, krraw, ckbuf, krbuf, m_ref, l_ref, acc_ref, sem, *,
                        layer, gs, pg, page, kvl, heads, nsteps, ns):
    b, s_idx = pl.program_id(0), pl.program_id(1)
    ring = ckraw.shape[0]
    step = b * ns + s_idx
    slot = step % ring
    npages = ns * pg

    def start_pages(n, sl):
        bb, ss = n // ns, n % ns
        for g in range(gs):
            for p in range(pg):
                j = g * pg + p
                pid = pt_ref[(bb * gs + g) * npages + ss * pg + p]
                pltpu.make_async_copy(ck_hbm.at[layer, pid], ckraw.at[sl, j], sem.at[sl]).start(priority=j % 2)
                pltpu.make_async_copy(kr_hbm.at[layer, pid], krraw.at[sl, j], sem.at[sl]).start(priority=j % 2)

    @pl.when(step == 0)
    def _():
        for n in range(min(ring - 1, nsteps)):
            start_pages(n, n)

    @pl.when(step + ring - 1 < nsteps)
    def _():
        start_pages(step + ring - 1, (step + ring - 1) % ring)

    @pl.when(s_idx == 0)
    def _():
        m_ref[...] = jnp.full_like(m_ref, NEG_BIG)
        l_ref[...] = jnp.zeros_like(l_ref)
        acc_ref[...] = jnp.zeros_like(acc_ref)

    for j in range(gs * pg):
        pltpu.make_async_copy(ck_hbm.at[layer, 0], ckraw.at[slot, j], sem.at[slot]).wait()
        pltpu.make_async_copy(kr_hbm.at[layer, 0], krraw.at[slot, j], sem.at[slot]).wait()

    for g in range(gs):
        for p in range(pg):
            ckbuf[g, p * page:(p + 1) * page, :] = ckraw[slot, g * pg + p].astype(BF16)
            krbuf[g, :, p * page:(p + 1) * page] = krraw[slot, g * pg + p].astype(BF16)
    s = [_dot_nt(q_ref[g][:, :kvl], ckbuf[g]) + _dot(q_ref[g][:, kvl:], krbuf[g]) for g in range(gs)]
    m_prev = [m_ref[g] for g in range(gs)]
    m_new = [jnp.maximum(m_prev[g], jnp.max(s[g], axis=-1, keepdims=True)) for g in range(gs)]
    alpha = [jnp.exp(m_prev[g] - m_new[g]) for g in range(gs)]
    pr = [jnp.exp(s[g] - m_new[g]) for g in range(gs)]
    pv = [_dot(pr[g].astype(BF16), ckbuf[g]) for g in range(gs)]
    for g in range(gs):
        l_ref[g] = alpha[g] * l_ref[g] + jnp.sum(pr[g], axis=-1, keepdims=True)
        acc_ref[g] = alpha[g] * acc_ref[g] + pv[g]
        m_ref[g] = m_new[g]

    @pl.when(s_idx == pl.num_programs(1) - 1)
    def _():
        nt = cn_ref.shape[1]
        zpad = lambda a: jnp.concatenate([a, jnp.zeros((NEW_ROWS_PAD - nt, a.shape[1]), F32)], axis=0).astype(BF16)
        for g in range(gs):
            q = q_ref[g]
            cn = zpad(cn_ref[g])
            s = _dot_nt(q[:, :kvl], cn) + _dot_nt(q[:, kvl:], zpad(kn_ref[g]))
            row = lax.broadcasted_iota(jnp.int32, s.shape, 0)
            col = lax.broadcasted_iota(jnp.int32, s.shape, 1)
            s = jnp.where(col * heads <= row, s, NEG_BIG)
            _softmax_step(s, cn, m_ref.at[g], l_ref.at[g], acc_ref.at[g])
            o_ref[g] = (acc_ref[g] / l_ref[g]).astype(BF16)


def attn_sample(q, cache_ckv, cache_krope_t, ckv_new, kr_new, page_table, *, layer, heads, gs=2, pg=8, ring=3):
    bs, nq, qd = q.shape
    npages = page_table.shape[1]
    page, kvl = cache_ckv.shape[2], cache_ckv.shape[3]
    nt = ckv_new.shape[1]
    assert nt <= NEW_ROWS_PAD and npages % pg == 0 and bs % gs == 0

    grid_spec = pltpu.PrefetchScalarGridSpec(
        num_scalar_prefetch=1,
        grid=(bs // gs, npages // pg),
        in_specs=[pl.BlockSpec((gs, nq, qd), lambda b, s, pt: (b, 0, 0)),
                  pl.BlockSpec(memory_space=pl.ANY), pl.BlockSpec(memory_space=pl.ANY),
                  pl.BlockSpec((gs, nt, kvl), lambda b, s, pt: (b, 0, 0)),
                  pl.BlockSpec((gs, nt, QK_ROPE), lambda b, s, pt: (b, 0, 0))],
        out_specs=pl.BlockSpec((gs, nq, kvl), lambda b, s, pt: (b, 0, 0)),
        scratch_shapes=[pltpu.VMEM((ring, gs * pg, page, kvl), F32), pltpu.VMEM((ring, gs * pg, QK_ROPE, page), F32),
                        pltpu.VMEM((gs, pg * page, kvl), BF16), pltpu.VMEM((gs, QK_ROPE, pg * page), BF16),
                        pltpu.VMEM((gs, nq, 1), F32), pltpu.VMEM((gs, nq, 1), F32), pltpu.VMEM((gs, nq, kvl), F32),
                        pltpu.SemaphoreType.DMA((ring,))],
    )
    ns = npages // pg
    return pl.pallas_call(
        functools.partial(_attn_sample_kernel, layer=layer, gs=gs, pg=pg, page=page, kvl=kvl, heads=heads,
                          nsteps=(bs // gs) * ns, ns=ns),
        grid_spec=grid_spec,
        out_shape=jax.ShapeDtypeStruct((bs, nq, kvl), BF16),
        compiler_params=_cparams(("arbitrary", "arbitrary")),
        name="attn_sample",
    )(page_table.reshape(-1), q, cache_ckv, cache_krope_t, ckv_new, kr_new)


def _uv_kernel(o_ref, w_ref, y_ref):
    y_ref[...] = _dot(o_ref[...], w_ref[...]).astype(BF16)


def uv_sample(o, wuv_h):
    t = o.shape[0]
    nh, kvl, vd = wuv_h.shape
    return pl.pallas_call(
        _uv_kernel,
        grid=(nh,),
        in_specs=[pl.BlockSpec((t, kvl), lambda h: (0, h)), pl.BlockSpec((None, kvl, vd), lambda h: (h, 0, 0))],
        out_specs=pl.BlockSpec((t, vd), lambda h: (0, h)),
        out_shape=jax.ShapeDtypeStruct((t, nh * vd), BF16),
        compiler_params=_cparams(("arbitrary",)),
        name="uv_sample",
    )(o, wuv_h)


def _merge_route_kernel(*refs, n_real, has_prev):
    if has_prev:
        refs = refs[:14] + refs[15:]
    hn_ref = refs[15]
    i = pl.program_id(0)

    @pl.when(i < n_real)
    def _():
        _merge_route_body(*refs)

    @pl.when(i >= n_real)
    def _():
        hn_ref[...] = jnp.zeros_like(hn_ref)


def _merge_route_body(x_ref, ya_ref, yb_ref, ga_ref, gb_ref, g1_ref, sc_ref, sh_ref, wo_ref, gn_ref,
                      wrh_ref, wrl_ref, br_ref, cin_ref, x1_ref, hn_ref, rt_ref, cnt_ref):
    @pl.when(pl.program_id(0) == 0)
    def _():
        cnt_ref[...] = cin_ref[...]

    merged = (jax.nn.sigmoid(ga_ref[...].astype(F32)) * ya_ref[...].astype(F32)
              + jax.nn.sigmoid(gb_ref[...].astype(F32)) * yb_ref[...].astype(F32))
    x1 = x_ref[...] + g1_ref[...] * _dot(merged.astype(BF16), wo_ref[...])
    x1_ref[...] = x1
    hn = _rms(x1, gn_ref[...]) * (1.0 + sc_ref[...]) + sh_ref[...]
    hn_ref[...] = hn

    hh = hn.astype(BF16)
    hl = (hn - hh.astype(F32)).astype(BF16)
    logits = _dot(hh, wrh_ref[...]) + (_dot(hh, wrl_ref[...]) + _dot(hl, wrh_ref[...])) + br_ref[...]
    tm = logits.shape[0]
    lane = lax.broadcasted_iota(jnp.int32, logits.shape, 1)

    def first_argmax(v):
        mx = jnp.max(v, axis=-1, keepdims=True)
        return mx, jnp.min(jnp.where(v == mx, lane, ROUTE_LANES), axis=-1, keepdims=True)

    gl = jnp.where(lane < N_GROUPS, logits, NEG_BIG)
    gmax, gidx = first_argmax(gl)
    p_top = 1.0 / jnp.sum(jnp.exp(gl - gmax), axis=-1, keepdims=True)
    lo = N_GROUPS + gidx * EXPERTS_PER_GROUP
    el = jnp.where((lane >= lo) & (lane < lo + EXPERTS_PER_GROUP), logits, NEG_BIG)
    m1, l1 = first_argmax(el)
    m2, l2 = first_argmax(jnp.where(lane == l1, NEG_BIG, el))
    e21 = jnp.exp(m2 - m1)
    w1 = p_top / (1.0 + e21)
    w2 = p_top * e21 / (1.0 + e21)
    e1, e2 = l1 - N_GROUPS, l2 - N_GROUPS

    oh1, oh2 = lane == e1, lane == e2
    ohs = jnp.where(oh1 | oh2, 1.0, 0.0)
    tri = jnp.where(lax.broadcasted_iota(jnp.int32, (tm, tm), 0) > lax.broadcasted_iota(jnp.int32, (tm, tm), 1),
                    1.0, 0.0).astype(BF16)
    before = _dot(tri, ohs.astype(BF16)) + cnt_ref[...]
    r1 = jnp.sum(jnp.where(oh1, before, 0.0), axis=-1, keepdims=True)
    r2 = jnp.sum(jnp.where(oh2, before, 0.0), axis=-1, keepdims=True)
    cnt_ref[...] = cnt_ref[...] + jnp.sum(ohs, axis=0, keepdims=True)

    rt = jnp.where(lane == 0, e1.astype(F32), 0.0)
    rt = jnp.where(lane == 1, e2.astype(F32), rt)
    rt = jnp.where(lane == 2, w1, rt)
    rt = jnp.where(lane == 3, w2, rt)
    rt = jnp.where(lane == 4, r1, rt)
    rt_ref[...] = jnp.where(lane == 5, r2, rt)


def merge_route(x, ya, yb, pa, g1, sc2, sh2, wo, gn2, wrh, wrl, br, cnt_in, *, per_seq, tm, h_rows, h_row0, h_prev=None):
    t, d = x.shape
    n_real = t // tm
    assert h_row0 % tm == 0 and (h_rows - t) % tm == 0
    n_steps = n_real if h_prev is not None else h_rows // tm
    ri = lambda i: jnp.minimum(i, n_real - 1)
    if per_seq:
        tps = per_seq // tm
        mod_spec = pl.BlockSpec((None, 1, d), lambda i: (ri(i) // tps, 0, 0))
    else:
        mod_spec = pl.BlockSpec((tm, d), lambda i: (ri(i), 0))
    row = pl.BlockSpec((tm, d), lambda i: (ri(i), 0))
    full = lambda a: pl.BlockSpec(a.shape, lambda i: (0,) * a.ndim)
    args = [x, ya, yb, pa, pa, g1, sc2, sh2, wo, gn2, wrh, wrl, br, cnt_in]
    in_specs = [row, row, row, pl.BlockSpec((tm, d), lambda i: (ri(i), 2)), pl.BlockSpec((tm, d), lambda i: (ri(i), 3)),
                mod_spec, mod_spec, mod_spec, full(wo), full(gn2), full(wrh), full(wrl), full(br), full(cnt_in)]
    aliases = {}
    if h_prev is not None:
        args.append(h_prev)
        in_specs.append(pl.BlockSpec(memory_space=pl.ANY))
        aliases = {len(args) - 1: 1}
    return pl.pallas_call(
        functools.partial(_merge_route_kernel, n_real=n_real, has_prev=h_prev is not None),
        grid=(n_steps,),
        in_specs=in_specs,
        out_specs=[row, pl.BlockSpec((tm, d), lambda i: (i + h_row0 // tm, 0)),
                   pl.BlockSpec((tm, ROUTE_LANES), lambda i: (ri(i), 0)),
                   pl.BlockSpec((1, ROUTE_LANES), lambda i: (0, 0))],
        out_shape=[jax.ShapeDtypeStruct((t, d), F32), jax.ShapeDtypeStruct((h_rows, d), F32),
                   jax.ShapeDtypeStruct((t, ROUTE_LANES), F32), jax.ShapeDtypeStruct((1, ROUTE_LANES), F32)],
        input_output_aliases=aliases,
        compiler_params=_cparams(("arbitrary",)),
        name="merge_route",
    )(*args)


GATHER_UNROLL = 4


def _row_copy(src, dst, sem):
    return pltpu.make_async_copy(src, dst, sem)


def _gmm_gather_kernel(te_ref, first_ref, nt_ref, tok_ref, h_hbm, wg_ref, wu_ref, wd_ref, y_ref,
                       xbuf, wgb, wub, wdb, sem):
    i = pl.program_id(0)
    tm = xbuf.shape[1]
    slot = i % 2

    def start_gather(tile, sl):
        for r in range(tm):
            t = tok_ref[tile * tm + r]
            pltpu.make_async_copy(h_hbm.at[pl.ds(t, 1), :], xbuf.at[sl, pl.ds(r, 1), :], sem.at[sl]).start(priority=r % 2)

    @pl.when(i == 0)
    def _():
        start_gather(0, 0)

    valid = i < nt_ref[0]
    more = i + 1 < pl.num_programs(0)

    @pl.when(valid & (first_ref[i] == 1))
    def _():
        wgb[...] = wg_ref[...].astype(BF16)
        wub[...] = wu_ref[...].astype(BF16)
        wdb[...] = wd_ref[...].astype(BF16)

    def body(compute, prefetch):
        pltpu.make_async_copy(h_hbm.at[pl.ds(0, tm), :], xbuf.at[slot], sem.at[slot]).wait()
        if compute:
            x = xbuf[slot].astype(BF16)
            g = _dot(x, wgb[...])
        if prefetch:
            start_gather(i + 1, 1 - slot)
        if compute:
            u = _dot(x, wub[...])
            y_ref[...] = _dot((g * jax.nn.sigmoid(g) * u).astype(BF16), wdb[...])
        else:
            y_ref[...] = jnp.zeros_like(y_ref)

    for compute in (True, False):
        for prefetch in (True, False):
            pl.when((valid == compute) & (more == prefetch))(functools.partial(body, compute, prefetch))


def expert_ffn_gather(tile_e, tile_first, n_tiles, tok, h_all, w_gate, w_up, w_down, *, layer, tm, nt_max):
    d = h_all.shape[1]
    de = w_gate.shape[3]
    grid_spec = pltpu.PrefetchScalarGridSpec(
        num_scalar_prefetch=4,
        grid=(nt_max,),
        in_specs=[pl.BlockSpec(memory_space=pl.ANY),
                  pl.BlockSpec((None, None, d, de), lambda i, te, tf, nt, tok: (layer, te[i], 0, 0)),
                  pl.BlockSpec((None, None, d, de), lambda i, te, tf, nt, tok: (layer, te[i], 0, 0)),
                  pl.BlockSpec((None, None, de, d), lambda i, te, tf, nt, tok: (layer, te[i], 0, 0))],
        out_specs=pl.BlockSpec((tm, d), lambda i, te, tf, nt, tok: (i, 0)),
        scratch_shapes=[pltpu.VMEM((2, tm, d), F32),
                        pltpu.VMEM((d, de), BF16), pltpu.VMEM((d, de), BF16), pltpu.VMEM((de, d), BF16),
                        pltpu.SemaphoreType.DMA((2,))],
    )
    return pl.pallas_call(
        _gmm_gather_kernel,
        grid_spec=grid_spec,
        out_shape=jax.ShapeDtypeStruct((nt_max * tm, d), F32),
        compiler_params=_cparams(("arbitrary",)),
        name="expert_ffn",
    )(tile_e, tile_first, n_tiles, tok, h_all, w_gate, w_up, w_down)


def _combine_kernel(pos_ref, x1_ref, g2_ref, rt_ref, gf_ref, ys_ref, o_ref, gbuf, sem):
    i = pl.program_id(0)
    tm = x1_ref.shape[0]

    def issue(rb, c):
        for u in range(GATHER_UNROLL):
            r = rb * GATHER_UNROLL + u
            for k in range(2):
                p = pos_ref[(i * tm + r) * 2 + k]
                _row_copy(ys_ref.at[pl.ds(p, 1), :], gbuf.at[k, pl.ds(r, 1), :], sem).start(priority=k)
        return c

    lax.fori_loop(0, tm // GATHER_UNROLL, issue, 0)
    for k in range(2):
        _row_copy(ys_ref.at[pl.ds(0, tm), :], gbuf.at[k], sem).wait()
    rt = rt_ref[...]
    moe = rt[:, 2:3] * gbuf[0] + rt[:, 3:4] * gbuf[1]
    o_ref[...] = _rms(x1_ref[...] + g2_ref[...] * moe, gf_ref[...])


def combine_final(pos, x1, g2, rt, gf, ys, *, per_seq, tm=256):
    t, d = x1.shape
    if per_seq:
        tps = per_seq // tm
        mod_spec = pl.BlockSpec((None, 1, d), lambda i, pos: (i // tps, 0, 0))
    else:
        mod_spec = pl.BlockSpec((tm, d), lambda i, pos: (i, 0))
    grid_spec = pltpu.PrefetchScalarGridSpec(
        num_scalar_prefetch=1,
        grid=(t // tm,),
        in_specs=[pl.BlockSpec((tm, d), lambda i, pos: (i, 0)), mod_spec,
                  pl.BlockSpec((tm, ROUTE_LANES), lambda i, pos: (i, 0)),
                  pl.BlockSpec((1, d), lambda i, pos: (0, 0)), pl.BlockSpec(memory_space=pl.ANY)],
        out_specs=pl.BlockSpec((tm, d), lambda i, pos: (i, 0)),
        scratch_shapes=[pltpu.VMEM((2, tm, d), F32), pltpu.SemaphoreType.DMA(())],
    )
    return pl.pallas_call(
        _combine_kernel,
        grid_spec=grid_spec,
        out_shape=jax.ShapeDtypeStruct((t, d), F32),
        compiler_params=_cparams(("arbitrary",)),
        name="combine_final",
    )(pos, x1, g2, rt, gf, ys)


def _rope_tables(pos):
    half = QK_ROPE // 2
    freqs = ROPE_THETA ** (-jnp.arange(half, dtype=F32) / half)
    ang = pos[:, None] * freqs[None, :]
    cos, sin = jnp.cos(ang), jnp.sin(ang)
    return jnp.concatenate([cos, cos], axis=1), jnp.concatenate([sin, sin], axis=1)


def _rotate_half_cols(w):
    lead = w.shape[0]
    w3 = w.reshape(lead, -1, QK_ROPE)
    half = QK_ROPE // 2
    return jnp.concatenate([-w3[..., half:], w3[..., :half]], axis=-1).reshape(lead, -1)


def _pick_tile(n, pref):
    t = min(pref, n)
    while n % t:
        t //= 2
    return t


def kernel(x_prompt, x_sample, cache_ckv, cache_krope, state_conv, state_h, page_table, c_prompt, c_sample, w_ada, b_ada, g_norm1, g_norm2, w_in, w_conv, b_conv, w_rg_a, b_rg_a, w_rg_x, b_rg_x, rg_lambda, g_q, w_uq, g_kv, w_uk, w_uv, w_o, w_group, b_group, w_router, b_router, w_gate, w_up, w_down, g_final):
    nb, seq, d = x_prompt.shape
    bs, nt, _ = x_sample.shape
    depth = w_ada.shape[0]
    d_rnn = w_conv.shape[2]
    q_lora, nh, qk_dim = w_uq.shape[1:]
    kv_lora = w_uk.shape[1]
    v_dim = w_uv.shape[3]
    past = page_table.shape[1] * cache_ckv.shape[2]
    scale = float(qk_dim) ** -0.5
    tp, ts = nb * seq, bs * nt
    assert d_rnn == d and qk_dim == QK_NOPE + QK_ROPE and v_dim == QK_NOPE and nh * v_dim == d

    xp = x_prompt.reshape(tp, d)
    xs_tok = x_sample.reshape(ts, d)
    cos_p, sin_p = _rope_tables(jnp.arange(seq, dtype=F32))
    cos_s, sin_s = _rope_tables(jnp.tile(past + jnp.arange(nt, dtype=F32), bs))
    dbl = lambda a: jnp.concatenate([a, a], axis=1)
    c_all = jnp.concatenate([c_prompt, c_sample], axis=0)
    n_c = c_all.shape[0]
    c_all = jnp.pad(c_all, ((0, (-n_c) % 8), (0, 0)))

    outs_p, outs_s = [], []
    for l in range(depth):
        o1, o2, o3, o4, o5, o6 = np.cumsum([d_rnn, d_rnn, q_lora, kv_lora, QK_ROPE, d]).tolist()
        wi = w_in[l]
        wa = jnp.concatenate([wi[:, :o2], wi[:, o5:]], axis=1).astype(BF16)
        w_kr = wi[:, o4:o5]
        wb = jnp.concatenate([wi[:, o2:o4], w_kr, _rotate_half_cols(w_kr)], axis=1).astype(BF16)
        wrg = jnp.concatenate([w_rg_a[l], w_rg_x[l]], axis=2).astype(BF16)
        brg = jnp.concatenate([b_rg_a[l], b_rg_x[l]], axis=1)[:, None, :]
        wq = w_uq[l]
        wqn = wq[:, :, :QK_NOPE].reshape(q_lora, nh * QK_NOPE).astype(BF16)
        wqr_f = wq[:, :, QK_NOPE:].reshape(q_lora, nh * QK_ROPE)
        wqr, wqrot = wqr_f.astype(BF16), _rotate_half_cols(wqr_f).astype(BF16)
        wuk_flat = w_uk[l].reshape(kv_lora, nh * QK_NOPE).astype(BF16)
        wuv_flat = w_uv[l].reshape(kv_lora, nh * v_dim).astype(BF16)
        wukt = jnp.transpose(w_uk[l], (1, 2, 0)).astype(BF16)
        wuv_h = jnp.transpose(w_uv[l], (1, 0, 2)).astype(BF16)
        wo = w_o[l].astype(BF16)
        wr = jnp.concatenate([w_group[l], w_router[l]], axis=1)
        wr = jnp.pad(wr, ((0, 0), (0, ROUTE_LANES - wr.shape[1])))
        wrh = wr.astype(BF16)
        wrl = (wr - wrh.astype(F32)).astype(BF16)
        br = jnp.pad(jnp.concatenate([b_group[l], b_router[l]]), (0, ROUTE_LANES - N_GROUPS - N_EXPERTS))[None, :]
        row = lambda a: a[None, :]

        mod = ada_mod(c_all, w_ada[l], b_ada[l])
        mods_p = [m[:nb, None, :] for m in jnp.split(mod, 6, axis=1)]
        mods_s = [jnp.repeat(m[nb:nb + bs], nt, axis=0) for m in jnp.split(mod, 6, axis=1)]

        tm_p = _pick_tile(seq, 512)
        pa_p, cq_p, ckv_p, kr_p = in_proj(xp, mods_p[1], mods_p[0], row(g_norm1[l]), wa, wb, row(g_q[l]),
                                          row(g_kv[l]), cos_p, sin_p, per_seq=seq, tm=_pick_tile(seq, 1024))
        tm_s = _pick_tile(ts, 512)
        pa_s, cq_s, ckv_s, kr_s = in_proj(xs_tok, mods_s[1], mods_s[0], row(g_norm1[l]), wa, wb, row(g_q[l]),
                                          row(g_kv[l]), cos_s, sin_s, per_seq=0, tm=tm_s)

        ya_p, conv_p, h_p = rglru_seq(pa_p, nb, seq, w_conv[l], row(b_conv[l]), wrg, brg, row(rg_lambda[l]),
                                      tc=_pick_tile(seq, 512), dc=_pick_tile(d_rnn, 1024))
        tmajor = lambda a: jnp.transpose(a.reshape(bs, nt, -1), (1, 0, 2))
        ya_s4, conv_s, h_s = rglru_step(tmajor(pa_s[:, :d_rnn]), tmajor(pa_s[:, d_rnn:2 * d_rnn]),
                                        jnp.transpose(state_conv[l], (1, 0, 2)), state_h[l],
                                        w_conv[l], row(b_conv[l]), wrg, brg, row(rg_lambda[l]))
        ya_s = jnp.transpose(ya_s4, (1, 0, 2)).reshape(ts, d)
        conv_s = jnp.transpose(conv_s, (1, 0, 2))

        qn_p, qr_p, kn_p, vt_p = qkv_prompt(cq_p, ckv_p, wqn, wqr, wqrot, wuk_flat, wuv_flat.T, dbl(cos_p), dbl(sin_p),
                                            seq=seq, scale=scale * LOG2_E, tm=tm_p)
        yb_p = attn_prompt(qn_p, qr_p, kn_p, kr_p, vt_p, nb=nb, seq=seq, tq=tm_p)
        q_s = q_sample(cq_s, wqn, wqr, wqrot, wukt, dbl(cos_s), dbl(sin_s), scale=scale)
        q_s = jnp.transpose(q_s, (1, 0, 2)).reshape(bs, nt * nh, kv_lora + QK_ROPE)
        o_s = attn_sample(q_s, cache_ckv, jnp.swapaxes(cache_krope, 2, 3), ckv_s.reshape(bs, nt, kv_lora),
                          kr_s.reshape(bs, nt, QK_ROPE), page_table, layer=l, heads=nh,
                          gs=_pick_tile(bs, 2), pg=_pick_tile(page_table.shape[1], 8))
        yb_s = uv_sample(o_s.reshape(ts, nh * kv_lora), wuv_h)

        tm_r = _pick_tile(seq, 256)
        cnt0 = jnp.zeros((1, ROUTE_LANES), F32)
        tm_rs = _pick_tile(ts, tm_r)
        x1_p, h_all, rt_p, cnt1 = merge_route(xp, ya_p, yb_p, pa_p, mods_p[2], mods_p[4], mods_p[3], wo,
                                              row(g_norm2[l]), wrh, wrl, br, cnt0, per_seq=seq, tm=tm_r,
                                              h_rows=tp + ts, h_row0=0)
        x1_s, h_all, rt_s, cnt2 = merge_route(xs_tok, ya_s, yb_s, pa_s, mods_s[2], mods_s[4], mods_s[3], wo,
                                              row(g_norm2[l]), wrh, wrl, br, cnt1, per_seq=0, tm=tm_rs,
                                              h_rows=tp + ts, h_row0=tp, h_prev=h_all)

        tmg = 256
        counts = cnt2[0, :N_EXPERTS].astype(jnp.int32)
        tiles_e = (counts + tmg - 1) // tmg
        tile_end = jnp.cumsum(tiles_e)
        offs = (tile_end - tiles_e) * tmg
        nt_max = (2 * (tp + ts)) // tmg + N_EXPERTS
        tile_ids = jnp.arange(nt_max, dtype=jnp.int32)
        tile_e = jnp.minimum(jnp.sum((tile_ids[:, None] >= tile_end[None, :]).astype(jnp.int32), axis=1), N_EXPERTS - 1)
        n_tiles = tile_end[-1:].astype(jnp.int32)
        tile_first = jnp.concatenate([jnp.ones((1,), jnp.int32), (tile_e[1:] != tile_e[:-1]).astype(jnp.int32)])

        def slot_pos(rt):
            e = rt[:, 0:2].astype(jnp.int32)
            return (offs[e] + rt[:, 4:6].astype(jnp.int32)).reshape(-1)

        pos_p, pos_s = slot_pos(rt_p), slot_pos(rt_s)
        pos_all = jnp.concatenate([pos_p, pos_s])
        tok = jnp.zeros((nt_max * tmg,), jnp.int32).at[pos_all].set(
            jnp.arange(2 * (tp + ts), dtype=jnp.int32) // 2, unique_indices=True, indices_are_sorted=False)
        ys_buf = expert_ffn_gather(tile_e, tile_first, n_tiles, tok, h_all, w_gate, w_up, w_down,
                                   layer=l, tm=tmg, nt_max=nt_max)

        last = l == depth - 1
        gf = row(g_final) if last else None
        assert last, "final norm is fused into the last layer's combine; DEPTH > 1 needs an un-normed variant"
        xp = combine_final(pos_p, x1_p, mods_p[5], rt_p, gf, ys_buf, per_seq=seq, tm=tm_r)
        xs_tok = combine_final(pos_s, x1_s, mods_s[5], rt_s, gf, ys_buf, per_seq=0, tm=tm_rs)

        outs_p.append((ckv_p.reshape(nb, seq, kv_lora), kr_p.reshape(nb, seq, QK_ROPE), conv_p, h_p.reshape(nb, d)))
        outs_s.append((ckv_s.reshape(bs, nt, kv_lora), kr_s.reshape(bs, nt, QK_ROPE), conv_s, h_s))

    stack = lambda outs, k: jnp.stack([o[k] for o in outs])
    return (xp.reshape(nb, seq, d), xs_tok.reshape(bs, nt, d),
            stack(outs_p, 0), stack(outs_p, 1), stack(outs_p, 2), stack(outs_p, 3),
            stack(outs_s, 0), stack(outs_s, 1), stack(outs_s, 2), stack(outs_s, 3))
```

```python
import functools

import jax
import jax.numpy as jnp
import numpy as np
from jax import lax
from jax.experimental import pallas as pl
from jax.experimental.pallas import tpu as pltpu

F32 = jnp.float32
BF16 = jnp.bfloat16

EPS = 1e-6
RG_C = 8.0
RNN_BLOCK_W = 128
CONV_W = 4
QK_NOPE = 128
QK_ROPE = 64
ROPE_THETA = 10000.0
N_GROUPS = 4
EXPERTS_PER_GROUP = 8
N_EXPERTS = N_GROUPS * EXPERTS_PER_GROUP
ROUTE_LANES = 128
ROUTE_PARTS = 2
NEG_BIG = -1e30
LOG2_E = 1.4426950408889634
VMEM_LIMIT = 56 * 1024 * 1024


def _cparams(sem, vmem=VMEM_LIMIT):
    return pltpu.CompilerParams(dimension_semantics=sem, vmem_limit_bytes=vmem)


def _rms(x, g):
    return x * lax.rsqrt(jnp.mean(x * x, axis=-1, keepdims=True) + EPS) * g


def _gelu_tanh(x):
    return 0.5 * x * (1.0 + jnp.tanh(0.7978845608028654 * (x + 0.044715 * x * x * x)))


def _dot(a, b):
    return jnp.dot(a, b, preferred_element_type=F32)


def _dot_nt(a, b):
    return lax.dot_general(a, b, (((1,), (1,)), ((), ())), preferred_element_type=F32)


def _ada_kernel(c_ref, w_ref, b_ref, o_ref):
    c = c_ref[...]
    s = (c * jax.nn.sigmoid(c)).astype(BF16)
    o_ref[...] = _dot(s, w_ref[...].astype(BF16)) + b_ref[...]


def ada_mod(c_all, w_ada, b_ada, tn=1024):
    m, d = c_all.shape
    n = w_ada.shape[1]
    return pl.pallas_call(
        _ada_kernel,
        grid=(n // tn,),
        in_specs=[pl.BlockSpec((m, d), lambda j: (0, 0)),
                  pl.BlockSpec((d, tn), lambda j: (0, j)),
                  pl.BlockSpec((1, tn), lambda j: (0, j))],
        out_specs=pl.BlockSpec((m, tn), lambda j: (0, j)),
        out_shape=jax.ShapeDtypeStruct((m, n), F32),
        compiler_params=_cparams(("arbitrary",)),
        name="ada_mod",
    )(c_all, w_ada, b_ada.reshape(1, n))


def _inproj_kernel(x_ref, sc_ref, sh_ref, g_ref, wa_ref, wb_ref, gq_ref, gkv_ref, cos_ref, sin_ref,
                   pa_ref, cq_ref, ckv_ref, kr_ref, hn_ref, *, q_lora, kv_lora):
    @pl.when(pl.program_id(1) == 0)
    def _():
        hn = _rms(x_ref[...], g_ref[...]) * (1.0 + sc_ref[...]) + sh_ref[...]
        hb = hn.astype(BF16)
        hn_ref[...] = hb
        pb = _dot(hb, wb_ref[...])
        o1 = q_lora
        o2 = o1 + kv_lora
        cq_ref[...] = _rms(pb[:, :o1], gq_ref[...]).astype(BF16)
        ckv_ref[...] = _rms(pb[:, o1:o2], gkv_ref[...])
        kr_ref[...] = pb[:, o2:o2 + QK_ROPE] * cos_ref[...] + pb[:, o2 + QK_ROPE:o2 + 2 * QK_ROPE] * sin_ref[...]

    pa_ref[...] = _dot(hn_ref[...], wa_ref[...]).astype(BF16)


def in_proj(x, sc, sh, g1, wa, wb, gq, gkv, cos64, sin64, *, per_seq, tm, tn=1024):
    t, d = x.shape
    na = wa.shape[1]
    q_lora, kv_lora = gq.shape[1], gkv.shape[1]
    if per_seq:
        tps = per_seq // tm
        mod_spec = pl.BlockSpec((None, 1, d), lambda i, j: (i // tps, 0, 0))
        tab_spec = pl.BlockSpec((tm, QK_ROPE), lambda i, j: (i % tps, 0))
    else:
        mod_spec = pl.BlockSpec((tm, d), lambda i, j: (i, 0))
        tab_spec = pl.BlockSpec((tm, QK_ROPE), lambda i, j: (i, 0))
    row = lambda w: pl.BlockSpec((tm, w), lambda i, j: (i, 0))
    full = lambda a: pl.BlockSpec(a.shape, lambda i, j: (0,) * a.ndim)
    return pl.pallas_call(
        functools.partial(_inproj_kernel, q_lora=q_lora, kv_lora=kv_lora),
        grid=(t // tm, na // tn),
        in_specs=[row(d), mod_spec, mod_spec, full(g1),
                  pl.BlockSpec((d, tn), lambda i, j: (0, j)), full(wb), full(gq), full(gkv), tab_spec, tab_spec],
        out_specs=[pl.BlockSpec((tm, tn), lambda i, j: (i, j)), row(q_lora), row(kv_lora), row(QK_ROPE)],
        out_shape=[jax.ShapeDtypeStruct((t, na), BF16), jax.ShapeDtypeStruct((t, q_lora), BF16),
                   jax.ShapeDtypeStruct((t, kv_lora), F32), jax.ShapeDtypeStruct((t, QK_ROPE), F32)],
        scratch_shapes=[pltpu.VMEM((tm, d), BF16)],
        compiler_params=_cparams(("arbitrary", "arbitrary")),
        name="in_proj",
    )(x, sc, sh, g1, wa, wb, gq, gkv, cos64, sin64)


def _rg_coeffs(xc, wrg_ref, brg_ref, sp):
    a_parts, b_parts = [], []
    for n in range(xc.shape[1] // RNN_BLOCK_W):
        sl = slice(n * RNN_BLOCK_W, (n + 1) * RNN_BLOCK_W)
        xb = xc[:, sl]
        z = _dot(xb.astype(BF16), wrg_ref[n]) + brg_ref[n]
        g = 0.5 + 0.5 * jnp.tanh(0.5 * z)
        r, i = g[:, :RNN_BLOCK_W], g[:, RNN_BLOCK_W:]
        a = jnp.exp(-RG_C * r * sp[:, sl])
        a_parts.append(a)
        b_parts.append(jnp.sqrt(1.0 - a * a) * i * xb)
    return jnp.concatenate(a_parts, axis=1), jnp.concatenate(b_parts, axis=1)


def _softplus_neg(lam):
    return jnp.maximum(-lam, 0.0) + jnp.log1p(jnp.exp(-jnp.abs(lam)))


def _rglru_seq_kernel(x_ref, g_ref, wc_ref, bc_ref, wrg_ref, brg_ref, lam_ref,
                      y_ref, conv_ref, h_ref, tail_ref, hc_ref, a_scr, b_scr):
    t = pl.program_id(2)
    tc, dc = x_ref.shape

    @pl.when(t == 0)
    def _():
        tail_ref[...] = jnp.zeros_like(tail_ref)
        hc_ref[...] = jnp.zeros_like(hc_ref)

    x = x_ref[...].astype(F32)
    xw = jnp.concatenate([tail_ref[...], x], axis=0)
    xc = bc_ref[...] + wc_ref[CONV_W - 1:CONV_W, :] * x
    for k in range(CONV_W - 1):
        d = CONV_W - 1 - k
        xc = xc + wc_ref[k:k + 1, :] * xw[8 - d:8 - d + tc, :]
    tail_ref[...] = x[tc - 8:, :]

    a, b = _rg_coeffs(xc, wrg_ref, brg_ref, _softplus_neg(lam_ref[...]))
    a_scr[...] = a
    b_scr[...] = b

    row = lax.broadcasted_iota(jnp.int32, (8, dc), 0)

    def body(k, hc):
        r0 = pl.multiple_of(k * 8, 8)
        av = a_scr[pl.ds(r0, 8), :]
        bv = b_scr[pl.ds(r0, 8), :]
        for s in (1, 2, 4):
            a_sh = jnp.where(row >= s, pltpu.roll(av, s, 0), 1.0)
            b_sh = jnp.where(row >= s, pltpu.roll(bv, s, 0), 0.0)
            bv = av * b_sh + bv
            av = av * a_sh
        h = av * hc + bv
        b_scr[pl.ds(r0, 8), :] = h
        return jnp.broadcast_to(h[7:8, :], (8, dc))

    hc = lax.fori_loop(0, tc // 8, body, hc_ref[...])
    hc_ref[...] = hc
    y_ref[...] = (b_scr[...] * _gelu_tanh(g_ref[...].astype(F32))).astype(BF16)

    @pl.when(t == pl.num_programs(2) - 1)
    def _():
        conv_ref[...] = x[tc - (CONV_W - 1):, :]
        h_ref[...] = hc[0:1, :]


def rglru_seq(pa, nb, seq, w_conv, b_conv, wrg, brg, lam, *, tc=512, dc=512):
    d = w_conv.shape[1]
    nc, nt = d // dc, seq // tc
    bpc = dc // RNN_BLOCK_W
    return pl.pallas_call(
        _rglru_seq_kernel,
        grid=(nb, nc, nt),
        in_specs=[pl.BlockSpec((tc, dc), lambda b, c, t: (b * nt + t, c)),
                  pl.BlockSpec((tc, dc), lambda b, c, t: (b * nt + t, nc + c)),
                  pl.BlockSpec((CONV_W, dc), lambda b, c, t: (0, c)),
                  pl.BlockSpec((1, dc), lambda b, c, t: (0, c)),
                  pl.BlockSpec((bpc, RNN_BLOCK_W, 2 * RNN_BLOCK_W), lambda b, c, t: (c, 0, 0)),
                  pl.BlockSpec((bpc, 1, 2 * RNN_BLOCK_W), lambda b, c, t: (c, 0, 0)),
                  pl.BlockSpec((1, dc), lambda b, c, t: (0, c))],
        out_specs=[pl.BlockSpec((tc, dc), lambda b, c, t: (b * nt + t, c)),
                   pl.BlockSpec((None, CONV_W - 1, dc), lambda b, c, t: (b, 0, c)),
                   pl.BlockSpec((None, 1, dc), lambda b, c, t: (b, 0, c))],
        out_shape=[jax.ShapeDtypeStruct((nb * seq, d), BF16),
                   jax.ShapeDtypeStruct((nb, CONV_W - 1, d), F32),
                   jax.ShapeDtypeStruct((nb, 1, d), F32)],
        scratch_shapes=[pltpu.VMEM((8, dc), F32), pltpu.VMEM((8, dc), F32),
                        pltpu.VMEM((tc, dc), F32), pltpu.VMEM((tc, dc), F32)],
        compiler_params=_cparams(("arbitrary", "arbitrary", "arbitrary")),
        name="rglru_seq",
    )(pa, pa, w_conv, b_conv, wrg, brg, lam)


def _rglru_step_kernel(x_ref, g_ref, prev_ref, h0_ref, wc_ref, bc_ref, wrg_ref, brg_ref, lam_ref,
                       y_ref, conv_ref, h_ref):
    nt = x_ref.shape[0]
    full = [prev_ref[k] for k in range(CONV_W - 1)] + [x_ref[k].astype(F32) for k in range(nt)]
    sp = _softplus_neg(lam_ref[...])
    h = h0_ref[...]
    for t in range(nt):
        xc = bc_ref[...] + wc_ref[0:1, :] * full[t]
        for k in range(1, CONV_W):
            xc = xc + wc_ref[k:k + 1, :] * full[t + k]
        a, b = _rg_coeffs(xc, wrg_ref, brg_ref, sp)
        h = a * h + b
        y_ref[t] = (h * _gelu_tanh(g_ref[t].astype(F32))).astype(BF16)
    for k in range(CONV_W - 1):
        conv_ref[k] = full[nt + k]
    h_ref[...] = h


def rglru_step(x4, g4, prev, h0, w_conv, b_conv, wrg, brg, lam, *, dc=512):
    nt, bs, d = x4.shape
    bpc = dc // RNN_BLOCK_W
    return pl.pallas_call(
        _rglru_step_kernel,
        grid=(d // dc,),
        in_specs=[pl.BlockSpec((nt, bs, dc), lambda c: (0, 0, c)),
                  pl.BlockSpec((nt, bs, dc), lambda c: (0, 0, c)),
                  pl.BlockSpec((CONV_W - 1, bs, dc), lambda c: (0, 0, c)),
                  pl.BlockSpec((bs, dc), lambda c: (0, c)),
                  pl.BlockSpec((CONV_W, dc), lambda c: (0, c)),
                  pl.BlockSpec((1, dc), lambda c: (0, c)),
                  pl.BlockSpec((bpc, RNN_BLOCK_W, 2 * RNN_BLOCK_W), lambda c: (c, 0, 0)),
                  pl.BlockSpec((bpc, 1, 2 * RNN_BLOCK_W), lambda c: (c, 0, 0)),
                  pl.BlockSpec((1, dc), lambda c: (0, c))],
        out_specs=[pl.BlockSpec((nt, bs, dc), lambda c: (0, 0, c)),
                   pl.BlockSpec((CONV_W - 1, bs, dc), lambda c: (0, 0, c)),
                   pl.BlockSpec((bs, dc), lambda c: (0, c))],
        out_shape=[jax.ShapeDtypeStruct((nt, bs, d), BF16),
                   jax.ShapeDtypeStruct((CONV_W - 1, bs, d), F32),
                   jax.ShapeDtypeStruct((bs, d), F32)],
        compiler_params=_cparams(("arbitrary",)),
        name="rglru_step",
    )(x4, g4, prev, h0, w_conv, b_conv, wrg, brg, lam)


def _rope_heads(qr, qrot, cos128, sin128, scale, out_ref, lane0):
    for hp in range(qr.shape[1] // 128):
        sl = slice(hp * 128, (hp + 1) * 128)
        v = ((qr[:, sl] * cos128 + qrot[:, sl] * sin128) * scale).astype(BF16)
        out_ref[2 * hp, :, lane0:lane0 + QK_ROPE] = v[:, :QK_ROPE]
        out_ref[2 * hp + 1, :, lane0:lane0 + QK_ROPE] = v[:, QK_ROPE:]


def _qkv_prompt_kernel(cq_ref, ckv_ref, wqn_ref, wqr_ref, wqrot_ref, wuk_ref, wuvt_ref, cos_ref, sin_ref,
                       qn_ref, qr_ref, kn_ref, vt_ref, *, scale):
    cq = cq_ref[...]
    qn_ref[...] = (_dot(cq, wqn_ref[...]) * scale).astype(BF16)
    _rope_heads(_dot(cq, wqr_ref[...]), _dot(cq, wqrot_ref[...]), cos_ref[...], sin_ref[...], scale, qr_ref, 0)
    cb = ckv_ref[...].astype(BF16)
    kn_ref[...] = _dot(cb, wuk_ref[...]).astype(BF16)
    vt_ref[...] = _dot_nt(wuvt_ref[...], cb).astype(BF16)


def qkv_prompt(cq, ckv, wqn, wqr, wqrot, wuk, wuvt, cos128, sin128, *, seq, scale, tm=512):
    t, ql = cq.shape
    kvl = ckv.shape[1]
    hn, hr, hv = wqn.shape[1], wqr.shape[1], wuvt.shape[0]
    nh = hr // QK_ROPE
    tps = seq // tm
    full = lambda a: pl.BlockSpec(a.shape, lambda i: (0,) * a.ndim)
    tab = pl.BlockSpec((tm, 128), lambda i: (i % tps, 0))
    return pl.pallas_call(
        functools.partial(_qkv_prompt_kernel, scale=scale),
        grid=(t // tm,),
        in_specs=[pl.BlockSpec((tm, ql), lambda i: (i, 0)), pl.BlockSpec((tm, kvl), lambda i: (i, 0)),
                  full(wqn), full(wqr), full(wqrot), full(wuk), full(wuvt), tab, tab],
        out_specs=[pl.BlockSpec((tm, hn), lambda i: (i, 0)),
                   pl.BlockSpec((nh, tm, QK_ROPE), lambda i: (0, i, 0)),
                   pl.BlockSpec((tm, hn), lambda i: (i, 0)),
                   pl.BlockSpec((hv, tm), lambda i: (0, i))],
        out_shape=[jax.ShapeDtypeStruct((t, hn), BF16), jax.ShapeDtypeStruct((nh, t, QK_ROPE), BF16),
                   jax.ShapeDtypeStruct((t, hn), BF16), jax.ShapeDtypeStruct((hv, t), BF16)],
        compiler_params=_cparams(("arbitrary",)),
        name="qkv_prompt",
    )(cq, ckv, wqn, wqr, wqrot, wuk, wuvt, cos128, sin128)


def _q_sample_kernel(cq_ref, wqn_ref, wqr_ref, wqrot_ref, wukt_ref, cos_ref, sin_ref, q_ref, *, scale):
    cq = cq_ref[...]
    qn = _dot(cq, wqn_ref[...])
    kvl = wukt_ref.shape[2]
    for h in range(wukt_ref.shape[0]):
        ql = _dot(qn[:, h * QK_NOPE:(h + 1) * QK_NOPE].astype(BF16), wukt_ref[h]) * scale
        q_ref[h, :, :kvl] = ql.astype(BF16)
    _rope_heads(_dot(cq, wqr_ref[...]), _dot(cq, wqrot_ref[...]), cos_ref[...], sin_ref[...], scale, q_ref, kvl)


def q_sample(cq, wqn, wqr, wqrot, wukt, cos128, sin128, *, scale):
    t = cq.shape[0]
    nh, _, kvl = wukt.shape
    full = lambda a: pl.BlockSpec(a.shape, lambda i: (0,) * a.ndim)
    return pl.pallas_call(
        functools.partial(_q_sample_kernel, scale=scale),
        grid=(1,),
        in_specs=[full(cq), full(wqn), full(wqr), full(wqrot), full(wukt), full(cos128), full(sin128)],
        out_specs=pl.BlockSpec((nh, t, kvl + QK_ROPE), lambda i: (0, 0, 0)),
        out_shape=jax.ShapeDtypeStruct((nh, t, kvl + QK_ROPE), BF16),
        compiler_params=_cparams(("arbitrary",)),
        name="q_sample",
    )(cq, wqn, wqr, wqrot, wukt, cos128, sin128)


def _softmax_step(s, v, m_ref, l_ref, acc_ref):
    m_prev = m_ref[...]
    m_new = jnp.maximum(m_prev, jnp.max(s, axis=-1, keepdims=True))
    alpha = jnp.exp(m_prev - m_new)
    p = jnp.exp(s - m_new)
    l_ref[...] = alpha * l_ref[...] + jnp.sum(p, axis=-1, keepdims=True)
    acc_ref[...] = alpha * acc_ref[...] + _dot(p.astype(BF16), v)
    m_ref[...] = m_new


def _attn_prompt_kernel(qi_ref, kj_ref, qn_ref, qr_ref, kn_ref, kr_ref, vt_ref, o_ref, m_ref, l_ref, acc_ref, *, hb, cw):
    p = pl.program_id(2)
    i, j = qi_ref[p], kj_ref[p]
    hs = [slice(h * QK_NOPE, (h + 1) * QK_NOPE) for h in range(hb)]

    @pl.when(j == 0)
    def _():
        m_ref[...] = jnp.full_like(m_ref, NEG_BIG)
        l_ref[...] = jnp.zeros_like(l_ref)
        acc_ref[...] = jnp.zeros_like(acc_ref)

    tq = qn_ref.shape[0]

    def block(diagonal):
        kr = kr_ref[...].astype(BF16)
        kh = [jnp.concatenate([kn_ref[:, hs[h]], kr], axis=1) for h in range(hb)]
        chains = [(h, c) for h in range(hb) for c in range(tq // cw)]
        qs = lambda c: slice(c * cw, (c + 1) * cw)
        nk = lambda c: (c + 1) * cw if diagonal else tq
        m_prev = [m_ref[h, :, qs(c)] for h, c in chains]
        l_prev = [l_ref[h, :, qs(c)] for h, c in chains]
        a_prev = [acc_ref[h, :, qs(c)] for h, c in chains]
        st = [_dot_nt(kh[h][:nk(c)], jnp.concatenate([qn_ref[qs(c), hs[h]], qr_ref[h, qs(c), :]], axis=1))
              for h, c in chains]
        if diagonal:
            visible = lambda s, c: (lax.broadcasted_iota(jnp.int32, s.shape, 0)
                                    <= lax.broadcasted_iota(jnp.int32, s.shape, 1) + c * cw)
            st = [jnp.where(visible(s, c), s, NEG_BIG) for s, (h, c) in zip(st, chains)]
        m_new = [jnp.maximum(mp, jnp.max(s, axis=0, keepdims=True)) for mp, s in zip(m_prev, st)]
        alpha = [jnp.exp2(mp - mn) for mp, mn in zip(m_prev, m_new)]
        pt = [jnp.exp2(s - mn) for s, mn in zip(st, m_new)]
        pv = [_dot(vt_ref[hs[h], :nk(c)], p.astype(BF16)) for p, (h, c) in zip(pt, chains)]
        for n, (h, c) in enumerate(chains):
            l_ref[h, :, qs(c)] = alpha[n] * l_prev[n] + jnp.sum(pt[n], axis=0, keepdims=True)
            acc_ref[h, :, qs(c)] = alpha[n] * a_prev[n] + pv[n]
            m_ref[h, :, qs(c)] = m_new[n]

    @pl.when(j < i)
    def _():
        block(False)

    @pl.when(j == i)
    def _():
        block(True)
        for h in range(hb):
            o_ref[:, hs[h]] = (acc_ref[h] / l_ref[h]).T.astype(BF16)


def attn_prompt(qn, qr, kn, kr, vt, *, nb, seq, tq=512, hb=2, cw=256):
    t, hn = qn.shape
    nh = hn // QK_NOPE
    nq = seq // tq
    wb = hb * QK_NOPE
    pairs = [(i, j) for i in range(nq) for j in range(i + 1)]
    qi = jnp.asarray([p[0] for p in pairs], jnp.int32)
    kj = jnp.asarray([p[1] for p in pairs], jnp.int32)
    grid_spec = pltpu.PrefetchScalarGridSpec(
        num_scalar_prefetch=2,
        grid=(nb, nh // hb, len(pairs)),
        in_specs=[pl.BlockSpec((tq, wb), lambda b, h, p, qi, kj: (b * nq + qi[p], h)),
                  pl.BlockSpec((hb, tq, QK_ROPE), lambda b, h, p, qi, kj: (h, b * nq + qi[p], 0)),
                  pl.BlockSpec((tq, wb), lambda b, h, p, qi, kj: (b * nq + kj[p], h)),
                  pl.BlockSpec((tq, QK_ROPE), lambda b, h, p, qi, kj: (b * nq + kj[p], 0)),
                  pl.BlockSpec((wb, tq), lambda b, h, p, qi, kj: (h, b * nq + kj[p]))],
        out_specs=pl.BlockSpec((tq, wb), lambda b, h, p, qi, kj: (b * nq + qi[p], h)),
        scratch_shapes=[pltpu.VMEM((hb, 1, tq), F32), pltpu.VMEM((hb, 1, tq), F32),
                        pltpu.VMEM((hb, QK_NOPE, tq), F32)],
    )
    return pl.pallas_call(
        functools.partial(_attn_prompt_kernel, hb=hb, cw=min(cw, tq)),
        grid_spec=grid_spec,
        out_shape=jax.ShapeDtypeStruct((t, hn), BF16),
        compiler_params=_cparams(("arbitrary", "arbitrary", "arbitrary")),
        name="attn_prompt",
    )(qi, kj, qn, qr, kn, kr, vt)


NEW_ROWS_PAD = 16


def _attn_sample_kernel(pt_ref, q_ref, ck_hbm, kr_hbm, cn_ref, kn_ref, o_ref,
                        ckraw, krraw, ckbuf, krbuf, m_ref, l_ref, acc_ref, sem, *,
                        layer, gs, pg, page, kvl, heads, nsteps, ns):
    b, s_idx = pl.program_id(0), pl.program_id(1)
    ring = ckraw.shape[0]
    step = b * ns + s_idx
    slot = step % ring
    npages = ns * pg

    def start_pages(n, sl):
        bb, ss = n // ns, n % ns
        for g in range(gs):
            for p in range(pg):
                j = g * pg + p
                pid = pt_ref[(bb * gs + g) * npages + ss * pg + p]
                pltpu.make_async_copy(ck_hbm.at[layer, pid], ckraw.at[sl, j], sem.at[sl]).start(priority=j % 2)
                pltpu.make_async_copy(kr_hbm.at[layer, pid], krraw.at[sl, j], sem.at[sl]).start(priority=j % 2)

    @pl.when(step == 0)
    def _():
        for n in range(min(ring - 1, nsteps)):
            start_pages(n, n)

    @pl.when(step + ring - 1 < nsteps)
    def _():
        start_pages(step + ring - 1, (step + ring - 1) % ring)

    @pl.when(s_idx == 0)
    def _():
        m_ref[...] = jnp.full_like(m_ref, NEG_BIG)
        l_ref[...] = jnp.zeros_like(l_ref)
        acc_ref[...] = jnp.zeros_like(acc_ref)

    for j in range(gs * pg):
        pltpu.make_async_copy(ck_hbm.at[layer, 0], ckraw.at[slot, j], sem.at[slot]).wait()
        pltpu.make_async_copy(kr_hbm.at[layer, 0], krraw.at[slot, j], sem.at[slot]).wait()

    for g in range(gs):
        for p in range(pg):
            ckbuf[g, p * page:(p + 1) * page, :] = ckraw[slot, g * pg + p].astype(BF16)
            krbuf[g, :, p * page:(p + 1) * page] = krraw[slot, g * pg + p].astype(BF16)
    s = [_dot_nt(q_ref[g][:, :kvl], ckbuf[g]) + _dot(q_ref[g][:, kvl:], krbuf[g]) for g in range(gs)]
    m_prev = [m_ref[g] for g in range(gs)]
    m_new = [jnp.maximum(m_prev[g], jnp.max(s[g], axis=-1, keepdims=True)) for g in range(gs)]
    alpha = [jnp.exp(m_prev[g] - m_new[g]) for g in range(gs)]
    pr = [jnp.exp(s[g] - m_new[g]) for g in range(gs)]
    pv = [_dot(pr[g].astype(BF16), ckbuf[g]) for g in range(gs)]
    for g in range(gs):
        l_ref[g] = alpha[g] * l_ref[g] + jnp.sum(pr[g], axis=-1, keepdims=True)
        acc_ref[g] = alpha[g] * acc_ref[g] + pv[g]
        m_ref[g] = m_new[g]

    @pl.when(s_idx == pl.num_programs(1) - 1)
    def _():
        nt = cn_ref.shape[1]
        zpad = lambda a: jnp.concatenate([a, jnp.zeros((NEW_ROWS_PAD - nt, a.shape[1]), F32)], axis=0).astype(BF16)
        for g in range(gs):
            q = q_ref[g]
            cn = zpad(cn_ref[g])
            s = _dot_nt(q[:, :kvl], cn) + _dot_nt(q[:, kvl:], zpad(kn_ref[g]))
            row = lax.broadcasted_iota(jnp.int32, s.shape, 0)
            col = lax.broadcasted_iota(jnp.int32, s.shape, 1)
            s = jnp.where(col * heads <= row, s, NEG_BIG)
            _softmax_step(s, cn, m_ref.at[g], l_ref.at[g], acc_ref.at[g])
            o_ref[g] = (acc_ref[g] / l_ref[g]).astype(BF16)


def attn_sample(q, cache_ckv, cache_krope_t, ckv_new, kr_new, page_table, *, layer, heads, gs=2, pg=8, ring=3):
    bs, nq, qd = q.shape
    npages = page_table.shape[1]
    page, kvl = cache_ckv.shape[2], cache_ckv.shape[3]
    nt = ckv_new.shape[1]
    assert nt <= NEW_ROWS_PAD and npages % pg == 0 and bs % gs == 0

    grid_spec = pltpu.PrefetchScalarGridSpec(
        num_scalar_prefetch=1,
        grid=(bs // gs, npages // pg),
        in_specs=[pl.BlockSpec((gs, nq, qd), lambda b, s, pt: (b, 0, 0)),
                  pl.BlockSpec(memory_space=pl.ANY), pl.BlockSpec(memory_space=pl.ANY),
                  pl.BlockSpec((gs, nt, kvl), lambda b, s, pt: (b, 0, 0)),
                  pl.BlockSpec((gs, nt, QK_ROPE), lambda b, s, pt: (b, 0, 0))],
        out_specs=pl.BlockSpec((gs, nq, kvl), lambda b, s, pt: (b, 0, 0)),
        scratch_shapes=[pltpu.VMEM((ring, gs * pg, page, kvl), F32), pltpu.VMEM((ring, gs * pg, QK_ROPE, page), F32),
                        pltpu.VMEM((gs, pg * page, kvl), BF16), pltpu.VMEM((gs, QK_ROPE, pg * page), BF16),
                        pltpu.VMEM((gs, nq, 1), F32), pltpu.VMEM((gs, nq, 1), F32), pltpu.VMEM((gs, nq, kvl), F32),
                        pltpu.SemaphoreType.DMA((ring,))],
    )
    ns = npages // pg
    return pl.pallas_call(
        functools.partial(_attn_sample_kernel, layer=layer, gs=gs, pg=pg, page=page, kvl=kvl, heads=heads,
                          nsteps=(bs // gs) * ns, ns=ns),
        grid_spec=grid_spec,
        out_shape=jax.ShapeDtypeStruct((bs, nq, kvl), BF16),
        compiler_params=_cparams(("arbitrary", "arbitrary")),
        name="attn_sample",
    )(page_table.reshape(-1), q, cache_ckv, cache_krope_t, ckv_new, kr_new)


def _uv_kernel(o_ref, w_ref, y_ref):
    y_ref[...] = _dot(o_ref[...], w_ref[...]).astype(BF16)


def uv_sample(o, wuv_h):
    t = o.shape[0]
    nh, kvl, vd = wuv_h.shape
    return pl.pallas_call(
        _uv_kernel,
        grid=(nh,),
        in_specs=[pl.BlockSpec((t, kvl), lambda h: (0, h)), pl.BlockSpec((None, kvl, vd), lambda h: (h, 0, 0))],
        out_specs=pl.BlockSpec((t, vd), lambda h: (0, h)),
        out_shape=jax.ShapeDtypeStruct((t, nh * vd), BF16),
        compiler_params=_cparams(("arbitrary",)),
        name="uv_sample",
    )(o, wuv_h)


def _merge_route_kernel(*refs, n_real, has_prev):
    if has_prev:
        refs = refs[:14] + refs[15:]
    hn_ref = refs[15]
    i = pl.program_id(0)

    @pl.when(i < n_real)
    def _():
        _merge_route_body(*refs)

    @pl.when(i >= n_real)
    def _():
        hn_ref[...] = jnp.zeros_like(hn_ref)


def _merge_route_body(x_ref, ya_ref, yb_ref, ga_ref, gb_ref, g1_ref, sc_ref, sh_ref, wo_ref, gn_ref,
                      wrh_ref, wrl_ref, br_ref, cin_ref, x1_ref, hn_ref, rt_ref, cnt_ref):
    @pl.when(pl.program_id(0) == 0)
    def _():
        cnt_ref[...] = cin_ref[...]

    tm = x_ref.shape[0]
    parts = [slice(k * (tm // ROUTE_PARTS), (k + 1) * (tm // ROUTE_PARTS)) for k in range(ROUTE_PARTS)]
    mod = lambda ref, r: ref[r] if ref.shape[0] == tm else ref[...]
    sig = lambda v: 0.5 + 0.5 * jnp.tanh(0.5 * v.astype(F32))
    merged = [(sig(ga_ref[r]) * ya_ref[r].astype(F32) + sig(gb_ref[r]) * yb_ref[r].astype(F32)).astype(BF16)
              for r in parts]
    mix = [_dot(m, wo_ref[...]) for m in merged]
    x1 = [x_ref[r] + mod(g1_ref, r) * mx for r, mx in zip(parts, mix)]
    hn = [_rms(v, gn_ref[...]) * (1.0 + mod(sc_ref, r)) + mod(sh_ref, r) for r, v in zip(parts, x1)]
    for r, v, h in zip(parts, x1, hn):
        x1_ref[r] = v
        hn_ref[r] = h

    hh = [h.astype(BF16) for h in hn]
    hl = [(h - hb.astype(F32)).astype(BF16) for h, hb in zip(hn, hh)]
    logits = [_dot(a, wrh_ref[...]) + (_dot(a, wrl_ref[...]) + _dot(b, wrh_ref[...])) + br_ref[...]
              for a, b in zip(hh, hl)]
    pm = tm // ROUTE_PARTS
    lane = lax.broadcasted_iota(jnp.int32, (pm, ROUTE_LANES), 1)
    tri = jnp.where(lax.broadcasted_iota(jnp.int32, (pm, pm), 0) > lax.broadcasted_iota(jnp.int32, (pm, pm), 1),
                    1.0, 0.0).astype(BF16)

    def first_argmax(v):
        mx = jnp.max(v, axis=-1, keepdims=True)
        return mx, jnp.min(jnp.where(v == mx, lane, ROUTE_LANES), axis=-1, keepdims=True)

    cnt = cnt_ref[...]
    for r, lg in zip(parts, logits):
        gl = jnp.where(lane < N_GROUPS, lg, NEG_BIG)
        gmax, gidx = first_argmax(gl)
        p_top = 1.0 / jnp.sum(jnp.exp(gl - gmax), axis=-1, keepdims=True)
        lo = N_GROUPS + gidx * EXPERTS_PER_GROUP
        el = jnp.where((lane >= lo) & (lane < lo + EXPERTS_PER_GROUP), lg, NEG_BIG)
        m1, l1 = first_argmax(el)
        m2, l2 = first_argmax(jnp.where(lane == l1, NEG_BIG, el))
        e21 = jnp.exp(m2 - m1)
        w1 = p_top / (1.0 + e21)
        w2 = p_top * e21 / (1.0 + e21)
        e1, e2 = l1 - N_GROUPS, l2 - N_GROUPS

        oh1, oh2 = lane == e1, lane == e2
        ohs = jnp.where(oh1 | oh2, 1.0, 0.0)
        before = _dot(tri, ohs.astype(BF16)) + cnt
        r1 = jnp.sum(jnp.where(oh1, before, 0.0), axis=-1, keepdims=True)
        r2 = jnp.sum(jnp.where(oh2, before, 0.0), axis=-1, keepdims=True)
        cnt = cnt + jnp.sum(ohs, axis=0, keepdims=True)

        rt = jnp.where(lane == 0, e1.astype(F32), 0.0)
        rt = jnp.where(lane == 1, e2.astype(F32), rt)
        rt = jnp.where(lane == 2, w1, rt)
        rt = jnp.where(lane == 3, w2, rt)
        rt = jnp.where(lane == 4, r1, rt)
        rt_ref[r] = jnp.where(lane == 5, r2, rt)
    cnt_ref[...] = cnt


def merge_route(x, ya, yb, pa, g1, sc2, sh2, wo, gn2, wrh, wrl, br, cnt_in, *, per_seq, tm, h_rows, h_row0, h_prev=None):
    t, d = x.shape
    n_real = t // tm
    assert h_row0 % tm == 0 and (h_rows - t) % tm == 0
    n_steps = n_real if h_prev is not None else h_rows // tm
    ri = lambda i: jnp.minimum(i, n_real - 1)
    if per_seq:
        tps = per_seq // tm
        mod_spec = pl.BlockSpec((None, 1, d), lambda i: (ri(i) // tps, 0, 0))
    else:
        mod_spec = pl.BlockSpec((tm, d), lambda i: (ri(i), 0))
    row = pl.BlockSpec((tm, d), lambda i: (ri(i), 0))
    full = lambda a: pl.BlockSpec(a.shape, lambda i: (0,) * a.ndim)
    args = [x, ya, yb, pa, pa, g1, sc2, sh2, wo, gn2, wrh, wrl, br, cnt_in]
    in_specs = [row, row, row, pl.BlockSpec((tm, d), lambda i: (ri(i), 2)), pl.BlockSpec((tm, d), lambda i: (ri(i), 3)),
                mod_spec, mod_spec, mod_spec, full(wo), full(gn2), full(wrh), full(wrl), full(br), full(cnt_in)]
    aliases = {}
    if h_prev is not None:
        args.append(h_prev)
        in_specs.append(pl.BlockSpec(memory_space=pl.ANY))
        aliases = {len(args) - 1: 1}
    return pl.pallas_call(
        functools.partial(_merge_route_kernel, n_real=n_real, has_prev=h_prev is not None),
        grid=(n_steps,),
        in_specs=in_specs,
        out_specs=[row, pl.BlockSpec((tm, d), lambda i: (i + h_row0 // tm, 0)),
                   pl.BlockSpec((tm, ROUTE_LANES), lambda i: (ri(i), 0)),
                   pl.BlockSpec((1, ROUTE_LANES), lambda i: (0, 0))],
        out_shape=[jax.ShapeDtypeStruct((t, d), F32), jax.ShapeDtypeStruct((h_rows, d), F32),
                   jax.ShapeDtypeStruct((t, ROUTE_LANES), F32), jax.ShapeDtypeStruct((1, ROUTE_LANES), F32)],
        input_output_aliases=aliases,
        compiler_params=_cparams(("arbitrary",)),
        name="merge_route",
    )(*args)


GATHER_UNROLL = 4


def _row_copy(src, dst, sem):
    return pltpu.make_async_copy(src, dst, sem)


def _dispatch_kernel(pos_ref, h_ref, xs_in_ref, xs_ref, sem):
    del xs_in_ref
    i = pl.program_id(0)
    tm = h_ref.shape[0]

    def issue(rb, c):
        for u in range(GATHER_UNROLL):
            r = rb * GATHER_UNROLL + u
            for k in range(2):
                p = pos_ref[(i * tm + r) * 2 + k]
                _row_copy(h_ref.at[pl.ds(r, 1), :], xs_ref.at[pl.ds(p, 1), :], sem).start(priority=k)
        return c

    lax.fori_loop(0, tm // GATHER_UNROLL, issue, 0)
    for k in range(2):
        _row_copy(h_ref, xs_ref.at[pl.ds(0, tm), :], sem).wait()


def dispatch(pos, h_all, xs, *, tm):
    t, d = h_all.shape
    grid_spec = pltpu.PrefetchScalarGridSpec(
        num_scalar_prefetch=1,
        grid=(t // tm,),
        in_specs=[pl.BlockSpec((tm, d), lambda i, pos: (i, 0)), pl.BlockSpec(memory_space=pl.ANY)],
        out_specs=pl.BlockSpec(memory_space=pl.ANY),
        scratch_shapes=[pltpu.SemaphoreType.DMA(())],
    )
    return pl.pallas_call(
        _dispatch_kernel,
        grid_spec=grid_spec,
        out_shape=jax.ShapeDtypeStruct(xs.shape, xs.dtype),
        input_output_aliases={2: 0},
        compiler_params=_cparams(("arbitrary",)),
        name="dispatch",
    )(pos, h_all, xs)


def _gmm_kernel(te_ref, first_ref, nt_ref, x_ref, wg_ref, wu_ref, wd_ref, y_ref, wgb, wub, wdb):
    i = pl.program_id(0)

    @pl.when(i < nt_ref[0])
    def _():
        @pl.when(first_ref[i] == 1)
        def _():
            wgb[...] = wg_ref[...].astype(BF16)
            wub[...] = wu_ref[...].astype(BF16)
            wdb[...] = wd_ref[...].astype(BF16)

        x = x_ref[...].astype(BF16)
        g = _dot(x, wgb[...])
        u = _dot(x, wub[...])
        y_ref[...] = _dot((g * jax.nn.sigmoid(g) * u).astype(BF16), wdb[...])

    @pl.when(i >= nt_ref[0])
    def _():
        y_ref[...] = jnp.zeros_like(y_ref)


def expert_ffn(tile_e, tile_first, n_tiles, xs, w_gate, w_up, w_down, *, layer, tm):
    s, d = xs.shape
    de = w_gate.shape[3]
    last = lambda i, nt: jnp.minimum(i, nt[0] - 1)
    grid_spec = pltpu.PrefetchScalarGridSpec(
        num_scalar_prefetch=3,
        grid=(s // tm,),
        in_specs=[pl.BlockSpec((tm, d), lambda i, te, tf, nt: (last(i, nt), 0)),
                  pl.BlockSpec((None, None, d, de), lambda i, te, tf, nt: (layer, te[i], 0, 0)),
                  pl.BlockSpec((None, None, d, de), lambda i, te, tf, nt: (layer, te[i], 0, 0)),
                  pl.BlockSpec((None, None, de, d), lambda i, te, tf, nt: (layer, te[i], 0, 0))],
        out_specs=pl.BlockSpec((tm, d), lambda i, te, tf, nt: (i, 0)),
        scratch_shapes=[pltpu.VMEM((d, de), BF16), pltpu.VMEM((d, de), BF16), pltpu.VMEM((de, d), BF16)],
    )
    return pl.pallas_call(
        _gmm_kernel,
        grid_spec=grid_spec,
        out_shape=jax.ShapeDtypeStruct((s, d), F32),
        compiler_params=_cparams(("arbitrary",)),
        name="expert_ffn",
    )(tile_e, tile_first, n_tiles, xs, w_gate, w_up, w_down)


def _combine_kernel(pos_ref, x1_ref, g2_ref, rt_ref, gf_ref, ys_ref, o_ref, gbuf, sem):
    i = pl.program_id(0)
    tm = x1_ref.shape[0]

    def issue(rb, c):
        for u in range(GATHER_UNROLL):
            r = rb * GATHER_UNROLL + u
            for k in range(2):
                p = pos_ref[(i * tm + r) * 2 + k]
                _row_copy(ys_ref.at[pl.ds(p, 1), :], gbuf.at[k, pl.ds(r, 1), :], sem).start(priority=k)
        return c

    lax.fori_loop(0, tm // GATHER_UNROLL, issue, 0)
    for k in range(2):
        _row_copy(ys_ref.at[pl.ds(0, tm), :], gbuf.at[k], sem).wait()
    rt = rt_ref[...]
    moe = rt[:, 2:3] * gbuf[0] + rt[:, 3:4] * gbuf[1]
    o_ref[...] = _rms(x1_ref[...] + g2_ref[...] * moe, gf_ref[...])


def combine_final(pos, x1, g2, rt, gf, ys, *, per_seq, tm=256):
    t, d = x1.shape
    if per_seq:
        tps = per_seq // tm
        mod_spec = pl.BlockSpec((None, 1, d), lambda i, pos: (i // tps, 0, 0))
    else:
        mod_spec = pl.BlockSpec((tm, d), lambda i, pos: (i, 0))
    grid_spec = pltpu.PrefetchScalarGridSpec(
        num_scalar_prefetch=1,
        grid=(t // tm,),
        in_specs=[pl.BlockSpec((tm, d), lambda i, pos: (i, 0)), mod_spec,
                  pl.BlockSpec((tm, ROUTE_LANES), lambda i, pos: (i, 0)),
                  pl.BlockSpec((1, d), lambda i, pos: (0, 0)), pl.BlockSpec(memory_space=pl.ANY)],
        out_specs=pl.BlockSpec((tm, d), lambda i, pos: (i, 0)),
        scratch_shapes=[pltpu.VMEM((2, tm, d), F32), pltpu.SemaphoreType.DMA(())],
    )
    return pl.pallas_call(
        _combine_kernel,
        grid_spec=grid_spec,
        out_shape=jax.ShapeDtypeStruct((t, d), F32),
        compiler_params=_cparams(("arbitrary",)),
        name="combine_final",
    )(pos, x1, g2, rt, gf, ys)


def _rope_tables(pos):
    half = QK_ROPE // 2
    freqs = ROPE_THETA ** (-jnp.arange(half, dtype=F32) / half)
    ang = pos[:, None] * freqs[None, :]
    cos, sin = jnp.cos(ang), jnp.sin(ang)
    return jnp.concatenate([cos, cos], axis=1), jnp.concatenate([sin, sin], axis=1)


def _rotate_half_cols(w):
    lead = w.shape[0]
    w3 = w.reshape(lead, -1, QK_ROPE)
    half = QK_ROPE // 2
    return jnp.concatenate([-w3[..., half:], w3[..., :half]], axis=-1).reshape(lead, -1)


def _pick_tile(n, pref):
    t = min(pref, n)
    while n % t:
        t //= 2
    return t


def kernel(x_prompt, x_sample, cache_ckv, cache_krope, state_conv, state_h, page_table, c_prompt, c_sample, w_ada, b_ada, g_norm1, g_norm2, w_in, w_conv, b_conv, w_rg_a, b_rg_a, w_rg_x, b_rg_x, rg_lambda, g_q, w_uq, g_kv, w_uk, w_uv, w_o, w_group, b_group, w_router, b_router, w_gate, w_up, w_down, g_final):
    nb, seq, d = x_prompt.shape
    bs, nt, _ = x_sample.shape
    depth = w_ada.shape[0]
    d_rnn = w_conv.shape[2]
    q_lora, nh, qk_dim = w_uq.shape[1:]
    kv_lora = w_uk.shape[1]
    v_dim = w_uv.shape[3]
    past = page_table.shape[1] * cache_ckv.shape[2]
    scale = float(qk_dim) ** -0.5
    tp, ts = nb * seq, bs * nt
    assert d_rnn == d and qk_dim == QK_NOPE + QK_ROPE and v_dim == QK_NOPE and nh * v_dim == d

    xp = x_prompt.reshape(tp, d)
    xs_tok = x_sample.reshape(ts, d)
    cos_p, sin_p = _rope_tables(jnp.arange(seq, dtype=F32))
    cos_s, sin_s = _rope_tables(jnp.tile(past + jnp.arange(nt, dtype=F32), bs))
    dbl = lambda a: jnp.concatenate([a, a], axis=1)
    c_all = jnp.concatenate([c_prompt, c_sample], axis=0)
    n_c = c_all.shape[0]
    c_all = jnp.pad(c_all, ((0, (-n_c) % 8), (0, 0)))

    outs_p, outs_s = [], []
    for l in range(depth):
        o1, o2, o3, o4, o5, o6 = np.cumsum([d_rnn, d_rnn, q_lora, kv_lora, QK_ROPE, d]).tolist()
        wi = w_in[l]
        wa = jnp.concatenate([wi[:, :o2], wi[:, o5:]], axis=1).astype(BF16)
        w_kr = wi[:, o4:o5]
        wb = jnp.concatenate([wi[:, o2:o4], w_kr, _rotate_half_cols(w_kr)], axis=1).astype(BF16)
        wrg = jnp.concatenate([w_rg_a[l], w_rg_x[l]], axis=2).astype(BF16)
        brg = jnp.concatenate([b_rg_a[l], b_rg_x[l]], axis=1)[:, None, :]
        wq = w_uq[l]
        wqn = wq[:, :, :QK_NOPE].reshape(q_lora, nh * QK_NOPE).astype(BF16)
        wqr_f = wq[:, :, QK_NOPE:].reshape(q_lora, nh * QK_ROPE)
        wqr, wqrot = wqr_f.astype(BF16), _rotate_half_cols(wqr_f).astype(BF16)
        wuk_flat = w_uk[l].reshape(kv_lora, nh * QK_NOPE).astype(BF16)
        wuv_flat = w_uv[l].reshape(kv_lora, nh * v_dim).astype(BF16)
        wukt = jnp.transpose(w_uk[l], (1, 2, 0)).astype(BF16)
        wuv_h = jnp.transpose(w_uv[l], (1, 0, 2)).astype(BF16)
        wo = w_o[l].astype(BF16)
        wr = jnp.concatenate([w_group[l], w_router[l]], axis=1)
        wr = jnp.pad(wr, ((0, 0), (0, ROUTE_LANES - wr.shape[1])))
        wrh = wr.astype(BF16)
        wrl = (wr - wrh.astype(F32)).astype(BF16)
        br = jnp.pad(jnp.concatenate([b_group[l], b_router[l]]), (0, ROUTE_LANES - N_GROUPS - N_EXPERTS))[None, :]
        row = lambda a: a[None, :]

        mod = ada_mod(c_all, w_ada[l], b_ada[l])
        mods_p = [m[:nb, None, :] for m in jnp.split(mod, 6, axis=1)]
        mods_s = [jnp.repeat(m[nb:nb + bs], nt, axis=0) for m in jnp.split(mod, 6, axis=1)]

        tm_p = _pick_tile(seq, 512)
        pa_p, cq_p, ckv_p, kr_p = in_proj(xp, mods_p[1], mods_p[0], row(g_norm1[l]), wa, wb, row(g_q[l]),
                                          row(g_kv[l]), cos_p, sin_p, per_seq=seq, tm=_pick_tile(seq, 1024))
        tm_s = _pick_tile(ts, 512)
        pa_s, cq_s, ckv_s, kr_s = in_proj(xs_tok, mods_s[1], mods_s[0], row(g_norm1[l]), wa, wb, row(g_q[l]),
                                          row(g_kv[l]), cos_s, sin_s, per_seq=0, tm=tm_s)

        ya_p, conv_p, h_p = rglru_seq(pa_p, nb, seq, w_conv[l], row(b_conv[l]), wrg, brg, row(rg_lambda[l]),
                                      tc=_pick_tile(seq, 512), dc=_pick_tile(d_rnn, 1024))
        tmajor = lambda a: jnp.transpose(a.reshape(bs, nt, -1), (1, 0, 2))
        ya_s4, conv_s, h_s = rglru_step(tmajor(pa_s[:, :d_rnn]), tmajor(pa_s[:, d_rnn:2 * d_rnn]),
                                        jnp.transpose(state_conv[l], (1, 0, 2)), state_h[l],
                                        w_conv[l], row(b_conv[l]), wrg, brg, row(rg_lambda[l]))
        ya_s = jnp.transpose(ya_s4, (1, 0, 2)).reshape(ts, d)
        conv_s = jnp.transpose(conv_s, (1, 0, 2))

        qn_p, qr_p, kn_p, vt_p = qkv_prompt(cq_p, ckv_p, wqn, wqr, wqrot, wuk_flat, wuv_flat.T, dbl(cos_p), dbl(sin_p),
                                            seq=seq, scale=scale * LOG2_E, tm=tm_p)
        yb_p = attn_prompt(qn_p, qr_p, kn_p, kr_p, vt_p, nb=nb, seq=seq, tq=tm_p, hb=8)
        q_s = q_sample(cq_s, wqn, wqr, wqrot, wukt, dbl(cos_s), dbl(sin_s), scale=scale)
        q_s = jnp.transpose(q_s, (1, 0, 2)).reshape(bs, nt * nh, kv_lora + QK_ROPE)
        o_s = attn_sample(q_s, cache_ckv, jnp.swapaxes(cache_krope, 2, 3), ckv_s.reshape(bs, nt, kv_lora),
                          kr_s.reshape(bs, nt, QK_ROPE), page_table, layer=l, heads=nh,
                          gs=_pick_tile(bs, 2), pg=_pick_tile(page_table.shape[1], 8))
        yb_s = uv_sample(o_s.reshape(ts, nh * kv_lora), wuv_h)

        tm_r = _pick_tile(seq, 256)
        cnt0 = jnp.zeros((1, ROUTE_LANES), F32)
        tm_rs = _pick_tile(ts, tm_r)
        x1_p, h_all, rt_p, cnt1 = merge_route(xp, ya_p, yb_p, pa_p, mods_p[2], mods_p[4], mods_p[3], wo,
                                              row(g_norm2[l]), wrh, wrl, br, cnt0, per_seq=seq, tm=tm_r,
                                              h_rows=tp + ts, h_row0=0)
        x1_s, h_all, rt_s, cnt2 = merge_route(xs_tok, ya_s, yb_s, pa_s, mods_s[2], mods_s[4], mods_s[3], wo,
                                              row(g_norm2[l]), wrh, wrl, br, cnt1, per_seq=0, tm=tm_rs,
                                              h_rows=tp + ts, h_row0=tp, h_prev=h_all)

        tmg = 256
        counts = cnt2[0, :N_EXPERTS].astype(jnp.int32)
        tiles_e = (counts + tmg - 1) // tmg
        tile_end = jnp.cumsum(tiles_e)
        offs = (tile_end - tiles_e) * tmg
        nt_max = (2 * (tp + ts)) // tmg + N_EXPERTS
        tile_ids = jnp.arange(nt_max, dtype=jnp.int32)
        tile_e = jnp.minimum(jnp.sum((tile_ids[:, None] >= tile_end[None, :]).astype(jnp.int32), axis=1), N_EXPERTS - 1)
        n_tiles = tile_end[-1:].astype(jnp.int32)
        tile_first = jnp.concatenate([jnp.ones((1,), jnp.int32), (tile_e[1:] != tile_e[:-1]).astype(jnp.int32)])

        def slot_pos(rt):
            e = rt[:, 0:2].astype(jnp.int32)
            return (offs[e] + rt[:, 4:6].astype(jnp.int32)).reshape(-1)

        pos_p, pos_s = slot_pos(rt_p), slot_pos(rt_s)
        xs_buf = dispatch(jnp.concatenate([pos_p, pos_s]), h_all, jnp.zeros((nt_max * tmg, d), F32), tm=tm_r)
        ys_buf = expert_ffn(tile_e, tile_first, n_tiles, xs_buf, w_gate, w_up, w_down, layer=l, tm=tmg)

        last = l == depth - 1
        gf = row(g_final) if last else None
        assert last, "final norm is fused into the last layer's combine; DEPTH > 1 needs an un-normed variant"
        xp = combine_final(pos_p, x1_p, mods_p[5], rt_p, gf, ys_buf, per_seq=seq, tm=tm_r)
        xs_tok = combine_final(pos_s, x1_s, mods_s[5], rt_s, gf, ys_buf, per_seq=0, tm=tm_rs)

        outs_p.append((ckv_p.reshape(nb, seq, kv_lora), kr_p.reshape(nb, seq, QK_ROPE), conv_p, h_p.reshape(nb, d)))
        outs_s.append((ckv_s.reshape(bs, nt, kv_lora), kr_s.reshape(bs, nt, QK_ROPE), conv_s, h_s))

    stack = lambda outs, k: jnp.stack([o[k] for o in outs])
    return (xp.reshape(nb, seq, d), xs_tok.reshape(bs, nt, d),
            stack(outs_p, 0), stack(outs_p, 1), stack(outs_p, 2), stack(outs_p, 3),
            stack(outs_s, 0), stack(outs_s, 1), stack(outs_s, 2), stack(outs_s, 3))
```

```python
import functools

import jax
import jax.numpy as jnp
import numpy as np
from jax import lax
from jax.experimental import pallas as pl
from jax.experimental.pallas import tpu as pltpu

F32 = jnp.float32
BF16 = jnp.bfloat16

EPS = 1e-6
RG_C = 8.0
RNN_BLOCK_W = 128
CONV_W = 4
QK_NOPE = 128
QK_ROPE = 64
ROPE_THETA = 10000.0
N_GROUPS = 4
EXPERTS_PER_GROUP = 8
N_EXPERTS = N_GROUPS * EXPERTS_PER_GROUP
ROUTE_LANES = 128
ROUTE_PARTS = 2
NEG_BIG = -1e30
LOG2_E = 1.4426950408889634
VMEM_LIMIT = 56 * 1024 * 1024


def _cparams(sem, vmem=VMEM_LIMIT):
    return pltpu.CompilerParams(dimension_semantics=sem, vmem_limit_bytes=vmem)


def _rms(x, g):
    return x * lax.rsqrt(jnp.mean(x * x, axis=-1, keepdims=True) + EPS) * g


def _gelu_tanh(x):
    return 0.5 * x * (1.0 + jnp.tanh(0.7978845608028654 * (x + 0.044715 * x * x * x)))


def _dot(a, b):
    return jnp.dot(a, b, preferred_element_type=F32)


def _dot_nt(a, b):
    return lax.dot_general(a, b, (((1,), (1,)), ((), ())), preferred_element_type=F32)


def _ada_kernel(c_ref, w_ref, b_ref, o_ref):
    c = c_ref[...]
    s = (c * jax.nn.sigmoid(c)).astype(BF16)
    o_ref[...] = _dot(s, w_ref[...].astype(BF16)) + b_ref[...]


def ada_mod(c_all, w_ada, b_ada, tn=1024):
    m, d = c_all.shape
    n = w_ada.shape[1]
    return pl.pallas_call(
        _ada_kernel,
        grid=(n // tn,),
        in_specs=[pl.BlockSpec((m, d), lambda j: (0, 0)),
                  pl.BlockSpec((d, tn), lambda j: (0, j)),
                  pl.BlockSpec((1, tn), lambda j: (0, j))],
        out_specs=pl.BlockSpec((m, tn), lambda j: (0, j)),
        out_shape=jax.ShapeDtypeStruct((m, n), F32),
        compiler_params=_cparams(("arbitrary",)),
        name="ada_mod",
    )(c_all, w_ada, b_ada.reshape(1, n))


def _inproj_kernel(x_ref, sc_ref, sh_ref, g_ref, wa_ref, wb_ref, gq_ref, gkv_ref, cos_ref, sin_ref,
                   pa_ref, cq_ref, ckv_ref, kr_ref, hn_ref, *, q_lora, kv_lora):
    @pl.when(pl.program_id(1) == 0)
    def _():
        hn = _rms(x_ref[...], g_ref[...]) * (1.0 + sc_ref[...]) + sh_ref[...]
        hb = hn.astype(BF16)
        hn_ref[...] = hb
        pb = _dot(hb, wb_ref[...])
        o1 = q_lora
        o2 = o1 + kv_lora
        cq_ref[...] = _rms(pb[:, :o1], gq_ref[...]).astype(BF16)
        ckv_ref[...] = _rms(pb[:, o1:o2], gkv_ref[...])
        kr_ref[...] = pb[:, o2:o2 + QK_ROPE] * cos_ref[...] + pb[:, o2 + QK_ROPE:o2 + 2 * QK_ROPE] * sin_ref[...]

    pa_ref[...] = _dot(hn_ref[...], wa_ref[...]).astype(BF16)


def in_proj(x, sc, sh, g1, wa, wb, gq, gkv, cos64, sin64, *, per_seq, tm, tn=1024):
    t, d = x.shape
    na = wa.shape[1]
    q_lora, kv_lora = gq.shape[1], gkv.shape[1]
    if per_seq:
        tps = per_seq // tm
        mod_spec = pl.BlockSpec((None, 1, d), lambda i, j: (i // tps, 0, 0))
        tab_spec = pl.BlockSpec((tm, QK_ROPE), lambda i, j: (i % tps, 0))
    else:
        mod_spec = pl.BlockSpec((tm, d), lambda i, j: (i, 0))
        tab_spec = pl.BlockSpec((tm, QK_ROPE), lambda i, j: (i, 0))
    row = lambda w: pl.BlockSpec((tm, w), lambda i, j: (i, 0))
    full = lambda a: pl.BlockSpec(a.shape, lambda i, j: (0,) * a.ndim)
    return pl.pallas_call(
        functools.partial(_inproj_kernel, q_lora=q_lora, kv_lora=kv_lora),
        grid=(t // tm, na // tn),
        in_specs=[row(d), mod_spec, mod_spec, full(g1),
                  pl.BlockSpec((d, tn), lambda i, j: (0, j)), full(wb), full(gq), full(gkv), tab_spec, tab_spec],
        out_specs=[pl.BlockSpec((tm, tn), lambda i, j: (i, j)), row(q_lora), row(kv_lora), row(QK_ROPE)],
        out_shape=[jax.ShapeDtypeStruct((t, na), BF16), jax.ShapeDtypeStruct((t, q_lora), BF16),
                   jax.ShapeDtypeStruct((t, kv_lora), F32), jax.ShapeDtypeStruct((t, QK_ROPE), F32)],
        scratch_shapes=[pltpu.VMEM((tm, d), BF16)],
        compiler_params=_cparams(("arbitrary", "arbitrary")),
        name="in_proj",
    )(x, sc, sh, g1, wa, wb, gq, gkv, cos64, sin64)


def _rg_coeffs(xc, wrg_ref, brg_ref, sp):
    a_parts, b_parts = [], []
    for n in range(xc.shape[1] // RNN_BLOCK_W):
        sl = slice(n * RNN_BLOCK_W, (n + 1) * RNN_BLOCK_W)
        xb = xc[:, sl]
        z = _dot(xb.astype(BF16), wrg_ref[n]) + brg_ref[n]
        g = 0.5 + 0.5 * jnp.tanh(0.5 * z)
        r, i = g[:, :RNN_BLOCK_W], g[:, RNN_BLOCK_W:]
        a = jnp.exp(-RG_C * r * sp[:, sl])
        a_parts.append(a)
        b_parts.append(jnp.sqrt(1.0 - a * a) * i * xb)
    return jnp.concatenate(a_parts, axis=1), jnp.concatenate(b_parts, axis=1)


def _softplus_neg(lam):
    return jnp.maximum(-lam, 0.0) + jnp.log1p(jnp.exp(-jnp.abs(lam)))


def _rglru_seq_kernel(x_ref, g_ref, wc_ref, bc_ref, wrg_ref, brg_ref, lam_ref,
                      y_ref, conv_ref, h_ref, tail_ref, hc_ref, a_scr, b_scr):
    t = pl.program_id(2)
    tc, dc = x_ref.shape

    @pl.when(t == 0)
    def _():
        tail_ref[...] = jnp.zeros_like(tail_ref)
        hc_ref[...] = jnp.zeros_like(hc_ref)

    x = x_ref[...].astype(F32)
    xw = jnp.concatenate([tail_ref[...], x], axis=0)
    xc = bc_ref[...] + wc_ref[CONV_W - 1:CONV_W, :] * x
    for k in range(CONV_W - 1):
        d = CONV_W - 1 - k
        xc = xc + wc_ref[k:k + 1, :] * xw[8 - d:8 - d + tc, :]
    tail_ref[...] = x[tc - 8:, :]

    a, b = _rg_coeffs(xc, wrg_ref, brg_ref, _softplus_neg(lam_ref[...]))
    a_scr[...] = a
    b_scr[...] = b

    row = lax.broadcasted_iota(jnp.int32, (8, dc), 0)

    def body(k, hc):
        r0 = pl.multiple_of(k * 8, 8)
        av = a_scr[pl.ds(r0, 8), :]
        bv = b_scr[pl.ds(r0, 8), :]
        for s in (1, 2, 4):
            a_sh = jnp.where(row >= s, pltpu.roll(av, s, 0), 1.0)
            b_sh = jnp.where(row >= s, pltpu.roll(bv, s, 0), 0.0)
            bv = av * b_sh + bv
            av = av * a_sh
        h = av * hc + bv
        b_scr[pl.ds(r0, 8), :] = h
        return jnp.broadcast_to(h[7:8, :], (8, dc))

    hc = lax.fori_loop(0, tc // 8, body, hc_ref[...])
    hc_ref[...] = hc
    y_ref[...] = (b_scr[...] * _gelu_tanh(g_ref[...].astype(F32))).astype(BF16)

    @pl.when(t == pl.num_programs(2) - 1)
    def _():
        conv_ref[...] = x[tc - (CONV_W - 1):, :]
        h_ref[...] = hc[0:1, :]


def rglru_seq(pa, nb, seq, w_conv, b_conv, wrg, brg, lam, *, tc=512, dc=512):
    d = w_conv.shape[1]
    nc, nt = d // dc, seq // tc
    bpc = dc // RNN_BLOCK_W
    return pl.pallas_call(
        _rglru_seq_kernel,
        grid=(nb, nc, nt),
        in_specs=[pl.BlockSpec((tc, dc), lambda b, c, t: (b * nt + t, c)),
                  pl.BlockSpec((tc, dc), lambda b, c, t: (b * nt + t, nc + c)),
                  pl.BlockSpec((CONV_W, dc), lambda b, c, t: (0, c)),
                  pl.BlockSpec((1, dc), lambda b, c, t: (0, c)),
                  pl.BlockSpec((bpc, RNN_BLOCK_W, 2 * RNN_BLOCK_W), lambda b, c, t: (c, 0, 0)),
                  pl.BlockSpec((bpc, 1, 2 * RNN_BLOCK_W), lambda b, c, t: (c, 0, 0)),
                  pl.BlockSpec((1, dc), lambda b, c, t: (0, c))],
        out_specs=[pl.BlockSpec((tc, dc), lambda b, c, t: (b * nt + t, c)),
                   pl.BlockSpec((None, CONV_W - 1, dc), lambda b, c, t: (b, 0, c)),
                   pl.BlockSpec((None, 1, dc), lambda b, c, t: (b, 0, c))],
        out_shape=[jax.ShapeDtypeStruct((nb * seq, d), BF16),
                   jax.ShapeDtypeStruct((nb, CONV_W - 1, d), F32),
                   jax.ShapeDtypeStruct((nb, 1, d), F32)],
        scratch_shapes=[pltpu.VMEM((8, dc), F32), pltpu.VMEM((8, dc), F32),
                        pltpu.VMEM((tc, dc), F32), pltpu.VMEM((tc, dc), F32)],
        compiler_params=_cparams(("arbitrary", "arbitrary", "arbitrary")),
        name="rglru_seq",
    )(pa, pa, w_conv, b_conv, wrg, brg, lam)


def _rglru_step_kernel(x_ref, g_ref, prev_ref, h0_ref, wc_ref, bc_ref, wrg_ref, brg_ref, lam_ref,
                       y_ref, conv_ref, h_ref):
    nt = x_ref.shape[0]
    full = [prev_ref[k] for k in range(CONV_W - 1)] + [x_ref[k].astype(F32) for k in range(nt)]
    sp = _softplus_neg(lam_ref[...])
    h = h0_ref[...]
    for t in range(nt):
        xc = bc_ref[...] + wc_ref[0:1, :] * full[t]
        for k in range(1, CONV_W):
            xc = xc + wc_ref[k:k + 1, :] * full[t + k]
        a, b = _rg_coeffs(xc, wrg_ref, brg_ref, sp)
        h = a * h + b
        y_ref[t] = (h * _gelu_tanh(g_ref[t].astype(F32))).astype(BF16)
    for k in range(CONV_W - 1):
        conv_ref[k] = full[nt + k]
    h_ref[...] = h


def rglru_step(x4, g4, prev, h0, w_conv, b_conv, wrg, brg, lam, *, dc=512):
    nt, bs, d = x4.shape
    bpc = dc // RNN_BLOCK_W
    return pl.pallas_call(
        _rglru_step_kernel,
        grid=(d // dc,),
        in_specs=[pl.BlockSpec((nt, bs, dc), lambda c: (0, 0, c)),
                  pl.BlockSpec((nt, bs, dc), lambda c: (0, 0, c)),
                  pl.BlockSpec((CONV_W - 1, bs, dc), lambda c: (0, 0, c)),
                  pl.BlockSpec((bs, dc), lambda c: (0, c)),
                  pl.BlockSpec((CONV_W, dc), lambda c: (0, c)),
                  pl.BlockSpec((1, dc), lambda c: (0, c)),
                  pl.BlockSpec((bpc, RNN_BLOCK_W, 2 * RNN_BLOCK_W), lambda c: (c, 0, 0)),
                  pl.BlockSpec((bpc, 1, 2 * RNN_BLOCK_W), lambda c: (c, 0, 0)),
                  pl.BlockSpec((1, dc), lambda c: (0, c))],
        out_specs=[pl.BlockSpec((nt, bs, dc), lambda c: (0, 0, c)),
                   pl.BlockSpec((CONV_W - 1, bs, dc), lambda c: (0, 0, c)),
                   pl.BlockSpec((bs, dc), lambda c: (0, c))],
        out_shape=[jax.ShapeDtypeStruct((nt, bs, d), BF16),
                   jax.ShapeDtypeStruct((CONV_W - 1, bs, d), F32),
                   jax.ShapeDtypeStruct((bs, d), F32)],
        compiler_params=_cparams(("arbitrary",)),
        name="rglru_step",
    )(x4, g4, prev, h0, w_conv, b_conv, wrg, brg, lam)


def _rope_heads(qr, qrot, cos128, sin128, scale, out_ref, lane0):
    for hp in range(qr.shape[1] // 128):
        sl = slice(hp * 128, (hp + 1) * 128)
        v = ((qr[:, sl] * cos128 + qrot[:, sl] * sin128) * scale).astype(BF16)
        out_ref[2 * hp, :, lane0:lane0 + QK_ROPE] = v[:, :QK_ROPE]
        out_ref[2 * hp + 1, :, lane0:lane0 + QK_ROPE] = v[:, QK_ROPE:]


def _qkv_prompt_kernel(cq_ref, ckv_ref, wqn_ref, wqr_ref, wqrot_ref, wuk_ref, wuvt_ref, cos_ref, sin_ref,
                       qn_ref, qr_ref, kn_ref, vt_ref, *, scale):
    cq = cq_ref[...]
    qn_ref[...] = (_dot(cq, wqn_ref[...]) * scale).astype(BF16)
    _rope_heads(_dot(cq, wqr_ref[...]), _dot(cq, wqrot_ref[...]), cos_ref[...], sin_ref[...], scale, qr_ref, 0)
    cb = ckv_ref[...].astype(BF16)
    kn_ref[...] = _dot(cb, wuk_ref[...]).astype(BF16)
    vt_ref[...] = _dot_nt(wuvt_ref[...], cb).astype(BF16)


def qkv_prompt(cq, ckv, wqn, wqr, wqrot, wuk, wuvt, cos128, sin128, *, seq, scale, tm=512):
    t, ql = cq.shape
    kvl = ckv.shape[1]
    hn, hr, hv = wqn.shape[1], wqr.shape[1], wuvt.shape[0]
    nh = hr // QK_ROPE
    tps = seq // tm
    full = lambda a: pl.BlockSpec(a.shape, lambda i: (0,) * a.ndim)
    tab = pl.BlockSpec((tm, 128), lambda i: (i % tps, 0))
    return pl.pallas_call(
        functools.partial(_qkv_prompt_kernel, scale=scale),
        grid=(t // tm,),
        in_specs=[pl.BlockSpec((tm, ql), lambda i: (i, 0)), pl.BlockSpec((tm, kvl), lambda i: (i, 0)),
                  full(wqn), full(wqr), full(wqrot), full(wuk), full(wuvt), tab, tab],
        out_specs=[pl.BlockSpec((tm, hn), lambda i: (i, 0)),
                   pl.BlockSpec((nh, tm, QK_ROPE), lambda i: (0, i, 0)),
                   pl.BlockSpec((tm, hn), lambda i: (i, 0)),
                   pl.BlockSpec((hv, tm), lambda i: (0, i))],
        out_shape=[jax.ShapeDtypeStruct((t, hn), BF16), jax.ShapeDtypeStruct((nh, t, QK_ROPE), BF16),
                   jax.ShapeDtypeStruct((t, hn), BF16), jax.ShapeDtypeStruct((hv, t), BF16)],
        compiler_params=_cparams(("arbitrary",)),
        name="qkv_prompt",
    )(cq, ckv, wqn, wqr, wqrot, wuk, wuvt, cos128, sin128)


def _q_sample_kernel(cq_ref, wqn_ref, wqr_ref, wqrot_ref, wukt_ref, cos_ref, sin_ref, q_ref, *, scale):
    cq = cq_ref[...]
    qn = _dot(cq, wqn_ref[...])
    kvl = wukt_ref.shape[2]
    for h in range(wukt_ref.shape[0]):
        ql = _dot(qn[:, h * QK_NOPE:(h + 1) * QK_NOPE].astype(BF16), wukt_ref[h]) * scale
        q_ref[h, :, :kvl] = ql.astype(BF16)
    _rope_heads(_dot(cq, wqr_ref[...]), _dot(cq, wqrot_ref[...]), cos_ref[...], sin_ref[...], scale, q_ref, kvl)


def q_sample(cq, wqn, wqr, wqrot, wukt, cos128, sin128, *, scale):
    t = cq.shape[0]
    nh, _, kvl = wukt.shape
    full = lambda a: pl.BlockSpec(a.shape, lambda i: (0,) * a.ndim)
    return pl.pallas_call(
        functools.partial(_q_sample_kernel, scale=scale),
        grid=(1,),
        in_specs=[full(cq), full(wqn), full(wqr), full(wqrot), full(wukt), full(cos128), full(sin128)],
        out_specs=pl.BlockSpec((nh, t, kvl + QK_ROPE), lambda i: (0, 0, 0)),
        out_shape=jax.ShapeDtypeStruct((nh, t, kvl + QK_ROPE), BF16),
        compiler_params=_cparams(("arbitrary",)),
        name="q_sample",
    )(cq, wqn, wqr, wqrot, wukt, cos128, sin128)


def _softmax_step(s, v, m_ref, l_ref, acc_ref):
    m_prev = m_ref[...]
    m_new = jnp.maximum(m_prev, jnp.max(s, axis=-1, keepdims=True))
    alpha = jnp.exp(m_prev - m_new)
    p = jnp.exp(s - m_new)
    l_ref[...] = alpha * l_ref[...] + jnp.sum(p, axis=-1, keepdims=True)
    acc_ref[...] = alpha * acc_ref[...] + _dot(p.astype(BF16), v)
    m_ref[...] = m_new


def _attn_prompt_kernel(qi_ref, kj_ref, qn_ref, qr_ref, kn_ref, kr_ref, vt_ref, o_ref, m_ref, l_ref, acc_ref, *, hb, cw):
    p = pl.program_id(2)
    i, j = qi_ref[p], kj_ref[p]
    hs = [slice(h * QK_NOPE, (h + 1) * QK_NOPE) for h in range(hb)]

    @pl.when(j == 0)
    def _():
        m_ref[...] = jnp.full_like(m_ref, NEG_BIG)
        l_ref[...] = jnp.zeros_like(l_ref)
        acc_ref[...] = jnp.zeros_like(acc_ref)

    tq = qn_ref.shape[0]

    def block(diagonal):
        kr = kr_ref[...].astype(BF16)
        kh = [jnp.concatenate([kn_ref[:, hs[h]], kr], axis=1) for h in range(hb)]
        chains = [(h, c) for h in range(hb) for c in range(tq // cw)]
        qs = lambda c: slice(c * cw, (c + 1) * cw)
        nk = lambda c: (c + 1) * cw if diagonal else tq
        m_prev = [m_ref[h, :, qs(c)] for h, c in chains]
        l_prev = [l_ref[h, :, qs(c)] for h, c in chains]
        a_prev = [acc_ref[h, :, qs(c)] for h, c in chains]
        st = [_dot_nt(kh[h][:nk(c)], jnp.concatenate([qn_ref[qs(c), hs[h]], qr_ref[h, qs(c), :]], axis=1))
              for h, c in chains]
        if diagonal:
            visible = lambda s, c: (lax.broadcasted_iota(jnp.int32, s.shape, 0)
                                    <= lax.broadcasted_iota(jnp.int32, s.shape, 1) + c * cw)
            st = [jnp.where(visible(s, c), s, NEG_BIG) for s, (h, c) in zip(st, chains)]
        m_new = [jnp.maximum(mp, jnp.max(s, axis=0, keepdims=True)) for mp, s in zip(m_prev, st)]
        alpha = [jnp.exp2(mp - mn) for mp, mn in zip(m_prev, m_new)]
        pt = [jnp.exp2(s - mn) for s, mn in zip(st, m_new)]
        pv = [_dot(vt_ref[hs[h], :nk(c)], p.astype(BF16)) for p, (h, c) in zip(pt, chains)]
        for n, (h, c) in enumerate(chains):
            l_ref[h, :, qs(c)] = alpha[n] * l_prev[n] + jnp.sum(pt[n], axis=0, keepdims=True)
            acc_ref[h, :, qs(c)] = alpha[n] * a_prev[n] + pv[n]
            m_ref[h, :, qs(c)] = m_new[n]

    @pl.when(j < i)
    def _():
        block(False)

    @pl.when(j == i)
    def _():
        block(True)
        for h in range(hb):
            o_ref[:, hs[h]] = (acc_ref[h] / l_ref[h]).T.astype(BF16)


def attn_prompt(qn, qr, kn, kr, vt, *, nb, seq, tq=512, hb=2, cw=256):
    t, hn = qn.shape
    nh = hn // QK_NOPE
    nq = seq // tq
    wb = hb * QK_NOPE
    pairs = [(i, j) for i in range(nq) for j in range(i + 1)]
    qi = jnp.asarray([p[0] for p in pairs], jnp.int32)
    kj = jnp.asarray([p[1] for p in pairs], jnp.int32)
    grid_spec = pltpu.PrefetchScalarGridSpec(
        num_scalar_prefetch=2,
        grid=(nb, nh // hb, len(pairs)),
        in_specs=[pl.BlockSpec((tq, wb), lambda b, h, p, qi, kj: (b * nq + qi[p], h)),
                  pl.BlockSpec((hb, tq, QK_ROPE), lambda b, h, p, qi, kj: (h, b * nq + qi[p], 0)),
                  pl.BlockSpec((tq, wb), lambda b, h, p, qi, kj: (b * nq + kj[p], h)),
                  pl.BlockSpec((tq, QK_ROPE), lambda b, h, p, qi, kj: (b * nq + kj[p], 0)),
                  pl.BlockSpec((wb, tq), lambda b, h, p, qi, kj: (h, b * nq + kj[p]))],
        out_specs=pl.BlockSpec((tq, wb), lambda b, h, p, qi, kj: (b * nq + qi[p], h)),
        scratch_shapes=[pltpu.VMEM((hb, 1, tq), F32), pltpu.VMEM((hb, 1, tq), F32),
                        pltpu.VMEM((hb, QK_NOPE, tq), F32)],
    )
    return pl.pallas_call(
        functools.partial(_attn_prompt_kernel, hb=hb, cw=min(cw, tq)),
        grid_spec=grid_spec,
        out_shape=jax.ShapeDtypeStruct((t, hn), BF16),
        compiler_params=_cparams(("arbitrary", "arbitrary", "arbitrary")),
        name="attn_prompt",
    )(qi, kj, qn, qr, kn, kr, vt)


NEW_ROWS_PAD = 16


def _attn_sample_kernel(pt_ref, q_ref, ck_hbm, kr_hbm, cn_ref, kn_ref, o_ref,
                        ckraw, krraw, ckbuf, krbuf, m_ref, l_ref, acc_ref, sem, *,
                        layer, gs, pg, page, kvl, heads, nsteps, ns):
    b, s_idx = pl.program_id(0), pl.program_id(1)
    ring = ckraw.shape[0]
    step = b * ns + s_idx
    slot = step % ring
    npages = ns * pg

    def start_pages(n, sl):
        bb, ss = n // ns, n % ns
        for g in range(gs):
            for p in range(pg):
                j = g * pg + p
                pid = pt_ref[(bb * gs + g) * npages + ss * pg + p]
                pltpu.make_async_copy(ck_hbm.at[layer, pid], ckraw.at[sl, j], sem.at[sl]).start(priority=j % 2)
                pltpu.make_async_copy(kr_hbm.at[layer, pid], krraw.at[sl, j], sem.at[sl]).start(priority=j % 2)

    @pl.when(step == 0)
    def _():
        for n in range(min(ring - 1, nsteps)):
            start_pages(n, n)

    @pl.when(step + ring - 1 < nsteps)
    def _():
        start_pages(step + ring - 1, (step + ring - 1) % ring)

    @pl.when(s_idx == 0)
    def _():
        m_ref[...] = jnp.full_like(m_ref, NEG_BIG)
        l_ref[...] = jnp.zeros_like(l_ref)
        acc_ref[...] = jnp.zeros_like(acc_ref)

    for j in range(gs * pg):
        pltpu.make_async_copy(ck_hbm.at[layer, 0], ckraw.at[slot, j], sem.at[slot]).wait()
        pltpu.make_async_copy(kr_hbm.at[layer, 0], krraw.at[slot, j], sem.at[slot]).wait()

    for g in range(gs):
        for p in range(pg):
            ckbuf[g, p * page:(p + 1) * page, :] = ckraw[slot, g * pg + p].astype(BF16)
            krbuf[g, :, p * page:(p + 1) * page] = krraw[slot, g * pg + p].astype(BF16)
    s = [_dot_nt(q_ref[g][:, :kvl], ckbuf[g]) + _dot(q_ref[g][:, kvl:], krbuf[g]) for g in range(gs)]
    m_prev = [m_ref[g] for g in range(gs)]
    m_new = [jnp.maximum(m_prev[g], jnp.max(s[g], axis=-1, keepdims=True)) for g in range(gs)]
    alpha = [jnp.exp(m_prev[g] - m_new[g]) for g in range(gs)]
    pr = [jnp.exp(s[g] - m_new[g]) for g in range(gs)]
    pv = [_dot(pr[g].astype(BF16), ckbuf[g]) for g in range(gs)]
    for g in range(gs):
        l_ref[g] = alpha[g] * l_ref[g] + jnp.sum(pr[g], axis=-1, keepdims=True)
        acc_ref[g] = alpha[g] * acc_ref[g] + pv[g]
        m_ref[g] = m_new[g]

    @pl.when(s_idx == pl.num_programs(1) - 1)
    def _():
        nt = cn_ref.shape[1]
        zpad = lambda a: jnp.concatenate([a, jnp.zeros((NEW_ROWS_PAD - nt, a.shape[1]), F32)], axis=0).astype(BF16)
        for g in range(gs):
            q = q_ref[g]
            cn = zpad(cn_ref[g])
            s = _dot_nt(q[:, :kvl], cn) + _dot_nt(q[:, kvl:], zpad(kn_ref[g]))
            row = lax.broadcasted_iota(jnp.int32, s.shape, 0)
            col = lax.broadcasted_iota(jnp.int32, s.shape, 1)
            s = jnp.where(col * heads <= row, s, NEG_BIG)
            _softmax_step(s, cn, m_ref.at[g], l_ref.at[g], acc_ref.at[g])
            o_ref[g] = (acc_ref[g] / l_ref[g]).astype(BF16)


def attn_sample(q, cache_ckv, cache_krope_t, ckv_new, kr_new, page_table, *, layer, heads, gs=2, pg=8, ring=3):
    bs, nq, qd = q.shape
    npages = page_table.shape[1]
    page, kvl = cache_ckv.shape[2], cache_ckv.shape[3]
    nt = ckv_new.shape[1]
    assert nt <= NEW_ROWS_PAD and npages % pg == 0 and bs % gs == 0

    grid_spec = pltpu.PrefetchScalarGridSpec(
        num_scalar_prefetch=1,
        grid=(bs // gs, npages // pg),
        in_specs=[pl.BlockSpec((gs, nq, qd), lambda b, s, pt: (b, 0, 0)),
                  pl.BlockSpec(memory_space=pl.ANY), pl.BlockSpec(memory_space=pl.ANY),
                  pl.BlockSpec((gs, nt, kvl), lambda b, s, pt: (b, 0, 0)),
                  pl.BlockSpec((gs, nt, QK_ROPE), lambda b, s, pt: (b, 0, 0))],
        out_specs=pl.BlockSpec((gs, nq, kvl), lambda b, s, pt: (b, 0, 0)),
        scratch_shapes=[pltpu.VMEM((ring, gs * pg, page, kvl), F32), pltpu.VMEM((ring, gs * pg, QK_ROPE, page), F32),
                        pltpu.VMEM((gs, pg * page, kvl), BF16), pltpu.VMEM((gs, QK_ROPE, pg * page), BF16),
                        pltpu.VMEM((gs, nq, 1), F32), pltpu.VMEM((gs, nq, 1), F32), pltpu.VMEM((gs, nq, kvl), F32),
                        pltpu.SemaphoreType.DMA((ring,))],
    )
    ns = npages // pg
    return pl.pallas_call(
        functools.partial(_attn_sample_kernel, layer=layer, gs=gs, pg=pg, page=page, kvl=kvl, heads=heads,
                          nsteps=(bs // gs) * ns, ns=ns),
        grid_spec=grid_spec,
        out_shape=jax.ShapeDtypeStruct((bs, nq, kvl), BF16),
        compiler_params=_cparams(("arbitrary", "arbitrary")),
        name="attn_sample",
    )(page_table.reshape(-1), q, cache_ckv, cache_krope_t, ckv_new, kr_new)


def _uv_kernel(o_ref, w_ref, y_ref):
    y_ref[...] = _dot(o_ref[...], w_ref[...]).astype(BF16)


def uv_sample(o, wuv_h):
    t = o.shape[0]
    nh, kvl, vd = wuv_h.shape
    return pl.pallas_call(
        _uv_kernel,
        grid=(nh,),
        in_specs=[pl.BlockSpec((t, kvl), lambda h: (0, h)), pl.BlockSpec((None, kvl, vd), lambda h: (h, 0, 0))],
        out_specs=pl.BlockSpec((t, vd), lambda h: (0, h)),
        out_shape=jax.ShapeDtypeStruct((t, nh * vd), BF16),
        compiler_params=_cparams(("arbitrary",)),
        name="uv_sample",
    )(o, wuv_h)


def _merge_route_kernel(*refs, n_real, has_prev):
    if has_prev:
        refs = refs[:14] + refs[15:]
    hn_ref = refs[15]
    i = pl.program_id(0)

    @pl.when(i < n_real)
    def _():
        _merge_route_body(*refs)

    @pl.when(i >= n_real)
    def _():
        hn_ref[...] = jnp.zeros_like(hn_ref)


def _merge_route_body(x_ref, ya_ref, yb_ref, ga_ref, gb_ref, g1_ref, sc_ref, sh_ref, wo_ref, gn_ref,
                      wrh_ref, wrl_ref, br_ref, cin_ref, x1_ref, hn_ref, rt_ref, cnt_ref, rtt_ref):
    @pl.when(pl.program_id(0) == 0)
    def _():
        cnt_ref[...] = cin_ref[...]

    tm = x_ref.shape[0]
    parts = [slice(k * (tm // ROUTE_PARTS), (k + 1) * (tm // ROUTE_PARTS)) for k in range(ROUTE_PARTS)]
    mod = lambda ref, r: ref[r] if ref.shape[0] == tm else ref[...]
    sig = lambda v: 0.5 + 0.5 * jnp.tanh(0.5 * v.astype(F32))
    merged = [(sig(ga_ref[r]) * ya_ref[r].astype(F32) + sig(gb_ref[r]) * yb_ref[r].astype(F32)).astype(BF16)
              for r in parts]
    mix = [_dot(m, wo_ref[...]) for m in merged]
    x1 = [x_ref[r] + mod(g1_ref, r) * mx for r, mx in zip(parts, mix)]
    hn = [_rms(v, gn_ref[...]) * (1.0 + mod(sc_ref, r)) + mod(sh_ref, r) for r, v in zip(parts, x1)]
    for r, v, h in zip(parts, x1, hn):
        x1_ref[r] = v
        hn_ref[r] = h

    hh = [h.astype(BF16) for h in hn]
    hl = [(h - hb.astype(F32)).astype(BF16) for h, hb in zip(hn, hh)]
    logits = [_dot(a, wrh_ref[...]) + (_dot(a, wrl_ref[...]) + _dot(b, wrh_ref[...])) + br_ref[...]
              for a, b in zip(hh, hl)]
    pm = tm // ROUTE_PARTS
    lane = lax.broadcasted_iota(jnp.int32, (pm, ROUTE_LANES), 1)
    tri = jnp.where(lax.broadcasted_iota(jnp.int32, (pm, pm), 0) > lax.broadcasted_iota(jnp.int32, (pm, pm), 1),
                    1.0, 0.0).astype(BF16)

    def first_argmax(v):
        mx = jnp.max(v, axis=-1, keepdims=True)
        return mx, jnp.min(jnp.where(v == mx, lane, ROUTE_LANES), axis=-1, keepdims=True)

    cnt = cnt_ref[...]
    for r, lg in zip(parts, logits):
        gl = jnp.where(lane < N_GROUPS, lg, NEG_BIG)
        gmax, gidx = first_argmax(gl)
        p_top = 1.0 / jnp.sum(jnp.exp(gl - gmax), axis=-1, keepdims=True)
        lo = N_GROUPS + gidx * EXPERTS_PER_GROUP
        el = jnp.where((lane >= lo) & (lane < lo + EXPERTS_PER_GROUP), lg, NEG_BIG)
        m1, l1 = first_argmax(el)
        m2, l2 = first_argmax(jnp.where(lane == l1, NEG_BIG, el))
        e21 = jnp.exp(m2 - m1)
        w1 = p_top / (1.0 + e21)
        w2 = p_top * e21 / (1.0 + e21)
        e1, e2 = l1 - N_GROUPS, l2 - N_GROUPS

        oh1, oh2 = lane == e1, lane == e2
        ohs = jnp.where(oh1 | oh2, 1.0, 0.0)
        before = _dot(tri, ohs.astype(BF16)) + cnt
        r1 = jnp.sum(jnp.where(oh1, before, 0.0), axis=-1, keepdims=True)
        r2 = jnp.sum(jnp.where(oh2, before, 0.0), axis=-1, keepdims=True)
        cnt = cnt + jnp.sum(ohs, axis=0, keepdims=True)

        rt = jnp.where(lane == 0, e1.astype(F32), 0.0)
        rt = jnp.where(lane == 1, e2.astype(F32), rt)
        rt = jnp.where(lane == 2, w1, rt)
        rt = jnp.where(lane == 3, w2, rt)
        rt = jnp.where(lane == 4, r1, rt)
        rt = jnp.where(lane == 5, r2, rt)
        rt_ref[r] = rt
        rtt_ref[:, r] = rt.T[:8, :]
    cnt_ref[...] = cnt


def merge_route(x, ya, yb, pa, g1, sc2, sh2, wo, gn2, wrh, wrl, br, cnt_in, *, per_seq, tm, h_rows, h_row0, h_prev=None):
    t, d = x.shape
    n_real = t // tm
    assert h_row0 % tm == 0 and (h_rows - t) % tm == 0
    n_steps = n_real if h_prev is not None else h_rows // tm
    ri = lambda i: jnp.minimum(i, n_real - 1)
    if per_seq:
        tps = per_seq // tm
        mod_spec = pl.BlockSpec((None, 1, d), lambda i: (ri(i) // tps, 0, 0))
    else:
        mod_spec = pl.BlockSpec((tm, d), lambda i: (ri(i), 0))
    row = pl.BlockSpec((tm, d), lambda i: (ri(i), 0))
    full = lambda a: pl.BlockSpec(a.shape, lambda i: (0,) * a.ndim)
    args = [x, ya, yb, pa, pa, g1, sc2, sh2, wo, gn2, wrh, wrl, br, cnt_in]
    in_specs = [row, row, row, pl.BlockSpec((tm, d), lambda i: (ri(i), 2)), pl.BlockSpec((tm, d), lambda i: (ri(i), 3)),
                mod_spec, mod_spec, mod_spec, full(wo), full(gn2), full(wrh), full(wrl), full(br), full(cnt_in)]
    aliases = {}
    if h_prev is not None:
        args.append(h_prev)
        in_specs.append(pl.BlockSpec(memory_space=pl.ANY))
        aliases = {len(args) - 1: 1}
    return pl.pallas_call(
        functools.partial(_merge_route_kernel, n_real=n_real, has_prev=h_prev is not None),
        grid=(n_steps,),
        in_specs=in_specs,
        out_specs=[row, pl.BlockSpec((tm, d), lambda i: (i + h_row0 // tm, 0)),
                   pl.BlockSpec((tm, ROUTE_LANES), lambda i: (ri(i), 0)),
                   pl.BlockSpec((1, ROUTE_LANES), lambda i: (0, 0)),
                   pl.BlockSpec((8, tm), lambda i: (0, ri(i)))],
        out_shape=[jax.ShapeDtypeStruct((t, d), F32), jax.ShapeDtypeStruct((h_rows, d), F32),
                   jax.ShapeDtypeStruct((t, ROUTE_LANES), F32), jax.ShapeDtypeStruct((1, ROUTE_LANES), F32),
                   jax.ShapeDtypeStruct((8, t), F32)],
        input_output_aliases=aliases,
        compiler_params=_cparams(("arbitrary",)),
        name="merge_route",
    )(*args)


GATHER_UNROLL = 8


def _row_copy(src, dst, sem):
    return pltpu.make_async_copy(src, dst, sem)


def _dispatch_kernel(pos_ref, h_ref, xs_in_ref, xs_ref, sem):
    del xs_in_ref
    i = pl.program_id(0)
    tm = h_ref.shape[0]
    nrows = pl.num_programs(0) * tm

    def issue(rb, c):
        for u in range(GATHER_UNROLL):
            r = rb * GATHER_UNROLL + u
            for k in range(2):
                p = pos_ref[k * nrows + i * tm + r]
                _row_copy(h_ref.at[pl.ds(r, 1), :], xs_ref.at[pl.ds(p, 1), :], sem).start(priority=k)
        return c

    lax.fori_loop(0, tm // GATHER_UNROLL, issue, 0)
    for k in range(2):
        _row_copy(h_ref, xs_ref.at[pl.ds(0, tm), :], sem).wait()


def dispatch(pos, h_all, xs, *, tm):
    t, d = h_all.shape
    grid_spec = pltpu.PrefetchScalarGridSpec(
        num_scalar_prefetch=1,
        grid=(t // tm,),
        in_specs=[pl.BlockSpec((tm, d), lambda i, pos: (i, 0)), pl.BlockSpec(memory_space=pl.ANY)],
        out_specs=pl.BlockSpec(memory_space=pl.ANY),
        scratch_shapes=[pltpu.SemaphoreType.DMA(())],
    )
    return pl.pallas_call(
        _dispatch_kernel,
        grid_spec=grid_spec,
        out_shape=jax.ShapeDtypeStruct(xs.shape, xs.dtype),
        input_output_aliases={2: 0},
        compiler_params=_cparams(("arbitrary",)),
        name="dispatch",
    )(pos, h_all, xs)


def _gmm_kernel(te_ref, first_ref, nt_ref, x_ref, wg_ref, wu_ref, wd_ref, y_ref, wgb, wub, wdb):
    i = pl.program_id(0)

    @pl.when(i < nt_ref[0])
    def _():
        @pl.when(first_ref[i] == 1)
        def _():
            wgb[...] = wg_ref[...].astype(BF16)
            wub[...] = wu_ref[...].astype(BF16)
            wdb[...] = wd_ref[...].astype(BF16)

        x = x_ref[...].astype(BF16)
        g = _dot(x, wgb[...])
        u = _dot(x, wub[...])
        y_ref[...] = _dot((g * jax.nn.sigmoid(g) * u).astype(BF16), wdb[...])

    @pl.when(i >= nt_ref[0])
    def _():
        y_ref[...] = jnp.zeros_like(y_ref)


def expert_ffn(tile_e, tile_first, n_tiles, xs, w_gate, w_up, w_down, *, layer, tm):
    s, d = xs.shape
    de = w_gate.shape[3]
    last = lambda i, nt: jnp.minimum(i, nt[0] - 1)
    grid_spec = pltpu.PrefetchScalarGridSpec(
        num_scalar_prefetch=3,
        grid=(s // tm,),
        in_specs=[pl.BlockSpec((tm, d), lambda i, te, tf, nt: (last(i, nt), 0)),
                  pl.BlockSpec((None, None, d, de), lambda i, te, tf, nt: (layer, te[i], 0, 0)),
                  pl.BlockSpec((None, None, d, de), lambda i, te, tf, nt: (layer, te[i], 0, 0)),
                  pl.BlockSpec((None, None, de, d), lambda i, te, tf, nt: (layer, te[i], 0, 0))],
        out_specs=pl.BlockSpec((tm, d), lambda i, te, tf, nt: (i, 0)),
        scratch_shapes=[pltpu.VMEM((d, de), BF16), pltpu.VMEM((d, de), BF16), pltpu.VMEM((de, d), BF16)],
    )
    return pl.pallas_call(
        _gmm_kernel,
        grid_spec=grid_spec,
        out_shape=jax.ShapeDtypeStruct((s, d), F32),
        compiler_params=_cparams(("arbitrary",)),
        name="expert_ffn",
    )(tile_e, tile_first, n_tiles, xs, w_gate, w_up, w_down)


def _combine_kernel(pos_ref, x1_ref, g2_ref, rt_ref, gf_ref, ys_ref, o_ref, gbuf, sem):
    i = pl.program_id(0)
    tm = x1_ref.shape[0]
    nrows = pl.num_programs(0) * tm

    def issue(rb, c):
        for u in range(GATHER_UNROLL):
            r = rb * GATHER_UNROLL + u
            for k in range(2):
                p = pos_ref[k * nrows + i * tm + r]
                _row_copy(ys_ref.at[pl.ds(p, 1), :], gbuf.at[k, pl.ds(r, 1), :], sem).start(priority=k)
        return c

    lax.fori_loop(0, tm // GATHER_UNROLL, issue, 0)
    for k in range(2):
        _row_copy(ys_ref.at[pl.ds(0, tm), :], gbuf.at[k], sem).wait()
    rt = rt_ref[...]
    moe = rt[:, 2:3] * gbuf[0] + rt[:, 3:4] * gbuf[1]
    o_ref[...] = _rms(x1_ref[...] + g2_ref[...] * moe, gf_ref[...])


def combine_final(pos, x1, g2, rt, gf, ys, *, per_seq, tm=256):
    t, d = x1.shape
    if per_seq:
        tps = per_seq // tm
        mod_spec = pl.BlockSpec((None, 1, d), lambda i, pos: (i // tps, 0, 0))
    else:
        mod_spec = pl.BlockSpec((tm, d), lambda i, pos: (i, 0))
    grid_spec = pltpu.PrefetchScalarGridSpec(
        num_scalar_prefetch=1,
        grid=(t // tm,),
        in_specs=[pl.BlockSpec((tm, d), lambda i, pos: (i, 0)), mod_spec,
                  pl.BlockSpec((tm, ROUTE_LANES), lambda i, pos: (i, 0)),
                  pl.BlockSpec((1, d), lambda i, pos: (0, 0)), pl.BlockSpec(memory_space=pl.ANY)],
        out_specs=pl.BlockSpec((tm, d), lambda i, pos: (i, 0)),
        scratch_shapes=[pltpu.VMEM((2, tm, d), F32), pltpu.SemaphoreType.DMA(())],
    )
    return pl.pallas_call(
        _combine_kernel,
        grid_spec=grid_spec,
        out_shape=jax.ShapeDtypeStruct((t, d), F32),
        compiler_params=_cparams(("arbitrary",)),
        name="combine_final",
    )(pos, x1, g2, rt, gf, ys)


def _rope_tables(pos):
    half = QK_ROPE // 2
    freqs = ROPE_THETA ** (-jnp.arange(half, dtype=F32) / half)
    ang = pos[:, None] * freqs[None, :]
    cos, sin = jnp.cos(ang), jnp.sin(ang)
    return jnp.concatenate([cos, cos], axis=1), jnp.concatenate([sin, sin], axis=1)


def _rotate_half_cols(w):
    lead = w.shape[0]
    w3 = w.reshape(lead, -1, QK_ROPE)
    half = QK_ROPE // 2
    return jnp.concatenate([-w3[..., half:], w3[..., :half]], axis=-1).reshape(lead, -1)


def _pick_tile(n, pref):
    t = min(pref, n)
    while n % t:
        t //= 2
    return t


def kernel(x_prompt, x_sample, cache_ckv, cache_krope, state_conv, state_h, page_table, c_prompt, c_sample, w_ada, b_ada, g_norm1, g_norm2, w_in, w_conv, b_conv, w_rg_a, b_rg_a, w_rg_x, b_rg_x, rg_lambda, g_q, w_uq, g_kv, w_uk, w_uv, w_o, w_group, b_group, w_router, b_router, w_gate, w_up, w_down, g_final):
    nb, seq, d = x_prompt.shape
    bs, nt, _ = x_sample.shape
    depth = w_ada.shape[0]
    d_rnn = w_conv.shape[2]
    q_lora, nh, qk_dim = w_uq.shape[1:]
    kv_lora = w_uk.shape[1]
    v_dim = w_uv.shape[3]
    past = page_table.shape[1] * cache_ckv.shape[2]
    scale = float(qk_dim) ** -0.5
    tp, ts = nb * seq, bs * nt
    assert d_rnn == d and qk_dim == QK_NOPE + QK_ROPE and v_dim == QK_NOPE and nh * v_dim == d

    xp = x_prompt.reshape(tp, d)
    xs_tok = x_sample.reshape(ts, d)
    cos_p, sin_p = _rope_tables(jnp.arange(seq, dtype=F32))
    cos_s, sin_s = _rope_tables(jnp.tile(past + jnp.arange(nt, dtype=F32), bs))
    dbl = lambda a: jnp.concatenate([a, a], axis=1)
    c_all = jnp.concatenate([c_prompt, c_sample], axis=0)
    n_c = c_all.shape[0]
    c_all = jnp.pad(c_all, ((0, (-n_c) % 8), (0, 0)))

    outs_p, outs_s = [], []
    for l in range(depth):
        o1, o2, o3, o4, o5, o6 = np.cumsum([d_rnn, d_rnn, q_lora, kv_lora, QK_ROPE, d]).tolist()
        wi = w_in[l]
        wa = jnp.concatenate([wi[:, :o2], wi[:, o5:]], axis=1).astype(BF16)
        w_kr = wi[:, o4:o5]
        wb = jnp.concatenate([wi[:, o2:o4], w_kr, _rotate_half_cols(w_kr)], axis=1).astype(BF16)
        wrg = jnp.concatenate([w_rg_a[l], w_rg_x[l]], axis=2).astype(BF16)
        brg = jnp.concatenate([b_rg_a[l], b_rg_x[l]], axis=1)[:, None, :]
        wq = w_uq[l]
        wqn = wq[:, :, :QK_NOPE].reshape(q_lora, nh * QK_NOPE).astype(BF16)
        wqr_f = wq[:, :, QK_NOPE:].reshape(q_lora, nh * QK_ROPE)
        wqr, wqrot = wqr_f.astype(BF16), _rotate_half_cols(wqr_f).astype(BF16)
        wuk_flat = w_uk[l].reshape(kv_lora, nh * QK_NOPE).astype(BF16)
        wuv_flat = w_uv[l].reshape(kv_lora, nh * v_dim).astype(BF16)
        wukt = jnp.transpose(w_uk[l], (1, 2, 0)).astype(BF16)
        wuv_h = jnp.transpose(w_uv[l], (1, 0, 2)).astype(BF16)
        wo = w_o[l].astype(BF16)
        wr = jnp.concatenate([w_group[l], w_router[l]], axis=1)
        wr = jnp.pad(wr, ((0, 0), (0, ROUTE_LANES - wr.shape[1])))
        wrh = wr.astype(BF16)
        wrl = (wr - wrh.astype(F32)).astype(BF16)
        br = jnp.pad(jnp.concatenate([b_group[l], b_router[l]]), (0, ROUTE_LANES - N_GROUPS - N_EXPERTS))[None, :]
        row = lambda a: a[None, :]

        mod = ada_mod(c_all, w_ada[l], b_ada[l])
        mods_p = [m[:nb, None, :] for m in jnp.split(mod, 6, axis=1)]
        mods_s = [jnp.repeat(m[nb:nb + bs], nt, axis=0) for m in jnp.split(mod, 6, axis=1)]

        tm_p = _pick_tile(seq, 512)
        pa_p, cq_p, ckv_p, kr_p = in_proj(xp, mods_p[1], mods_p[0], row(g_norm1[l]), wa, wb, row(g_q[l]),
                                          row(g_kv[l]), cos_p, sin_p, per_seq=seq, tm=_pick_tile(seq, 1024))
        tm_s = _pick_tile(ts, 512)
        pa_s, cq_s, ckv_s, kr_s = in_proj(xs_tok, mods_s[1], mods_s[0], row(g_norm1[l]), wa, wb, row(g_q[l]),
                                          row(g_kv[l]), cos_s, sin_s, per_seq=0, tm=tm_s)

        ya_p, conv_p, h_p = rglru_seq(pa_p, nb, seq, w_conv[l], row(b_conv[l]), wrg, brg, row(rg_lambda[l]),
                                      tc=_pick_tile(seq, 1024), dc=_pick_tile(d_rnn, 2048))
        tmajor = lambda a: jnp.transpose(a.reshape(bs, nt, -1), (1, 0, 2))
        ya_s4, conv_s, h_s = rglru_step(tmajor(pa_s[:, :d_rnn]), tmajor(pa_s[:, d_rnn:2 * d_rnn]),
                                        jnp.transpose(state_conv[l], (1, 0, 2)), state_h[l],
                                        w_conv[l], row(b_conv[l]), wrg, brg, row(rg_lambda[l]))
        ya_s = jnp.transpose(ya_s4, (1, 0, 2)).reshape(ts, d)
        conv_s = jnp.transpose(conv_s, (1, 0, 2))

        qn_p, qr_p, kn_p, vt_p = qkv_prompt(cq_p, ckv_p, wqn, wqr, wqrot, wuk_flat, wuv_flat.T, dbl(cos_p), dbl(sin_p),
                                            seq=seq, scale=scale * LOG2_E, tm=_pick_tile(seq, 1024))
        yb_p = attn_prompt(qn_p, qr_p, kn_p, kr_p, vt_p, nb=nb, seq=seq, tq=tm_p, hb=8)
        q_s = q_sample(cq_s, wqn, wqr, wqrot, wukt, dbl(cos_s), dbl(sin_s), scale=scale)
        q_s = jnp.transpose(q_s, (1, 0, 2)).reshape(bs, nt * nh, kv_lora + QK_ROPE)
        o_s = attn_sample(q_s, cache_ckv, jnp.swapaxes(cache_krope, 2, 3), ckv_s.reshape(bs, nt, kv_lora),
                          kr_s.reshape(bs, nt, QK_ROPE), page_table, layer=l, heads=nh,
                          gs=_pick_tile(bs, 4), pg=_pick_tile(page_table.shape[1], 8), ring=4)
        yb_s = uv_sample(o_s.reshape(ts, nh * kv_lora), wuv_h)

        tm_r = _pick_tile(seq, 256)
        cnt0 = jnp.zeros((1, ROUTE_LANES), F32)
        tm_rs = _pick_tile(ts, tm_r)
        x1_p, h_all, rt_p, cnt1, rtt_p = merge_route(xp, ya_p, yb_p, pa_p, mods_p[2], mods_p[4], mods_p[3], wo,
                                              row(g_norm2[l]), wrh, wrl, br, cnt0, per_seq=seq, tm=tm_r,
                                              h_rows=tp + ts, h_row0=0)
        x1_s, h_all, rt_s, cnt2, rtt_s = merge_route(xs_tok, ya_s, yb_s, pa_s, mods_s[2], mods_s[4], mods_s[3], wo,
                                              row(g_norm2[l]), wrh, wrl, br, cnt1, per_seq=0, tm=tm_rs,
                                              h_rows=tp + ts, h_row0=tp, h_prev=h_all)

        tmg = 256
        counts = cnt2[0, :N_EXPERTS].astype(jnp.int32)
        tiles_e = (counts + tmg - 1) // tmg
        tile_end = jnp.cumsum(tiles_e)
        offs = (tile_end - tiles_e) * tmg
        nt_max = (2 * (tp + ts)) // tmg + N_EXPERTS
        tile_ids = jnp.arange(nt_max, dtype=jnp.int32)
        tile_e = jnp.minimum(jnp.sum((tile_ids[:, None] >= tile_end[None, :]).astype(jnp.int32), axis=1), N_EXPERTS - 1)
        n_tiles = tile_end[-1:].astype(jnp.int32)
        tile_first = jnp.concatenate([jnp.ones((1,), jnp.int32), (tile_e[1:] != tile_e[:-1]).astype(jnp.int32)])

        def slot_pos(rtt):
            e = rtt[0:2].astype(jnp.int32)
            hit = e[None] == jnp.arange(N_EXPERTS, dtype=jnp.int32)[:, None, None]
            return jnp.sum(jnp.where(hit, offs[:, None, None], 0), axis=0) + rtt[4:6].astype(jnp.int32)

        pos_p, pos_s = slot_pos(rtt_p), slot_pos(rtt_s)
        pos_all = jnp.concatenate([pos_p, pos_s], axis=1).reshape(-1)
        pos_p, pos_s = pos_p.reshape(-1), pos_s.reshape(-1)
        xs_buf = dispatch(pos_all, h_all, jnp.zeros((nt_max * tmg, d), F32), tm=tm_r)
        ys_buf = expert_ffn(tile_e, tile_first, n_tiles, xs_buf, w_gate, w_up, w_down, layer=l, tm=tmg)

        last = l == depth - 1
        gf = row(g_final) if last else None
        assert last, "final norm is fused into the last layer's combine; DEPTH > 1 needs an un-normed variant"
        xp = combine_final(pos_p, x1_p, mods_p[5], rt_p, gf, ys_buf, per_seq=seq, tm=tm_r)
        xs_tok = combine_final(pos_s, x1_s, mods_s[5], rt_s, gf, ys_buf, per_seq=0, tm=tm_rs)

        outs_p.append((ckv_p.reshape(nb, seq, kv_lora), kr_p.reshape(nb, seq, QK_ROPE), conv_p, h_p.reshape(nb, d)))
        outs_s.append((ckv_s.reshape(bs, nt, kv_lora), kr_s.reshape(bs, nt, QK_ROPE), conv_s, h_s))

    stack = lambda outs, k: jnp.stack([o[k] for o in outs])
    return (xp.reshape(nb, seq, d), xs_tok.reshape(bs, nt, d),
            stack(outs_p, 0), stack(outs_p, 1), stack(outs_p, 2), stack(outs_p, 3),
            stack(outs_s, 0), stack(outs_s, 1), stack(outs_s, 2), stack(outs_s, 3))
```

```python
import functools

import jax
import jax.numpy as jnp
import numpy as np
from jax import lax
from jax.experimental import pallas as pl
from jax.experimental.pallas import tpu as pltpu

F32 = jnp.float32
BF16 = jnp.bfloat16

EPS = 1e-6
RG_C = 8.0
RNN_BLOCK_W = 128
CONV_W = 4
QK_NOPE = 128
QK_ROPE = 64
ROPE_THETA = 10000.0
N_GROUPS = 4
EXPERTS_PER_GROUP = 8
N_EXPERTS = N_GROUPS * EXPERTS_PER_GROUP
ROUTE_LANES = 128
ROUTE_PARTS = 2
NEG_BIG = -1e30
LOG2_E = 1.4426950408889634
VMEM_LIMIT = 56 * 1024 * 1024


def _cparams(sem, vmem=VMEM_LIMIT):
    return pltpu.CompilerParams(dimension_semantics=sem, vmem_limit_bytes=vmem)


def _rms(x, g):
    return x * lax.rsqrt(jnp.mean(x * x, axis=-1, keepdims=True) + EPS) * g


def _gelu_tanh(x):
    return 0.5 * x * (1.0 + jnp.tanh(0.7978845608028654 * (x + 0.044715 * x * x * x)))


def _dot(a, b):
    return jnp.dot(a, b, preferred_element_type=F32)


def _dot_nt(a, b):
    return lax.dot_general(a, b, (((1,), (1,)), ((), ())), preferred_element_type=F32)


def _ada_kernel(c_ref, w_ref, b_ref, o_ref):
    c = c_ref[...]
    s = (c * jax.nn.sigmoid(c)).astype(BF16)
    o_ref[...] = _dot(s, w_ref[...].astype(BF16)) + b_ref[...]


def ada_mod(c_all, w_ada, b_ada, tn=1024):
    m, d = c_all.shape
    n = w_ada.shape[1]
    return pl.pallas_call(
        _ada_kernel,
        grid=(n // tn,),
        in_specs=[pl.BlockSpec((m, d), lambda j: (0, 0)),
                  pl.BlockSpec((d, tn), lambda j: (0, j)),
                  pl.BlockSpec((1, tn), lambda j: (0, j))],
        out_specs=pl.BlockSpec((m, tn), lambda j: (0, j)),
        out_shape=jax.ShapeDtypeStruct((m, n), F32),
        compiler_params=_cparams(("arbitrary",)),
        name="ada_mod",
    )(c_all, w_ada, b_ada.reshape(1, n))


def _inproj_kernel(x_ref, sc_ref, sh_ref, g_ref, wa_ref, wb_ref, gq_ref, gkv_ref, cos_ref, sin_ref,
                   pa_ref, cq_ref, ckv_ref, kr_ref, hn_ref, *, q_lora, kv_lora):
    @pl.when(pl.program_id(1) == 0)
    def _():
        hn = _rms(x_ref[...], g_ref[...]) * (1.0 + sc_ref[...]) + sh_ref[...]
        hb = hn.astype(BF16)
        hn_ref[...] = hb
        pb = _dot(hb, wb_ref[...])
        o1 = q_lora
        o2 = o1 + kv_lora
        cq_ref[...] = _rms(pb[:, :o1], gq_ref[...]).astype(BF16)
        ckv_ref[...] = _rms(pb[:, o1:o2], gkv_ref[...])
        kr_ref[...] = pb[:, o2:o2 + QK_ROPE] * cos_ref[...] + pb[:, o2 + QK_ROPE:o2 + 2 * QK_ROPE] * sin_ref[...]

    pa_ref[...] = _dot(hn_ref[...], wa_ref[...]).astype(BF16)


def in_proj(x, sc, sh, g1, wa, wb, gq, gkv, cos64, sin64, *, per_seq, tm, tn=1024):
    t, d = x.shape
    na = wa.shape[1]
    q_lora, kv_lora = gq.shape[1], gkv.shape[1]
    if per_seq:
        tps = per_seq // tm
        mod_spec = pl.BlockSpec((None, 1, d), lambda i, j: (i // tps, 0, 0))
        tab_spec = pl.BlockSpec((tm, QK_ROPE), lambda i, j: (i % tps, 0))
    else:
        mod_spec = pl.BlockSpec((tm, d), lambda i, j: (i, 0))
        tab_spec = pl.BlockSpec((tm, QK_ROPE), lambda i, j: (i, 0))
    row = lambda w: pl.BlockSpec((tm, w), lambda i, j: (i, 0))
    full = lambda a: pl.BlockSpec(a.shape, lambda i, j: (0,) * a.ndim)
    return pl.pallas_call(
        functools.partial(_inproj_kernel, q_lora=q_lora, kv_lora=kv_lora),
        grid=(t // tm, na // tn),
        in_specs=[row(d), mod_spec, mod_spec, full(g1),
                  pl.BlockSpec((d, tn), lambda i, j: (0, j)), full(wb), full(gq), full(gkv), tab_spec, tab_spec],
        out_specs=[pl.BlockSpec((tm, tn), lambda i, j: (i, j)), row(q_lora), row(kv_lora), row(QK_ROPE)],
        out_shape=[jax.ShapeDtypeStruct((t, na), BF16), jax.ShapeDtypeStruct((t, q_lora), BF16),
                   jax.ShapeDtypeStruct((t, kv_lora), F32), jax.ShapeDtypeStruct((t, QK_ROPE), F32)],
        scratch_shapes=[pltpu.VMEM((tm, d), BF16)],
        compiler_params=_cparams(("arbitrary", "arbitrary")),
        name="in_proj",
    )(x, sc, sh, g1, wa, wb, gq, gkv, cos64, sin64)


def _rg_coeffs(xc, wrg_ref, brg_ref, sp):
    a_parts, b_parts = [], []
    for n in range(xc.shape[1] // RNN_BLOCK_W):
        sl = slice(n * RNN_BLOCK_W, (n + 1) * RNN_BLOCK_W)
        xb = xc[:, sl]
        z = _dot(xb.astype(BF16), wrg_ref[n]) + brg_ref[n]
        g = 0.5 + 0.5 * jnp.tanh(0.5 * z)
        r, i = g[:, :RNN_BLOCK_W], g[:, RNN_BLOCK_W:]
        a = jnp.exp(-RG_C * r * sp[:, sl])
        a_parts.append(a)
        b_parts.append(jnp.sqrt(1.0 - a * a) * i * xb)
    return jnp.concatenate(a_parts, axis=1), jnp.concatenate(b_parts, axis=1)


def _softplus_neg(lam):
    return jnp.maximum(-lam, 0.0) + jnp.log1p(jnp.exp(-jnp.abs(lam)))


def _rglru_seq_kernel(x_ref, g_ref, wc_ref, bc_ref, wrg_ref, brg_ref, lam_ref,
                      y_ref, conv_ref, h_ref, tail_ref, hc_ref, a_scr, b_scr):
    t = pl.program_id(2)
    tc, dc = x_ref.shape

    @pl.when(t == 0)
    def _():
        tail_ref[...] = jnp.zeros_like(tail_ref)
        hc_ref[...] = jnp.zeros_like(hc_ref)

    x = x_ref[...].astype(F32)
    xw = jnp.concatenate([tail_ref[...], x], axis=0)
    xc = bc_ref[...] + wc_ref[CONV_W - 1:CONV_W, :] * x
    for k in range(CONV_W - 1):
        d = CONV_W - 1 - k
        xc = xc + wc_ref[k:k + 1, :] * xw[8 - d:8 - d + tc, :]
    tail_ref[...] = x[tc - 8:, :]

    a, b = _rg_coeffs(xc, wrg_ref, brg_ref, _softplus_neg(lam_ref[...]))
    a_scr[...] = a
    b_scr[...] = b

    row = lax.broadcasted_iota(jnp.int32, (8, dc), 0)

    def body(k, hc):
        r0 = pl.multiple_of(k * 8, 8)
        av = a_scr[pl.ds(r0, 8), :]
        bv = b_scr[pl.ds(r0, 8), :]
        for s in (1, 2, 4):
            a_sh = jnp.where(row >= s, pltpu.roll(av, s, 0), 1.0)
            b_sh = jnp.where(row >= s, pltpu.roll(bv, s, 0), 0.0)
            bv = av * b_sh + bv
            av = av * a_sh
        h = av * hc + bv
        b_scr[pl.ds(r0, 8), :] = h
        return jnp.broadcast_to(h[7:8, :], (8, dc))

    hc = lax.fori_loop(0, tc // 8, body, hc_ref[...])
    hc_ref[...] = hc
    y_ref[...] = (b_scr[...] * _gelu_tanh(g_ref[...].astype(F32))).astype(BF16)

    @pl.when(t == pl.num_programs(2) - 1)
    def _():
        conv_ref[...] = x[tc - (CONV_W - 1):, :]
        h_ref[...] = hc[0:1, :]


def rglru_seq(pa, nb, seq, w_conv, b_conv, wrg, brg, lam, *, tc=512, dc=512):
    d = w_conv.shape[1]
    nc, nt = d // dc, seq // tc
    bpc = dc // RNN_BLOCK_W
    return pl.pallas_call(
        _rglru_seq_kernel,
        grid=(nb, nc, nt),
        in_specs=[pl.BlockSpec((tc, dc), lambda b, c, t: (b * nt + t, c)),
                  pl.BlockSpec((tc, dc), lambda b, c, t: (b * nt + t, nc + c)),
                  pl.BlockSpec((CONV_W, dc), lambda b, c, t: (0, c)),
                  pl.BlockSpec((1, dc), lambda b, c, t: (0, c)),
                  pl.BlockSpec((bpc, RNN_BLOCK_W, 2 * RNN_BLOCK_W), lambda b, c, t: (c, 0, 0)),
                  pl.BlockSpec((bpc, 1, 2 * RNN_BLOCK_W), lambda b, c, t: (c, 0, 0)),
                  pl.BlockSpec((1, dc), lambda b, c, t: (0, c))],
        out_specs=[pl.BlockSpec((tc, dc), lambda b, c, t: (b * nt + t, c)),
                   pl.BlockSpec((None, CONV_W - 1, dc), lambda b, c, t: (b, 0, c)),
                   pl.BlockSpec((None, 1, dc), lambda b, c, t: (b, 0, c))],
        out_shape=[jax.ShapeDtypeStruct((nb * seq, d), BF16),
                   jax.ShapeDtypeStruct((nb, CONV_W - 1, d), F32),
                   jax.ShapeDtypeStruct((nb, 1, d), F32)],
        scratch_shapes=[pltpu.VMEM((8, dc), F32), pltpu.VMEM((8, dc), F32),
                        pltpu.VMEM((tc, dc), F32), pltpu.VMEM((tc, dc), F32)],
        compiler_params=_cparams(("arbitrary", "arbitrary", "arbitrary")),
        name="rglru_seq",
    )(pa, pa, w_conv, b_conv, wrg, brg, lam)


def _rglru_step_kernel(x_ref, g_ref, prev_ref, h0_ref, wc_ref, bc_ref, wrg_ref, brg_ref, lam_ref,
                       y_ref, conv_ref, h_ref):
    nt = x_ref.shape[0]
    full = [prev_ref[k] for k in range(CONV_W - 1)] + [x_ref[k].astype(F32) for k in range(nt)]
    sp = _softplus_neg(lam_ref[...])
    h = h0_ref[...]
    for t in range(nt):
        xc = bc_ref[...] + wc_ref[0:1, :] * full[t]
        for k in range(1, CONV_W):
            xc = xc + wc_ref[k:k + 1, :] * full[t + k]
        a, b = _rg_coeffs(xc, wrg_ref, brg_ref, sp)
        h = a * h + b
        y_ref[t] = (h * _gelu_tanh(g_ref[t].astype(F32))).astype(BF16)
    for k in range(CONV_W - 1):
        conv_ref[k] = full[nt + k]
    h_ref[...] = h


def rglru_step(x4, g4, prev, h0, w_conv, b_conv, wrg, brg, lam, *, dc=512):
    nt, bs, d = x4.shape
    bpc = dc // RNN_BLOCK_W
    return pl.pallas_call(
        _rglru_step_kernel,
        grid=(d // dc,),
        in_specs=[pl.BlockSpec((nt, bs, dc), lambda c: (0, 0, c)),
                  pl.BlockSpec((nt, bs, dc), lambda c: (0, 0, c)),
                  pl.BlockSpec((CONV_W - 1, bs, dc), lambda c: (0, 0, c)),
                  pl.BlockSpec((bs, dc), lambda c: (0, c)),
                  pl.BlockSpec((CONV_W, dc), lambda c: (0, c)),
                  pl.BlockSpec((1, dc), lambda c: (0, c)),
                  pl.BlockSpec((bpc, RNN_BLOCK_W, 2 * RNN_BLOCK_W), lambda c: (c, 0, 0)),
                  pl.BlockSpec((bpc, 1, 2 * RNN_BLOCK_W), lambda c: (c, 0, 0)),
                  pl.BlockSpec((1, dc), lambda c: (0, c))],
        out_specs=[pl.BlockSpec((nt, bs, dc), lambda c: (0, 0, c)),
                   pl.BlockSpec((CONV_W - 1, bs, dc), lambda c: (0, 0, c)),
                   pl.BlockSpec((bs, dc), lambda c: (0, c))],
        out_shape=[jax.ShapeDtypeStruct((nt, bs, d), BF16),
                   jax.ShapeDtypeStruct((CONV_W - 1, bs, d), F32),
                   jax.ShapeDtypeStruct((bs, d), F32)],
        compiler_params=_cparams(("arbitrary",)),
        name="rglru_step",
    )(x4, g4, prev, h0, w_conv, b_conv, wrg, brg, lam)


def _rope_heads(qr, qrot, cos128, sin128, scale, out_ref, lane0):
    for hp in range(qr.shape[1] // 128):
        sl = slice(hp * 128, (hp + 1) * 128)
        v = ((qr[:, sl] * cos128 + qrot[:, sl] * sin128) * scale).astype(BF16)
        out_ref[2 * hp, :, lane0:lane0 + QK_ROPE] = v[:, :QK_ROPE]
        out_ref[2 * hp + 1, :, lane0:lane0 + QK_ROPE] = v[:, QK_ROPE:]


def _qkv_prompt_kernel(cq_ref, ckv_ref, wqn_ref, wqr_ref, wqrot_ref, wuk_ref, wuvt_ref, cos_ref, sin_ref,
                       qn_ref, qr_ref, kn_ref, vt_ref, *, scale):
    cq = cq_ref[...]
    qn_ref[...] = (_dot(cq, wqn_ref[...]) * scale).astype(BF16)
    _rope_heads(_dot(cq, wqr_ref[...]), _dot(cq, wqrot_ref[...]), cos_ref[...], sin_ref[...], scale, qr_ref, 0)
    cb = ckv_ref[...].astype(BF16)
    kn_ref[...] = _dot(cb, wuk_ref[...]).astype(BF16)
    vt_ref[...] = _dot_nt(wuvt_ref[...], cb).astype(BF16)


def qkv_prompt(cq, ckv, wqn, wqr, wqrot, wuk, wuvt, cos128, sin128, *, seq, scale, tm=512):
    t, ql = cq.shape
    kvl = ckv.shape[1]
    hn, hr, hv = wqn.shape[1], wqr.shape[1], wuvt.shape[0]
    nh = hr // QK_ROPE
    tps = seq // tm
    full = lambda a: pl.BlockSpec(a.shape, lambda i: (0,) * a.ndim)
    tab = pl.BlockSpec((tm, 128), lambda i: (i % tps, 0))
    return pl.pallas_call(
        functools.partial(_qkv_prompt_kernel, scale=scale),
        grid=(t // tm,),
        in_specs=[pl.BlockSpec((tm, ql), lambda i: (i, 0)), pl.BlockSpec((tm, kvl), lambda i: (i, 0)),
                  full(wqn), full(wqr), full(wqrot), full(wuk), full(wuvt), tab, tab],
        out_specs=[pl.BlockSpec((tm, hn), lambda i: (i, 0)),
                   pl.BlockSpec((nh, tm, QK_ROPE), lambda i: (0, i, 0)),
                   pl.BlockSpec((tm, hn), lambda i: (i, 0)),
                   pl.BlockSpec((hv, tm), lambda i: (0, i))],
        out_shape=[jax.ShapeDtypeStruct((t, hn), BF16), jax.ShapeDtypeStruct((nh, t, QK_ROPE), BF16),
                   jax.ShapeDtypeStruct((t, hn), BF16), jax.ShapeDtypeStruct((hv, t), BF16)],
        compiler_params=_cparams(("arbitrary",)),
        name="qkv_prompt",
    )(cq, ckv, wqn, wqr, wqrot, wuk, wuvt, cos128, sin128)


def _q_sample_kernel(cq_ref, wqn_ref, wqr_ref, wqrot_ref, wukt_ref, cos_ref, sin_ref, q_ref, *, scale):
    cq = cq_ref[...]
    qn = _dot(cq, wqn_ref[...])
    kvl = wukt_ref.shape[2]
    for h in range(wukt_ref.shape[0]):
        ql = _dot(qn[:, h * QK_NOPE:(h + 1) * QK_NOPE].astype(BF16), wukt_ref[h]) * scale
        q_ref[h, :, :kvl] = ql.astype(BF16)
    _rope_heads(_dot(cq, wqr_ref[...]), _dot(cq, wqrot_ref[...]), cos_ref[...], sin_ref[...], scale, q_ref, kvl)


def q_sample(cq, wqn, wqr, wqrot, wukt, cos128, sin128, *, scale):
    t = cq.shape[0]
    nh, _, kvl = wukt.shape
    full = lambda a: pl.BlockSpec(a.shape, lambda i: (0,) * a.ndim)
    return pl.pallas_call(
        functools.partial(_q_sample_kernel, scale=scale),
        grid=(1,),
        in_specs=[full(cq), full(wqn), full(wqr), full(wqrot), full(wukt), full(cos128), full(sin128)],
        out_specs=pl.BlockSpec((nh, t, kvl + QK_ROPE), lambda i: (0, 0, 0)),
        out_shape=jax.ShapeDtypeStruct((nh, t, kvl + QK_ROPE), BF16),
        compiler_params=_cparams(("arbitrary",)),
        name="q_sample",
    )(cq, wqn, wqr, wqrot, wukt, cos128, sin128)


def _softmax_step(s, v, m_ref, l_ref, acc_ref):
    m_prev = m_ref[...]
    m_new = jnp.maximum(m_prev, jnp.max(s, axis=-1, keepdims=True))
    alpha = jnp.exp(m_prev - m_new)
    p = jnp.exp(s - m_new)
    l_ref[...] = alpha * l_ref[...] + jnp.sum(p, axis=-1, keepdims=True)
    acc_ref[...] = alpha * acc_ref[...] + _dot(p.astype(BF16), v)
    m_ref[...] = m_new


def _attn_prompt_kernel(qi_ref, kj_ref, qn_ref, qr_ref, kn_ref, kr_ref, vt_ref, o_ref, m_ref, l_ref, acc_ref, *, hb, cw):
    p = pl.program_id(2)
    i, j = qi_ref[p], kj_ref[p]
    hs = [slice(h * QK_NOPE, (h + 1) * QK_NOPE) for h in range(hb)]

    @pl.when(j == 0)
    def _():
        m_ref[...] = jnp.full_like(m_ref, NEG_BIG)
        l_ref[...] = jnp.zeros_like(l_ref)
        acc_ref[...] = jnp.zeros_like(acc_ref)

    tq = qn_ref.shape[0]

    def block(diagonal):
        kr = kr_ref[...].astype(BF16)
        kh = [jnp.concatenate([kn_ref[:, hs[h]], kr], axis=1) for h in range(hb)]
        chains = [(h, c) for h in range(hb) for c in range(tq // cw)]
        qs = lambda c: slice(c * cw, (c + 1) * cw)
        nk = lambda c: (c + 1) * cw if diagonal else tq
        m_prev = [m_ref[h, :, qs(c)] for h, c in chains]
        l_prev = [l_ref[h, :, qs(c)] for h, c in chains]
        a_prev = [acc_ref[h, :, qs(c)] for h, c in chains]
        st = [_dot_nt(kh[h][:nk(c)], jnp.concatenate([qn_ref[qs(c), hs[h]], qr_ref[h, qs(c), :]], axis=1))
              for h, c in chains]
        if diagonal:
            visible = lambda s, c: (lax.broadcasted_iota(jnp.int32, s.shape, 0)
                                    <= lax.broadcasted_iota(jnp.int32, s.shape, 1) + c * cw)
            st = [jnp.where(visible(s, c), s, NEG_BIG) for s, (h, c) in zip(st, chains)]
        m_new = [jnp.maximum(mp, jnp.max(s, axis=0, keepdims=True)) for mp, s in zip(m_prev, st)]
        alpha = [jnp.exp2(mp - mn) for mp, mn in zip(m_prev, m_new)]
        pt = [jnp.exp2(s - mn) for s, mn in zip(st, m_new)]
        pv = [_dot(vt_ref[hs[h], :nk(c)], p.astype(BF16)) for p, (h, c) in zip(pt, chains)]
        for n, (h, c) in enumerate(chains):
            l_ref[h, :, qs(c)] = alpha[n] * l_prev[n] + jnp.sum(pt[n], axis=0, keepdims=True)
            acc_ref[h, :, qs(c)] = alpha[n] * a_prev[n] + pv[n]
            m_ref[h, :, qs(c)] = m_new[n]

    @pl.when(j < i)
    def _():
        block(False)

    @pl.when(j == i)
    def _():
        block(True)
        for h in range(hb):
            o_ref[:, hs[h]] = (acc_ref[h] / l_ref[h]).T.astype(BF16)


def attn_prompt(qn, qr, kn, kr, vt, *, nb, seq, tq=512, hb=2, cw=256):
    t, hn = qn.shape
    nh = hn // QK_NOPE
    nq = seq // tq
    wb = hb * QK_NOPE
    pairs = [(i, j) for i in range(nq) for j in range(i + 1)]
    qi = jnp.asarray([p[0] for p in pairs], jnp.int32)
    kj = jnp.asarray([p[1] for p in pairs], jnp.int32)
    grid_spec = pltpu.PrefetchScalarGridSpec(
        num_scalar_prefetch=2,
        grid=(nb, nh // hb, len(pairs)),
        in_specs=[pl.BlockSpec((tq, wb), lambda b, h, p, qi, kj: (b * nq + qi[p], h)),
                  pl.BlockSpec((hb, tq, QK_ROPE), lambda b, h, p, qi, kj: (h, b * nq + qi[p], 0)),
                  pl.BlockSpec((tq, wb), lambda b, h, p, qi, kj: (b * nq + kj[p], h)),
                  pl.BlockSpec((tq, QK_ROPE), lambda b, h, p, qi, kj: (b * nq + kj[p], 0)),
                  pl.BlockSpec((wb, tq), lambda b, h, p, qi, kj: (h, b * nq + kj[p]))],
        out_specs=pl.BlockSpec((tq, wb), lambda b, h, p, qi, kj: (b * nq + qi[p], h)),
        scratch_shapes=[pltpu.VMEM((hb, 1, tq), F32), pltpu.VMEM((hb, 1, tq), F32),
                        pltpu.VMEM((hb, QK_NOPE, tq), F32)],
    )
    return pl.pallas_call(
        functools.partial(_attn_prompt_kernel, hb=hb, cw=min(cw, tq)),
        grid_spec=grid_spec,
        out_shape=jax.ShapeDtypeStruct((t, hn), BF16),
        compiler_params=_cparams(("arbitrary", "arbitrary", "arbitrary")),
        name="attn_prompt",
    )(qi, kj, qn, qr, kn, kr, vt)


NEW_ROWS_PAD = 16


def _attn_sample_kernel(pt_ref, q_ref, ck_hbm, kr_hbm, cn_ref, kn_ref, o_ref,
                        ckraw, krraw, ckbuf, krbuf, m_ref, l_ref, acc_ref, sem, *,
                        layer, gs, pg, page, kvl, heads, nsteps, ns):
    b, s_idx = pl.program_id(0), pl.program_id(1)
    ring = ckraw.shape[0]
    step = b * ns + s_idx
    slot = step % ring
    npages = ns * pg

    def start_pages(n, sl):
        bb, ss = n // ns, n % ns
        for g in range(gs):
            for p in range(pg):
                j = g * pg + p
                pid = pt_ref[(bb * gs + g) * npages + ss * pg + p]
                pltpu.make_async_copy(ck_hbm.at[layer, pid], ckraw.at[sl, j], sem.at[sl]).start(priority=j % 2)
                pltpu.make_async_copy(kr_hbm.at[layer, pid], krraw.at[sl, j], sem.at[sl]).start(priority=j % 2)

    @pl.when(step == 0)
    def _():
        for n in range(min(ring - 1, nsteps)):
            start_pages(n, n)

    @pl.when(step + ring - 1 < nsteps)
    def _():
        start_pages(step + ring - 1, (step + ring - 1) % ring)

    @pl.when(s_idx == 0)
    def _():
        m_ref[...] = jnp.full_like(m_ref, NEG_BIG)
        l_ref[...] = jnp.zeros_like(l_ref)
        acc_ref[...] = jnp.zeros_like(acc_ref)

    for j in range(gs * pg):
        pltpu.make_async_copy(ck_hbm.at[layer, 0], ckraw.at[slot, j], sem.at[slot]).wait()
        pltpu.make_async_copy(kr_hbm.at[layer, 0], krraw.at[slot, j], sem.at[slot]).wait()

    for g in range(gs):
        for p in range(pg):
            ckbuf[g, p * page:(p + 1) * page, :] = ckraw[slot, g * pg + p].astype(BF16)
            krbuf[g, :, p * page:(p + 1) * page] = krraw[slot, g * pg + p].astype(BF16)
    s = [_dot_nt(q_ref[g][:, :kvl], ckbuf[g]) + _dot(q_ref[g][:, kvl:], krbuf[g]) for g in range(gs)]
    m_prev = [m_ref[g] for g in range(gs)]
    m_new = [jnp.maximum(m_prev[g], jnp.max(s[g], axis=-1, keepdims=True)) for g in range(gs)]
    alpha = [jnp.exp(m_prev[g] - m_new[g]) for g in range(gs)]
    pr = [jnp.exp(s[g] - m_new[g]) for g in range(gs)]
    pv = [_dot(pr[g].astype(BF16), ckbuf[g]) for g in range(gs)]
    for g in range(gs):
        l_ref[g] = alpha[g] * l_ref[g] + jnp.sum(pr[g], axis=-1, keepdims=True)
        acc_ref[g] = alpha[g] * acc_ref[g] + pv[g]
        m_ref[g] = m_new[g]

    @pl.when(s_idx == pl.num_programs(1) - 1)
    def _():
        nt = cn_ref.shape[1]
        zpad = lambda a: jnp.concatenate([a, jnp.zeros((NEW_ROWS_PAD - nt, a.shape[1]), F32)], axis=0).astype(BF16)
        for g in range(gs):
            q = q_ref[g]
            cn = zpad(cn_ref[g])
            s = _dot_nt(q[:, :kvl], cn) + _dot_nt(q[:, kvl:], zpad(kn_ref[g]))
            row = lax.broadcasted_iota(jnp.int32, s.shape, 0)
            col = lax.broadcasted_iota(jnp.int32, s.shape, 1)
            s = jnp.where(col * heads <= row, s, NEG_BIG)
            _softmax_step(s, cn, m_ref.at[g], l_ref.at[g], acc_ref.at[g])
            o_ref[g] = (acc_ref[g] / l_ref[g]).astype(BF16)


def attn_sample(q, cache_ckv, cache_krope_t, ckv_new, kr_new, page_table, *, layer, heads, gs=2, pg=8, ring=3):
    bs, nq, qd = q.shape
    npages = page_table.shape[1]
    page, kvl = cache_ckv.shape[2], cache_ckv.shape[3]
    nt = ckv_new.shape[1]
    assert nt <= NEW_ROWS_PAD and npages % pg == 0 and bs % gs == 0

    grid_spec = pltpu.PrefetchScalarGridSpec(
        num_scalar_prefetch=1,
        grid=(bs // gs, npages // pg),
        in_specs=[pl.BlockSpec((gs, nq, qd), lambda b, s, pt: (b, 0, 0)),
                  pl.BlockSpec(memory_space=pl.ANY), pl.BlockSpec(memory_space=pl.ANY),
                  pl.BlockSpec((gs, nt, kvl), lambda b, s, pt: (b, 0, 0)),
                  pl.BlockSpec((gs, nt, QK_ROPE), lambda b, s, pt: (b, 0, 0))],
        out_specs=pl.BlockSpec((gs, nq, kvl), lambda b, s, pt: (b, 0, 0)),
        scratch_shapes=[pltpu.VMEM((ring, gs * pg, page, kvl), F32), pltpu.VMEM((ring, gs * pg, QK_ROPE, page), F32),
                        pltpu.VMEM((gs, pg * page, kvl), BF16), pltpu.VMEM((gs, QK_ROPE, pg * page), BF16),
                        pltpu.VMEM((gs, nq, 1), F32), pltpu.VMEM((gs, nq, 1), F32), pltpu.VMEM((gs, nq, kvl), F32),
                        pltpu.SemaphoreType.DMA((ring,))],
    )
    ns = npages // pg
    return pl.pallas_call(
        functools.partial(_attn_sample_kernel, layer=layer, gs=gs, pg=pg, page=page, kvl=kvl, heads=heads,
                          nsteps=(bs // gs) * ns, ns=ns),
        grid_spec=grid_spec,
        out_shape=jax.ShapeDtypeStruct((bs, nq, kvl), BF16),
        compiler_params=_cparams(("arbitrary", "arbitrary")),
        name="attn_sample",
    )(page_table.reshape(-1), q, cache_ckv, cache_krope_t, ckv_new, kr_new)


def _uv_kernel(o_ref, w_ref, y_ref):
    y_ref[...] = _dot(o_ref[...], w_ref[...]).astype(BF16)


def uv_sample(o, wuv_h):
    t = o.shape[0]
    nh, kvl, vd = wuv_h.shape
    return pl.pallas_call(
        _uv_kernel,
        grid=(nh,),
        in_specs=[pl.BlockSpec((t, kvl), lambda h: (0, h)), pl.BlockSpec((None, kvl, vd), lambda h: (h, 0, 0))],
        out_specs=pl.BlockSpec((t, vd), lambda h: (0, h)),
        out_shape=jax.ShapeDtypeStruct((t, nh * vd), BF16),
        compiler_params=_cparams(("arbitrary",)),
        name="uv_sample",
    )(o, wuv_h)


def _merge_route_kernel(*refs, n_real, has_prev):
    if has_prev:
        refs = refs[:14] + refs[15:]
    hn_ref = refs[15]
    i = pl.program_id(0)

    @pl.when(i < n_real)
    def _():
        _merge_route_body(*refs)

    @pl.when(i >= n_real)
    def _():
        hn_ref[...] = jnp.zeros_like(hn_ref)


def _merge_route_body(x_ref, ya_ref, yb_ref, ga_ref, gb_ref, g1_ref, sc_ref, sh_ref, wo_ref, gn_ref,
                      wrh_ref, wrl_ref, br_ref, cin_ref, x1_ref, hn_ref, rt_ref, cnt_ref, rtt_ref):
    @pl.when(pl.program_id(0) == 0)
    def _():
        cnt_ref[...] = cin_ref[...]

    tm = x_ref.shape[0]
    parts = [slice(k * (tm // ROUTE_PARTS), (k + 1) * (tm // ROUTE_PARTS)) for k in range(ROUTE_PARTS)]
    mod = lambda ref, r: ref[r] if ref.shape[0] == tm else ref[...]
    sig = lambda v: 0.5 + 0.5 * jnp.tanh(0.5 * v.astype(F32))
    merged = [(sig(ga_ref[r]) * ya_ref[r].astype(F32) + sig(gb_ref[r]) * yb_ref[r].astype(F32)).astype(BF16)
              for r in parts]
    mix = [_dot(m, wo_ref[...]) for m in merged]
    x1 = [x_ref[r] + mod(g1_ref, r) * mx for r, mx in zip(parts, mix)]
    hn = [_rms(v, gn_ref[...]) * (1.0 + mod(sc_ref, r)) + mod(sh_ref, r) for r, v in zip(parts, x1)]
    for r, v, h in zip(parts, x1, hn):
        x1_ref[r] = v
        hn_ref[r] = h

    hh = [h.astype(BF16) for h in hn]
    hl = [(h - hb.astype(F32)).astype(BF16) for h, hb in zip(hn, hh)]
    logits = [_dot(a, wrh_ref[...]) + (_dot(a, wrl_ref[...]) + _dot(b, wrh_ref[...])) + br_ref[...]
              for a, b in zip(hh, hl)]
    pm = tm // ROUTE_PARTS
    lane = lax.broadcasted_iota(jnp.int32, (pm, ROUTE_LANES), 1)
    tri = jnp.where(lax.broadcasted_iota(jnp.int32, (pm, pm), 0) > lax.broadcasted_iota(jnp.int32, (pm, pm), 1),
                    1.0, 0.0).astype(BF16)

    def first_argmax(v):
        mx = jnp.max(v, axis=-1, keepdims=True)
        return mx, jnp.min(jnp.where(v == mx, lane, ROUTE_LANES), axis=-1, keepdims=True)

    cnt = cnt_ref[...]
    for r, lg in zip(parts, logits):
        gl = jnp.where(lane < N_GROUPS, lg, NEG_BIG)
        gmax, gidx = first_argmax(gl)
        p_top = 1.0 / jnp.sum(jnp.exp(gl - gmax), axis=-1, keepdims=True)
        lo = N_GROUPS + gidx * EXPERTS_PER_GROUP
        el = jnp.where((lane >= lo) & (lane < lo + EXPERTS_PER_GROUP), lg, NEG_BIG)
        m1, l1 = first_argmax(el)
        m2, l2 = first_argmax(jnp.where(lane == l1, NEG_BIG, el))
        e21 = jnp.exp(m2 - m1)
        w1 = p_top / (1.0 + e21)
        w2 = p_top * e21 / (1.0 + e21)
        e1, e2 = l1 - N_GROUPS, l2 - N_GROUPS

        oh1, oh2 = lane == e1, lane == e2
        ohs = jnp.where(oh1 | oh2, 1.0, 0.0)
        before = _dot(tri, ohs.astype(BF16)) + cnt
        r1 = jnp.sum(jnp.where(oh1, before, 0.0), axis=-1, keepdims=True)
        r2 = jnp.sum(jnp.where(oh2, before, 0.0), axis=-1, keepdims=True)
        cnt = cnt + jnp.sum(ohs, axis=0, keepdims=True)

        rt = jnp.where(lane == 0, e1.astype(F32), 0.0)
        rt = jnp.where(lane == 1, e2.astype(F32), rt)
        rt = jnp.where(lane == 2, w1, rt)
        rt = jnp.where(lane == 3, w2, rt)
        rt = jnp.where(lane == 4, r1, rt)
        rt = jnp.where(lane == 5, r2, rt)
        rt_ref[r] = rt
        rtt_ref[:, r] = rt.T[:8, :]
    cnt_ref[...] = cnt


def merge_route(x, ya, yb, pa, g1, sc2, sh2, wo, gn2, wrh, wrl, br, cnt_in, *, per_seq, tm, h_rows, h_row0, h_prev=None):
    t, d = x.shape
    n_real = t // tm
    assert h_row0 % tm == 0 and (h_rows - t) % tm == 0
    n_steps = n_real if h_prev is not None else h_rows // tm
    ri = lambda i: jnp.minimum(i, n_real - 1)
    if per_seq:
        tps = per_seq // tm
        mod_spec = pl.BlockSpec((None, 1, d), lambda i: (ri(i) // tps, 0, 0))
    else:
        mod_spec = pl.BlockSpec((tm, d), lambda i: (ri(i), 0))
    row = pl.BlockSpec((tm, d), lambda i: (ri(i), 0))
    full = lambda a: pl.BlockSpec(a.shape, lambda i: (0,) * a.ndim)
    args = [x, ya, yb, pa, pa, g1, sc2, sh2, wo, gn2, wrh, wrl, br, cnt_in]
    in_specs = [row, row, row, pl.BlockSpec((tm, d), lambda i: (ri(i), 2)), pl.BlockSpec((tm, d), lambda i: (ri(i), 3)),
                mod_spec, mod_spec, mod_spec, full(wo), full(gn2), full(wrh), full(wrl), full(br), full(cnt_in)]
    aliases = {}
    if h_prev is not None:
        args.append(h_prev)
        in_specs.append(pl.BlockSpec(memory_space=pl.ANY))
        aliases = {len(args) - 1: 1}
    return pl.pallas_call(
        functools.partial(_merge_route_kernel, n_real=n_real, has_prev=h_prev is not None),
        grid=(n_steps,),
        in_specs=in_specs,
        out_specs=[row, pl.BlockSpec((tm, d), lambda i: (i + h_row0 // tm, 0)),
                   pl.BlockSpec((tm, ROUTE_LANES), lambda i: (ri(i), 0)),
                   pl.BlockSpec((1, ROUTE_LANES), lambda i: (0, 0)),
                   pl.BlockSpec((8, tm), lambda i: (0, ri(i)))],
        out_shape=[jax.ShapeDtypeStruct((t, d), F32), jax.ShapeDtypeStruct((h_rows, d), F32),
                   jax.ShapeDtypeStruct((t, ROUTE_LANES), F32), jax.ShapeDtypeStruct((1, ROUTE_LANES), F32),
                   jax.ShapeDtypeStruct((8, t), F32)],
        input_output_aliases=aliases,
        compiler_params=_cparams(("arbitrary",)),
        name="merge_route",
    )(*args)


GATHER_UNROLL = 8


def _row_copy(src, dst, sem):
    return pltpu.make_async_copy(src, dst, sem)


def _dispatch_kernel(pos_ref, zf_ref, h_ref, xs_ref, zbuf, sem, zsem, *, tile):
    i = pl.program_id(0)
    tm = h_ref.shape[0]
    nrows = pl.num_programs(0) * tm

    @pl.when(i == 0)
    def _():
        zbuf[...] = jnp.zeros_like(zbuf)

        def zero_tile(wait):
            def body(t, c):
                @pl.when(zf_ref[t] == 1)
                def _():
                    cp = _row_copy(zbuf, xs_ref.at[pl.ds(pl.multiple_of(t * tile, tile), tile), :], zsem)
                    cp.wait() if wait else cp.start()
                return c
            return body

        lax.fori_loop(0, zf_ref.shape[0], zero_tile(False), 0)
        lax.fori_loop(0, zf_ref.shape[0], zero_tile(True), 0)

    def issue(rb, c):
        for u in range(GATHER_UNROLL):
            r = rb * GATHER_UNROLL + u
            for k in range(2):
                p = pos_ref[k * nrows + i * tm + r]
                _row_copy(h_ref.at[pl.ds(r, 1), :], xs_ref.at[pl.ds(p, 1), :], sem).start(priority=k)
        return c

    lax.fori_loop(0, tm // GATHER_UNROLL, issue, 0)
    for k in range(2):
        _row_copy(h_ref, xs_ref.at[pl.ds(0, tm), :], sem).wait()


def dispatch(pos, zero_flags, h_all, *, tm, tile):
    t, d = h_all.shape
    grid_spec = pltpu.PrefetchScalarGridSpec(
        num_scalar_prefetch=2,
        grid=(t // tm,),
        in_specs=[pl.BlockSpec((tm, d), lambda i, pos, zf: (i, 0))],
        out_specs=pl.BlockSpec(memory_space=pl.ANY),
        scratch_shapes=[pltpu.VMEM((tile, d), h_all.dtype), pltpu.SemaphoreType.DMA(()), pltpu.SemaphoreType.DMA(())],
    )
    return pl.pallas_call(
        functools.partial(_dispatch_kernel, tile=tile),
        grid_spec=grid_spec,
        out_shape=jax.ShapeDtypeStruct((zero_flags.shape[0] * tile, d), h_all.dtype),
        compiler_params=_cparams(("arbitrary",)),
        name="dispatch",
    )(pos, zero_flags, h_all)


def _gmm_kernel(te_ref, first_ref, nt_ref, x_ref, wg_ref, wu_ref, wd_ref, y_ref, wgb, wub, wdb):
    i = pl.program_id(0)

    @pl.when(i < nt_ref[0])
    def _():
        @pl.when(first_ref[i] == 1)
        def _():
            wgb[...] = wg_ref[...].astype(BF16)
            wub[...] = wu_ref[...].astype(BF16)
            wdb[...] = wd_ref[...].astype(BF16)

        x = x_ref[...].astype(BF16)
        g = _dot(x, wgb[...])
        u = _dot(x, wub[...])
        y_ref[...] = _dot((g * jax.nn.sigmoid(g) * u).astype(BF16), wdb[...])

    @pl.when(i >= nt_ref[0])
    def _():
        y_ref[...] = jnp.zeros_like(y_ref)


def expert_ffn(tile_e, tile_first, n_tiles, xs, w_gate, w_up, w_down, *, layer, tm):
    s, d = xs.shape
    de = w_gate.shape[3]
    last = lambda i, nt: jnp.minimum(i, nt[0] - 1)
    grid_spec = pltpu.PrefetchScalarGridSpec(
        num_scalar_prefetch=3,
        grid=(s // tm,),
        in_specs=[pl.BlockSpec((tm, d), lambda i, te, tf, nt: (last(i, nt), 0)),
                  pl.BlockSpec((None, None, d, de), lambda i, te, tf, nt: (layer, te[i], 0, 0)),
                  pl.BlockSpec((None, None, d, de), lambda i, te, tf, nt: (layer, te[i], 0, 0)),
                  pl.BlockSpec((None, None, de, d), lambda i, te, tf, nt: (layer, te[i], 0, 0))],
        out_specs=pl.BlockSpec((tm, d), lambda i, te, tf, nt: (i, 0)),
        scratch_shapes=[pltpu.VMEM((d, de), BF16), pltpu.VMEM((d, de), BF16), pltpu.VMEM((de, d), BF16)],
    )
    return pl.pallas_call(
        _gmm_kernel,
        grid_spec=grid_spec,
        out_shape=jax.ShapeDtypeStruct((s, d), F32),
        compiler_params=_cparams(("arbitrary",)),
        name="expert_ffn",
    )(tile_e, tile_first, n_tiles, xs, w_gate, w_up, w_down)


def _combine_kernel(pos_ref, x1_ref, g2_ref, rt_ref, gf_ref, ys_ref, o_ref, gbuf, sem):
    i = pl.program_id(0)
    tm = x1_ref.shape[0]
    nrows = pl.num_programs(0) * tm

    def issue(rb, c):
        for u in range(GATHER_UNROLL):
            r = rb * GATHER_UNROLL + u
            for k in range(2):
                p = pos_ref[k * nrows + i * tm + r]
                _row_copy(ys_ref.at[pl.ds(p, 1), :], gbuf.at[k, pl.ds(r, 1), :], sem).start(priority=k)
        return c

    lax.fori_loop(0, tm // GATHER_UNROLL, issue, 0)
    for k in range(2):
        _row_copy(ys_ref.at[pl.ds(0, tm), :], gbuf.at[k], sem).wait()
    rt = rt_ref[...]
    moe = rt[:, 2:3] * gbuf[0] + rt[:, 3:4] * gbuf[1]
    o_ref[...] = _rms(x1_ref[...] + g2_ref[...] * moe, gf_ref[...])


def combine_final(pos, x1, g2, rt, gf, ys, *, per_seq, tm=256):
    t, d = x1.shape
    if per_seq:
        tps = per_seq // tm
        mod_spec = pl.BlockSpec((None, 1, d), lambda i, pos: (i // tps, 0, 0))
    else:
        mod_spec = pl.BlockSpec((tm, d), lambda i, pos: (i, 0))
    grid_spec = pltpu.PrefetchScalarGridSpec(
        num_scalar_prefetch=1,
        grid=(t // tm,),
        in_specs=[pl.BlockSpec((tm, d), lambda i, pos: (i, 0)), mod_spec,
                  pl.BlockSpec((tm, ROUTE_LANES), lambda i, pos: (i, 0)),
                  pl.BlockSpec((1, d), lambda i, pos: (0, 0)), pl.BlockSpec(memory_space=pl.ANY)],
        out_specs=pl.BlockSpec((tm, d), lambda i, pos: (i, 0)),
        scratch_shapes=[pltpu.VMEM((2, tm, d), F32), pltpu.SemaphoreType.DMA(())],
    )
    return pl.pallas_call(
        _combine_kernel,
        grid_spec=grid_spec,
        out_shape=jax.ShapeDtypeStruct((t, d), F32),
        compiler_params=_cparams(("arbitrary",)),
        name="combine_final",
    )(pos, x1, g2, rt, gf, ys)


def _rope_tables(pos):
    half = QK_ROPE // 2
    freqs = ROPE_THETA ** (-jnp.arange(half, dtype=F32) / half)
    ang = pos[:, None] * freqs[None, :]
    cos, sin = jnp.cos(ang), jnp.sin(ang)
    return jnp.concatenate([cos, cos], axis=1), jnp.concatenate([sin, sin], axis=1)


def _rotate_half_cols(w):
    lead = w.shape[0]
    w3 = w.reshape(lead, -1, QK_ROPE)
    half = QK_ROPE // 2
    return jnp.concatenate([-w3[..., half:], w3[..., :half]], axis=-1).reshape(lead, -1)


def _pick_tile(n, pref):
    t = min(pref, n)
    while n % t:
        t //= 2
    return t


def kernel(x_prompt, x_sample, cache_ckv, cache_krope, state_conv, state_h, page_table, c_prompt, c_sample, w_ada, b_ada, g_norm1, g_norm2, w_in, w_conv, b_conv, w_rg_a, b_rg_a, w_rg_x, b_rg_x, rg_lambda, g_q, w_uq, g_kv, w_uk, w_uv, w_o, w_group, b_group, w_router, b_router, w_gate, w_up, w_down, g_final):
    nb, seq, d = x_prompt.shape
    bs, nt, _ = x_sample.shape
    depth = w_ada.shape[0]
    d_rnn = w_conv.shape[2]
    q_lora, nh, qk_dim = w_uq.shape[1:]
    kv_lora = w_uk.shape[1]
    v_dim = w_uv.shape[3]
    past = page_table.shape[1] * cache_ckv.shape[2]
    scale = float(qk_dim) ** -0.5
    tp, ts = nb * seq, bs * nt
    assert d_rnn == d and qk_dim == QK_NOPE + QK_ROPE and v_dim == QK_NOPE and nh * v_dim == d

    xp = x_prompt.reshape(tp, d)
    xs_tok = x_sample.reshape(ts, d)
    cos_p, sin_p = _rope_tables(jnp.arange(seq, dtype=F32))
    cos_s, sin_s = _rope_tables(jnp.tile(past + jnp.arange(nt, dtype=F32), bs))
    dbl = lambda a: jnp.concatenate([a, a], axis=1)
    c_all = jnp.concatenate([c_prompt, c_sample], axis=0)
    n_c = c_all.shape[0]
    c_all = jnp.pad(c_all, ((0, (-n_c) % 8), (0, 0)))

    outs_p, outs_s = [], []
    for l in range(depth):
        o1, o2, o3, o4, o5, o6 = np.cumsum([d_rnn, d_rnn, q_lora, kv_lora, QK_ROPE, d]).tolist()
        wi = w_in[l]
        wa = jnp.concatenate([wi[:, :o2], wi[:, o5:]], axis=1).astype(BF16)
        w_kr = wi[:, o4:o5]
        wb = jnp.concatenate([wi[:, o2:o4], w_kr, _rotate_half_cols(w_kr)], axis=1).astype(BF16)
        wrg = jnp.concatenate([w_rg_a[l], w_rg_x[l]], axis=2).astype(BF16)
        brg = jnp.concatenate([b_rg_a[l], b_rg_x[l]], axis=1)[:, None, :]
        wq = w_uq[l]
        wqn = wq[:, :, :QK_NOPE].reshape(q_lora, nh * QK_NOPE).astype(BF16)
        wqr_f = wq[:, :, QK_NOPE:].reshape(q_lora, nh * QK_ROPE)
        wqr, wqrot = wqr_f.astype(BF16), _rotate_half_cols(wqr_f).astype(BF16)
        wuk_flat = w_uk[l].reshape(kv_lora, nh * QK_NOPE).astype(BF16)
        wuv_flat = w_uv[l].reshape(kv_lora, nh * v_dim).astype(BF16)
        wukt = jnp.transpose(w_uk[l], (1, 2, 0)).astype(BF16)
        wuv_h = jnp.transpose(w_uv[l], (1, 0, 2)).astype(BF16)
        wo = w_o[l].astype(BF16)
        wr = jnp.concatenate([w_group[l], w_router[l]], axis=1)
        wr = jnp.pad(wr, ((0, 0), (0, ROUTE_LANES - wr.shape[1])))
        wrh = wr.astype(BF16)
        wrl = (wr - wrh.astype(F32)).astype(BF16)
        br = jnp.pad(jnp.concatenate([b_group[l], b_router[l]]), (0, ROUTE_LANES - N_GROUPS - N_EXPERTS))[None, :]
        row = lambda a: a[None, :]

        mod = ada_mod(c_all, w_ada[l], b_ada[l])
        mods_p = [m[:nb, None, :] for m in jnp.split(mod, 6, axis=1)]
        mods_s = [jnp.repeat(m[nb:nb + bs], nt, axis=0) for m in jnp.split(mod, 6, axis=1)]

        tm_p = _pick_tile(seq, 512)
        pa_p, cq_p, ckv_p, kr_p = in_proj(xp, mods_p[1], mods_p[0], row(g_norm1[l]), wa, wb, row(g_q[l]),
                                          row(g_kv[l]), cos_p, sin_p, per_seq=seq, tm=_pick_tile(seq, 1024))
        tm_s = _pick_tile(ts, 512)
        pa_s, cq_s, ckv_s, kr_s = in_proj(xs_tok, mods_s[1], mods_s[0], row(g_norm1[l]), wa, wb, row(g_q[l]),
                                          row(g_kv[l]), cos_s, sin_s, per_seq=0, tm=tm_s)

        ya_p, conv_p, h_p = rglru_seq(pa_p, nb, seq, w_conv[l], row(b_conv[l]), wrg, brg, row(rg_lambda[l]),
                                      tc=_pick_tile(seq, 1024), dc=_pick_tile(d_rnn, 2048))
        tmajor = lambda a: jnp.transpose(a.reshape(bs, nt, -1), (1, 0, 2))
        ya_s4, conv_s, h_s = rglru_step(tmajor(pa_s[:, :d_rnn]), tmajor(pa_s[:, d_rnn:2 * d_rnn]),
                                        jnp.transpose(state_conv[l], (1, 0, 2)), state_h[l],
                                        w_conv[l], row(b_conv[l]), wrg, brg, row(rg_lambda[l]))
        ya_s = jnp.transpose(ya_s4, (1, 0, 2)).reshape(ts, d)
        conv_s = jnp.transpose(conv_s, (1, 0, 2))

        qn_p, qr_p, kn_p, vt_p = qkv_prompt(cq_p, ckv_p, wqn, wqr, wqrot, wuk_flat, wuv_flat.T, dbl(cos_p), dbl(sin_p),
                                            seq=seq, scale=scale * LOG2_E, tm=_pick_tile(seq, 1024))
        yb_p = attn_prompt(qn_p, qr_p, kn_p, kr_p, vt_p, nb=nb, seq=seq, tq=tm_p, hb=8)
        q_s = q_sample(cq_s, wqn, wqr, wqrot, wukt, dbl(cos_s), dbl(sin_s), scale=scale)
        q_s = jnp.transpose(q_s, (1, 0, 2)).reshape(bs, nt * nh, kv_lora + QK_ROPE)
        o_s = attn_sample(q_s, cache_ckv, jnp.swapaxes(cache_krope, 2, 3), ckv_s.reshape(bs, nt, kv_lora),
                          kr_s.reshape(bs, nt, QK_ROPE), page_table, layer=l, heads=nh,
                          gs=_pick_tile(bs, 4), pg=_pick_tile(page_table.shape[1], 8), ring=4)
        yb_s = uv_sample(o_s.reshape(ts, nh * kv_lora), wuv_h)

        tm_r = _pick_tile(seq, 256)
        cnt0 = jnp.zeros((1, ROUTE_LANES), F32)
        tm_rs = _pick_tile(ts, tm_r)
        x1_p, h_all, rt_p, cnt1, rtt_p = merge_route(xp, ya_p, yb_p, pa_p, mods_p[2], mods_p[4], mods_p[3], wo,
                                              row(g_norm2[l]), wrh, wrl, br, cnt0, per_seq=seq, tm=tm_r,
                                              h_rows=tp + ts, h_row0=0)
        x1_s, h_all, rt_s, cnt2, rtt_s = merge_route(xs_tok, ya_s, yb_s, pa_s, mods_s[2], mods_s[4], mods_s[3], wo,
                                              row(g_norm2[l]), wrh, wrl, br, cnt1, per_seq=0, tm=tm_rs,
                                              h_rows=tp + ts, h_row0=tp, h_prev=h_all)

        tmg = 256
        counts = cnt2[0, :N_EXPERTS].astype(jnp.int32)
        tiles_e = (counts + tmg - 1) // tmg
        tile_end = jnp.cumsum(tiles_e)
        offs = (tile_end - tiles_e) * tmg
        nt_max = (2 * (tp + ts)) // tmg + N_EXPERTS
        tile_ids = jnp.arange(nt_max, dtype=jnp.int32)
        tile_e = jnp.minimum(jnp.sum((tile_ids[:, None] >= tile_end[None, :]).astype(jnp.int32), axis=1), N_EXPERTS - 1)
        n_tiles = tile_end[-1:].astype(jnp.int32)
        tile_first = jnp.concatenate([jnp.ones((1,), jnp.int32), (tile_e[1:] != tile_e[:-1]).astype(jnp.int32)])

        def slot_pos(rtt):
            e = rtt[0:2].astype(jnp.int32)
            hit = e[None] == jnp.arange(N_EXPERTS, dtype=jnp.int32)[:, None, None]
            return jnp.sum(jnp.where(hit, offs[:, None, None], 0), axis=0) + rtt[4:6].astype(jnp.int32)

        pos_p, pos_s = slot_pos(rtt_p), slot_pos(rtt_s)
        pos_all = jnp.concatenate([pos_p, pos_s], axis=1).reshape(-1)
        pos_p, pos_s = pos_p.reshape(-1), pos_s.reshape(-1)
        zero_flags = ((tile_ids >= n_tiles[0] - 1) | (tile_e != jnp.roll(tile_e, -1))).astype(jnp.int32)
        xs_buf = dispatch(pos_all, zero_flags, h_all, tm=tm_r, tile=tmg)
        ys_buf = expert_ffn(tile_e, tile_first, n_tiles, xs_buf, w_gate, w_up, w_down, layer=l, tm=tmg)

        last = l == depth - 1
        gf = row(g_final) if last else None
        assert last, "final norm is fused into the last layer's combine; DEPTH > 1 needs an un-normed variant"
        xp = combine_final(pos_p, x1_p, mods_p[5], rt_p, gf, ys_buf, per_seq=seq, tm=tm_r)
        xs_tok = combine_final(pos_s, x1_s, mods_s[5], rt_s, gf, ys_buf, per_seq=0, tm=tm_rs)

        outs_p.append((ckv_p.reshape(nb, seq, kv_lora), kr_p.reshape(nb, seq, QK_ROPE), conv_p, h_p.reshape(nb, d)))
        outs_s.append((ckv_s.reshape(bs, nt, kv_lora), kr_s.reshape(bs, nt, QK_ROPE), conv_s, h_s))

    stack = lambda outs, k: jnp.stack([o[k] for o in outs])
    return (xp.reshape(nb, seq, d), xs_tok.reshape(bs, nt, d),
            stack(outs_p, 0), stack(outs_p, 1), stack(outs_p, 2), stack(outs_p, 3),
            stack(outs_s, 0), stack(outs_s, 1), stack(outs_s, 2), stack(outs_s, 3))
```

```python
import functools

import jax
import jax.numpy as jnp
import numpy as np
from jax import lax
from jax.experimental import pallas as pl
from jax.experimental.pallas import tpu as pltpu

F32 = jnp.float32
BF16 = jnp.bfloat16

EPS = 1e-6
RG_C = 8.0
RNN_BLOCK_W = 128
CONV_W = 4
QK_NOPE = 128
QK_ROPE = 64
ROPE_THETA = 10000.0
N_GROUPS = 4
EXPERTS_PER_GROUP = 8
N_EXPERTS = N_GROUPS * EXPERTS_PER_GROUP
ROUTE_LANES = 128
ROUTE_PARTS = 2
NEG_BIG = -1e30
LOG2_E = 1.4426950408889634
VMEM_LIMIT = 56 * 1024 * 1024


def _cparams(sem, vmem=VMEM_LIMIT):
    return pltpu.CompilerParams(dimension_semantics=sem, vmem_limit_bytes=vmem)


def _rms(x, g):
    return x * lax.rsqrt(jnp.mean(x * x, axis=-1, keepdims=True) + EPS) * g


def _gelu_tanh(x):
    return 0.5 * x * (1.0 + jnp.tanh(0.7978845608028654 * (x + 0.044715 * x * x * x)))


def _dot(a, b):
    return jnp.dot(a, b, preferred_element_type=F32)


def _dot_nt(a, b):
    return lax.dot_general(a, b, (((1,), (1,)), ((), ())), preferred_element_type=F32)


def _ada_kernel(c_ref, w_ref, b_ref, o_ref):
    c = c_ref[...]
    s = (c * jax.nn.sigmoid(c)).astype(BF16)
    o_ref[...] = _dot(s, w_ref[...].astype(BF16)) + b_ref[...]


def ada_mod(c_all, w_ada, b_ada, tn=1024):
    m, d = c_all.shape
    n = w_ada.shape[1]
    return pl.pallas_call(
        _ada_kernel,
        grid=(n // tn,),
        in_specs=[pl.BlockSpec((m, d), lambda j: (0, 0)),
                  pl.BlockSpec((d, tn), lambda j: (0, j)),
                  pl.BlockSpec((1, tn), lambda j: (0, j))],
        out_specs=pl.BlockSpec((m, tn), lambda j: (0, j)),
        out_shape=jax.ShapeDtypeStruct((m, n), F32),
        compiler_params=_cparams(("arbitrary",)),
        name="ada_mod",
    )(c_all, w_ada, b_ada.reshape(1, n))


def _inproj_kernel(x_ref, sc_ref, sh_ref, g_ref, wa_ref, wb_ref, gq_ref, gkv_ref, cos_ref, sin_ref,
                   pa_ref, cq_ref, ckv_ref, kr_ref, hn_ref, *, q_lora, kv_lora):
    @pl.when(pl.program_id(1) == 0)
    def _():
        hn = _rms(x_ref[...], g_ref[...]) * (1.0 + sc_ref[...]) + sh_ref[...]
        hb = hn.astype(BF16)
        hn_ref[...] = hb
        pb = _dot(hb, wb_ref[...])
        o1 = q_lora
        o2 = o1 + kv_lora
        cq_ref[...] = _rms(pb[:, :o1], gq_ref[...]).astype(BF16)
        ckv_ref[...] = _rms(pb[:, o1:o2], gkv_ref[...])
        kr_ref[...] = pb[:, o2:o2 + QK_ROPE] * cos_ref[...] + pb[:, o2 + QK_ROPE:o2 + 2 * QK_ROPE] * sin_ref[...]

    pa_ref[...] = _dot(hn_ref[...], wa_ref[...]).astype(BF16)


def in_proj(x, sc, sh, g1, wa, wb, gq, gkv, cos64, sin64, *, per_seq, tm, tn=1024):
    t, d = x.shape
    na = wa.shape[1]
    q_lora, kv_lora = gq.shape[1], gkv.shape[1]
    if per_seq:
        tps = per_seq // tm
        mod_spec = pl.BlockSpec((None, 1, d), lambda i, j: (i // tps, 0, 0))
        tab_spec = pl.BlockSpec((tm, QK_ROPE), lambda i, j: (i % tps, 0))
    else:
        mod_spec = pl.BlockSpec((tm, d), lambda i, j: (i, 0))
        tab_spec = pl.BlockSpec((tm, QK_ROPE), lambda i, j: (i, 0))
    row = lambda w: pl.BlockSpec((tm, w), lambda i, j: (i, 0))
    full = lambda a: pl.BlockSpec(a.shape, lambda i, j: (0,) * a.ndim)
    return pl.pallas_call(
        functools.partial(_inproj_kernel, q_lora=q_lora, kv_lora=kv_lora),
        grid=(t // tm, na // tn),
        in_specs=[row(d), mod_spec, mod_spec, full(g1),
                  pl.BlockSpec((d, tn), lambda i, j: (0, j)), full(wb), full(gq), full(gkv), tab_spec, tab_spec],
        out_specs=[pl.BlockSpec((tm, tn), lambda i, j: (i, j)), row(q_lora), row(kv_lora), row(QK_ROPE)],
        out_shape=[jax.ShapeDtypeStruct((t, na), BF16), jax.ShapeDtypeStruct((t, q_lora), BF16),
                   jax.ShapeDtypeStruct((t, kv_lora), F32), jax.ShapeDtypeStruct((t, QK_ROPE), F32)],
        scratch_shapes=[pltpu.VMEM((tm, d), BF16)],
        compiler_params=_cparams(("arbitrary", "arbitrary")),
        name="in_proj",
    )(x, sc, sh, g1, wa, wb, gq, gkv, cos64, sin64)


def _rg_coeffs(xc, wrg_ref, brg_ref, sp):
    a_parts, b_parts = [], []
    for n in range(xc.shape[1] // RNN_BLOCK_W):
        sl = slice(n * RNN_BLOCK_W, (n + 1) * RNN_BLOCK_W)
        xb = xc[:, sl]
        z = _dot(xb.astype(BF16), wrg_ref[n]) + brg_ref[n]
        g = 0.5 + 0.5 * jnp.tanh(0.5 * z)
        r, i = g[:, :RNN_BLOCK_W], g[:, RNN_BLOCK_W:]
        a = jnp.exp(-RG_C * r * sp[:, sl])
        a_parts.append(a)
        b_parts.append(jnp.sqrt(1.0 - a * a) * i * xb)
    return jnp.concatenate(a_parts, axis=1), jnp.concatenate(b_parts, axis=1)


def _softplus_neg(lam):
    return jnp.maximum(-lam, 0.0) + jnp.log1p(jnp.exp(-jnp.abs(lam)))


def _rglru_seq_kernel(x_ref, g_ref, wc_ref, bc_ref, wrg_ref, brg_ref, lam_ref,
                      y_ref, conv_ref, h_ref, tail_ref, hc_ref, a_scr, b_scr):
    t = pl.program_id(2)
    tc, dc = x_ref.shape

    @pl.when(t == 0)
    def _():
        tail_ref[...] = jnp.zeros_like(tail_ref)
        hc_ref[...] = jnp.zeros_like(hc_ref)

    x = x_ref[...].astype(F32)
    xw = jnp.concatenate([tail_ref[...], x], axis=0)
    xc = bc_ref[...] + wc_ref[CONV_W - 1:CONV_W, :] * x
    for k in range(CONV_W - 1):
        d = CONV_W - 1 - k
        xc = xc + wc_ref[k:k + 1, :] * xw[8 - d:8 - d + tc, :]
    tail_ref[...] = x[tc - 8:, :]

    a, b = _rg_coeffs(xc, wrg_ref, brg_ref, _softplus_neg(lam_ref[...]))
    a_scr[...] = a
    b_scr[...] = b

    row = lax.broadcasted_iota(jnp.int32, (8, dc), 0)

    def body(k, hc):
        r0 = pl.multiple_of(k * 8, 8)
        av = a_scr[pl.ds(r0, 8), :]
        bv = b_scr[pl.ds(r0, 8), :]
        for s in (1, 2, 4):
            a_sh = jnp.where(row >= s, pltpu.roll(av, s, 0), 1.0)
            b_sh = jnp.where(row >= s, pltpu.roll(bv, s, 0), 0.0)
            bv = av * b_sh + bv
            av = av * a_sh
        h = av * hc + bv
        b_scr[pl.ds(r0, 8), :] = h
        return jnp.broadcast_to(h[7:8, :], (8, dc))

    hc = lax.fori_loop(0, tc // 8, body, hc_ref[...])
    hc_ref[...] = hc
    y_ref[...] = (b_scr[...] * _gelu_tanh(g_ref[...].astype(F32))).astype(BF16)

    @pl.when(t == pl.num_programs(2) - 1)
    def _():
        conv_ref[...] = x[tc - (CONV_W - 1):, :]
        h_ref[...] = hc[0:1, :]


def rglru_seq(pa, nb, seq, w_conv, b_conv, wrg, brg, lam, *, tc=512, dc=512):
    d = w_conv.shape[1]
    nc, nt = d // dc, seq // tc
    bpc = dc // RNN_BLOCK_W
    return pl.pallas_call(
        _rglru_seq_kernel,
        grid=(nb, nc, nt),
        in_specs=[pl.BlockSpec((tc, dc), lambda b, c, t: (b * nt + t, c)),
                  pl.BlockSpec((tc, dc), lambda b, c, t: (b * nt + t, nc + c)),
                  pl.BlockSpec((CONV_W, dc), lambda b, c, t: (0, c)),
                  pl.BlockSpec((1, dc), lambda b, c, t: (0, c)),
                  pl.BlockSpec((bpc, RNN_BLOCK_W, 2 * RNN_BLOCK_W), lambda b, c, t: (c, 0, 0)),
                  pl.BlockSpec((bpc, 1, 2 * RNN_BLOCK_W), lambda b, c, t: (c, 0, 0)),
                  pl.BlockSpec((1, dc), lambda b, c, t: (0, c))],
        out_specs=[pl.BlockSpec((tc, dc), lambda b, c, t: (b * nt + t, c)),
                   pl.BlockSpec((None, CONV_W - 1, dc), lambda b, c, t: (b, 0, c)),
                   pl.BlockSpec((None, 1, dc), lambda b, c, t: (b, 0, c))],
        out_shape=[jax.ShapeDtypeStruct((nb * seq, d), BF16),
                   jax.ShapeDtypeStruct((nb, CONV_W - 1, d), F32),
                   jax.ShapeDtypeStruct((nb, 1, d), F32)],
        scratch_shapes=[pltpu.VMEM((8, dc), F32), pltpu.VMEM((8, dc), F32),
                        pltpu.VMEM((tc, dc), F32), pltpu.VMEM((tc, dc), F32)],
        compiler_params=_cparams(("arbitrary", "arbitrary", "arbitrary")),
        name="rglru_seq",
    )(pa, pa, w_conv, b_conv, wrg, brg, lam)


def _rglru_step_kernel(x_ref, g_ref, prev_ref, h0_ref, wc_ref, bc_ref, wrg_ref, brg_ref, lam_ref,
                       y_ref, conv_ref, h_ref):
    nt = x_ref.shape[0]
    full = [prev_ref[k] for k in range(CONV_W - 1)] + [x_ref[k].astype(F32) for k in range(nt)]
    sp = _softplus_neg(lam_ref[...])
    h = h0_ref[...]
    for t in range(nt):
        xc = bc_ref[...] + wc_ref[0:1, :] * full[t]
        for k in range(1, CONV_W):
            xc = xc + wc_ref[k:k + 1, :] * full[t + k]
        a, b = _rg_coeffs(xc, wrg_ref, brg_ref, sp)
        h = a * h + b
        y_ref[t] = (h * _gelu_tanh(g_ref[t].astype(F32))).astype(BF16)
    for k in range(CONV_W - 1):
        conv_ref[k] = full[nt + k]
    h_ref[...] = h


def rglru_step(x4, g4, prev, h0, w_conv, b_conv, wrg, brg, lam, *, dc=512):
    nt, bs, d = x4.shape
    bpc = dc // RNN_BLOCK_W
    return pl.pallas_call(
        _rglru_step_kernel,
        grid=(d // dc,),
        in_specs=[pl.BlockSpec((nt, bs, dc), lambda c: (0, 0, c)),
                  pl.BlockSpec((nt, bs, dc), lambda c: (0, 0, c)),
                  pl.BlockSpec((CONV_W - 1, bs, dc), lambda c: (0, 0, c)),
                  pl.BlockSpec((bs, dc), lambda c: (0, c)),
                  pl.BlockSpec((CONV_W, dc), lambda c: (0, c)),
                  pl.BlockSpec((1, dc), lambda c: (0, c)),
                  pl.BlockSpec((bpc, RNN_BLOCK_W, 2 * RNN_BLOCK_W), lambda c: (c, 0, 0)),
                  pl.BlockSpec((bpc, 1, 2 * RNN_BLOCK_W), lambda c: (c, 0, 0)),
                  pl.BlockSpec((1, dc), lambda c: (0, c))],
        out_specs=[pl.BlockSpec((nt, bs, dc), lambda c: (0, 0, c)),
                   pl.BlockSpec((CONV_W - 1, bs, dc), lambda c: (0, 0, c)),
                   pl.BlockSpec((bs, dc), lambda c: (0, c))],
        out_shape=[jax.ShapeDtypeStruct((nt, bs, d), BF16),
                   jax.ShapeDtypeStruct((CONV_W - 1, bs, d), F32),
                   jax.ShapeDtypeStruct((bs, d), F32)],
        compiler_params=_cparams(("arbitrary",)),
        name="rglru_step",
    )(x4, g4, prev, h0, w_conv, b_conv, wrg, brg, lam)


def _rope_heads(qr, qrot, cos128, sin128, scale, out_ref, lane0):
    for hp in range(qr.shape[1] // 128):
        sl = slice(hp * 128, (hp + 1) * 128)
        v = ((qr[:, sl] * cos128 + qrot[:, sl] * sin128) * scale).astype(BF16)
        out_ref[2 * hp, :, lane0:lane0 + QK_ROPE] = v[:, :QK_ROPE]
        out_ref[2 * hp + 1, :, lane0:lane0 + QK_ROPE] = v[:, QK_ROPE:]


def _qkv_prompt_kernel(cq_ref, ckv_ref, wqn_ref, wqr_ref, wqrot_ref, wuk_ref, wuvt_ref, cos_ref, sin_ref,
                       qn_ref, qr_ref, kn_ref, vt_ref, *, scale):
    cq = cq_ref[...]
    qn_ref[...] = (_dot(cq, wqn_ref[...]) * scale).astype(BF16)
    _rope_heads(_dot(cq, wqr_ref[...]), _dot(cq, wqrot_ref[...]), cos_ref[...], sin_ref[...], scale, qr_ref, 0)
    cb = ckv_ref[...].astype(BF16)
    kn_ref[...] = _dot(cb, wuk_ref[...]).astype(BF16)
    vt_ref[...] = _dot_nt(wuvt_ref[...], cb).astype(BF16)


def qkv_prompt(cq, ckv, wqn, wqr, wqrot, wuk, wuvt, cos128, sin128, *, seq, scale, tm=512):
    t, ql = cq.shape
    kvl = ckv.shape[1]
    hn, hr, hv = wqn.shape[1], wqr.shape[1], wuvt.shape[0]
    nh = hr // QK_ROPE
    tps = seq // tm
    full = lambda a: pl.BlockSpec(a.shape, lambda i: (0,) * a.ndim)
    tab = pl.BlockSpec((tm, 128), lambda i: (i % tps, 0))
    return pl.pallas_call(
        functools.partial(_qkv_prompt_kernel, scale=scale),
        grid=(t // tm,),
        in_specs=[pl.BlockSpec((tm, ql), lambda i: (i, 0)), pl.BlockSpec((tm, kvl), lambda i: (i, 0)),
                  full(wqn), full(wqr), full(wqrot), full(wuk), full(wuvt), tab, tab],
        out_specs=[pl.BlockSpec((tm, hn), lambda i: (i, 0)),
                   pl.BlockSpec((nh, tm, QK_ROPE), lambda i: (0, i, 0)),
                   pl.BlockSpec((tm, hn), lambda i: (i, 0)),
                   pl.BlockSpec((hv, tm), lambda i: (0, i))],
        out_shape=[jax.ShapeDtypeStruct((t, hn), BF16), jax.ShapeDtypeStruct((nh, t, QK_ROPE), BF16),
                   jax.ShapeDtypeStruct((t, hn), BF16), jax.ShapeDtypeStruct((hv, t), BF16)],
        compiler_params=_cparams(("arbitrary",)),
        name="qkv_prompt",
    )(cq, ckv, wqn, wqr, wqrot, wuk, wuvt, cos128, sin128)


def _q_sample_kernel(cq_ref, wqn_ref, wqr_ref, wqrot_ref, wukt_ref, cos_ref, sin_ref, q_ref, *, scale):
    cq = cq_ref[...]
    qn = _dot(cq, wqn_ref[...])
    kvl = wukt_ref.shape[2]
    for h in range(wukt_ref.shape[0]):
        ql = _dot(qn[:, h * QK_NOPE:(h + 1) * QK_NOPE].astype(BF16), wukt_ref[h]) * scale
        q_ref[h, :, :kvl] = ql.astype(BF16)
    _rope_heads(_dot(cq, wqr_ref[...]), _dot(cq, wqrot_ref[...]), cos_ref[...], sin_ref[...], scale, q_ref, kvl)


def q_sample(cq, wqn, wqr, wqrot, wukt, cos128, sin128, *, scale):
    t = cq.shape[0]
    nh, _, kvl = wukt.shape
    full = lambda a: pl.BlockSpec(a.shape, lambda i: (0,) * a.ndim)
    return pl.pallas_call(
        functools.partial(_q_sample_kernel, scale=scale),
        grid=(1,),
        in_specs=[full(cq), full(wqn), full(wqr), full(wqrot), full(wukt), full(cos128), full(sin128)],
        out_specs=pl.BlockSpec((nh, t, kvl + QK_ROPE), lambda i: (0, 0, 0)),
        out_shape=jax.ShapeDtypeStruct((nh, t, kvl + QK_ROPE), BF16),
        compiler_params=_cparams(("arbitrary",)),
        name="q_sample",
    )(cq, wqn, wqr, wqrot, wukt, cos128, sin128)


def _softmax_step(s, v, m_ref, l_ref, acc_ref):
    m_prev = m_ref[...]
    m_new = jnp.maximum(m_prev, jnp.max(s, axis=-1, keepdims=True))
    alpha = jnp.exp(m_prev - m_new)
    p = jnp.exp(s - m_new)
    l_ref[...] = alpha * l_ref[...] + jnp.sum(p, axis=-1, keepdims=True)
    acc_ref[...] = alpha * acc_ref[...] + _dot(p.astype(BF16), v)
    m_ref[...] = m_new


def _attn_prompt_kernel(qi_ref, kj_ref, qn_ref, qr_ref, kn_ref, kr_ref, vt_ref, o_ref, m_ref, l_ref, acc_ref, *, hb, cw):
    p = pl.program_id(2)
    i, j = qi_ref[p], kj_ref[p]
    hs = [slice(h * QK_NOPE, (h + 1) * QK_NOPE) for h in range(hb)]

    @pl.when(j == 0)
    def _():
        m_ref[...] = jnp.full_like(m_ref, NEG_BIG)
        l_ref[...] = jnp.zeros_like(l_ref)
        acc_ref[...] = jnp.zeros_like(acc_ref)

    tq = qn_ref.shape[0]

    def block(diagonal):
        kr = kr_ref[...].astype(BF16)
        kh = [jnp.concatenate([kn_ref[:, hs[h]], kr], axis=1) for h in range(hb)]
        chains = [(h, c) for h in range(hb) for c in range(tq // cw)]
        qs = lambda c: slice(c * cw, (c + 1) * cw)
        nk = lambda c: (c + 1) * cw if diagonal else tq
        m_prev = [m_ref[h, :, qs(c)] for h, c in chains]
        l_prev = [l_ref[h, :, qs(c)] for h, c in chains]
        a_prev = [acc_ref[h, :, qs(c)] for h, c in chains]
        st = [_dot_nt(kh[h][:nk(c)], jnp.concatenate([qn_ref[qs(c), hs[h]], qr_ref[h, qs(c), :]], axis=1))
              for h, c in chains]
        if diagonal:
            visible = lambda s, c: (lax.broadcasted_iota(jnp.int32, s.shape, 0)
                                    <= lax.broadcasted_iota(jnp.int32, s.shape, 1) + c * cw)
            st = [jnp.where(visible(s, c), s, NEG_BIG) for s, (h, c) in zip(st, chains)]
        m_new = [jnp.maximum(mp, jnp.max(s, axis=0, keepdims=True)) for mp, s in zip(m_prev, st)]
        alpha = [jnp.exp2(mp - mn) for mp, mn in zip(m_prev, m_new)]
        pt = [jnp.exp2(s - mn) for s, mn in zip(st, m_new)]
        pv = [_dot(vt_ref[hs[h], :nk(c)], p.astype(BF16)) for p, (h, c) in zip(pt, chains)]
        for n, (h, c) in enumerate(chains):
            l_ref[h, :, qs(c)] = alpha[n] * l_prev[n] + jnp.sum(pt[n], axis=0, keepdims=True)
            acc_ref[h, :, qs(c)] = alpha[n] * a_prev[n] + pv[n]
            m_ref[h, :, qs(c)] = m_new[n]

    @pl.when(j < i)
    def _():
        block(False)

    @pl.when(j == i)
    def _():
        block(True)
        for h in range(hb):
            o_ref[:, hs[h]] = (acc_ref[h] / l_ref[h]).T.astype(BF16)


def attn_prompt(qn, qr, kn, kr, vt, *, nb, seq, tq=512, hb=2, cw=256):
    t, hn = qn.shape
    nh = hn // QK_NOPE
    nq = seq // tq
    wb = hb * QK_NOPE
    pairs = [(i, j) for i in range(nq) for j in range(i + 1)]
    qi = jnp.asarray([p[0] for p in pairs], jnp.int32)
    kj = jnp.asarray([p[1] for p in pairs], jnp.int32)
    grid_spec = pltpu.PrefetchScalarGridSpec(
        num_scalar_prefetch=2,
        grid=(nb, nh // hb, len(pairs)),
        in_specs=[pl.BlockSpec((tq, wb), lambda b, h, p, qi, kj: (b * nq + qi[p], h)),
                  pl.BlockSpec((hb, tq, QK_ROPE), lambda b, h, p, qi, kj: (h, b * nq + qi[p], 0)),
                  pl.BlockSpec((tq, wb), lambda b, h, p, qi, kj: (b * nq + kj[p], h)),
                  pl.BlockSpec((tq, QK_ROPE), lambda b, h, p, qi, kj: (b * nq + kj[p], 0)),
                  pl.BlockSpec((wb, tq), lambda b, h, p, qi, kj: (h, b * nq + kj[p]))],
        out_specs=pl.BlockSpec((tq, wb), lambda b, h, p, qi, kj: (b * nq + qi[p], h)),
        scratch_shapes=[pltpu.VMEM((hb, 1, tq), F32), pltpu.VMEM((hb, 1, tq), F32),
                        pltpu.VMEM((hb, QK_NOPE, tq), F32)],
    )
    return pl.pallas_call(
        functools.partial(_attn_prompt_kernel, hb=hb, cw=min(cw, tq)),
        grid_spec=grid_spec,
        out_shape=jax.ShapeDtypeStruct((t, hn), BF16),
        compiler_params=_cparams(("arbitrary", "arbitrary", "arbitrary")),
        name="attn_prompt",
    )(qi, kj, qn, qr, kn, kr, vt)


NEW_ROWS_PAD = 16


def _attn_sample_kernel(pt_ref, q_ref, ck_hbm, kr_hbm, cn_ref, kn_ref, o_ref,
                        ckraw, krraw, ckbuf, krbuf, m_ref, l_ref, acc_ref, sem, *,
                        layer, gs, pg, page, kvl, heads, nsteps, ns):
    b, s_idx = pl.program_id(0), pl.program_id(1)
    ring = ckraw.shape[0]
    step = b * ns + s_idx
    slot = step % ring
    npages = ns * pg

    def start_pages(n, sl):
        bb, ss = n // ns, n % ns
        for g in range(gs):
            for p in range(pg):
                j = g * pg + p
                pid = pt_ref[(bb * gs + g) * npages + ss * pg + p]
                pltpu.make_async_copy(ck_hbm.at[layer, pid], ckraw.at[sl, j], sem.at[sl]).start(priority=j % 2)
                pltpu.make_async_copy(kr_hbm.at[layer, pid], krraw.at[sl, j], sem.at[sl]).start(priority=j % 2)

    @pl.when(step == 0)
    def _():
        for n in range(min(ring - 1, nsteps)):
            start_pages(n, n)

    @pl.when(step + ring - 1 < nsteps)
    def _():
        start_pages(step + ring - 1, (step + ring - 1) % ring)

    @pl.when(s_idx == 0)
    def _():
        m_ref[...] = jnp.full_like(m_ref, NEG_BIG)
        l_ref[...] = jnp.zeros_like(l_ref)
        acc_ref[...] = jnp.zeros_like(acc_ref)

    for j in range(gs * pg):
        pltpu.make_async_copy(ck_hbm.at[layer, 0], ckraw.at[slot, j], sem.at[slot]).wait()
        pltpu.make_async_copy(kr_hbm.at[layer, 0], krraw.at[slot, j], sem.at[slot]).wait()

    for g in range(gs):
        for p in range(pg):
            ckbuf[g, p * page:(p + 1) * page, :] = ckraw[slot, g * pg + p].astype(BF16)
            krbuf[g, :, p * page:(p + 1) * page] = krraw[slot, g * pg + p].astype(BF16)
    s = [_dot_nt(q_ref[g][:, :kvl], ckbuf[g]) + _dot(q_ref[g][:, kvl:], krbuf[g]) for g in range(gs)]
    m_prev = [m_ref[g] for g in range(gs)]
    m_new = [jnp.maximum(m_prev[g], jnp.max(s[g], axis=-1, keepdims=True)) for g in range(gs)]
    alpha = [jnp.exp(m_prev[g] - m_new[g]) for g in range(gs)]
    pr = [jnp.exp(s[g] - m_new[g]) for g in range(gs)]
    pv = [_dot(pr[g].astype(BF16), ckbuf[g]) for g in range(gs)]
    for g in range(gs):
        l_ref[g] = alpha[g] * l_ref[g] + jnp.sum(pr[g], axis=-1, keepdims=True)
        acc_ref[g] = alpha[g] * acc_ref[g] + pv[g]
        m_ref[g] = m_new[g]

    @pl.when(s_idx == pl.num_programs(1) - 1)
    def _():
        nt = cn_ref.shape[1]
        zpad = lambda a: jnp.concatenate([a, jnp.zeros((NEW_ROWS_PAD - nt, a.shape[1]), F32)], axis=0).astype(BF16)
        for g in range(gs):
            q = q_ref[g]
            cn = zpad(cn_ref[g])
            s = _dot_nt(q[:, :kvl], cn) + _dot_nt(q[:, kvl:], zpad(kn_ref[g]))
            row = lax.broadcasted_iota(jnp.int32, s.shape, 0)
            col = lax.broadcasted_iota(jnp.int32, s.shape, 1)
            s = jnp.where(col * heads <= row, s, NEG_BIG)
            _softmax_step(s, cn, m_ref.at[g], l_ref.at[g], acc_ref.at[g])
            o_ref[g] = (acc_ref[g] / l_ref[g]).astype(BF16)


def attn_sample(q, cache_ckv, cache_krope_t, ckv_new, kr_new, page_table, *, layer, heads, gs=2, pg=8, ring=3):
    bs, nq, qd = q.shape
    npages = page_table.shape[1]
    page, kvl = cache_ckv.shape[2], cache_ckv.shape[3]
    nt = ckv_new.shape[1]
    assert nt <= NEW_ROWS_PAD and npages % pg == 0 and bs % gs == 0

    grid_spec = pltpu.PrefetchScalarGridSpec(
        num_scalar_prefetch=1,
        grid=(bs // gs, npages // pg),
        in_specs=[pl.BlockSpec((gs, nq, qd), lambda b, s, pt: (b, 0, 0)),
                  pl.BlockSpec(memory_space=pl.ANY), pl.BlockSpec(memory_space=pl.ANY),
                  pl.BlockSpec((gs, nt, kvl), lambda b, s, pt: (b, 0, 0)),
                  pl.BlockSpec((gs, nt, QK_ROPE), lambda b, s, pt: (b, 0, 0))],
        out_specs=pl.BlockSpec((gs, nq, kvl), lambda b, s, pt: (b, 0, 0)),
        scratch_shapes=[pltpu.VMEM((ring, gs * pg, page, kvl), F32), pltpu.VMEM((ring, gs * pg, QK_ROPE, page), F32),
                        pltpu.VMEM((gs, pg * page, kvl), BF16), pltpu.VMEM((gs, QK_ROPE, pg * page), BF16),
                        pltpu.VMEM((gs, nq, 1), F32), pltpu.VMEM((gs, nq, 1), F32), pltpu.VMEM((gs, nq, kvl), F32),
                        pltpu.SemaphoreType.DMA((ring,))],
    )
    ns = npages // pg
    return pl.pallas_call(
        functools.partial(_attn_sample_kernel, layer=layer, gs=gs, pg=pg, page=page, kvl=kvl, heads=heads,
                          nsteps=(bs // gs) * ns, ns=ns),
        grid_spec=grid_spec,
        out_shape=jax.ShapeDtypeStruct((bs, nq, kvl), BF16),
        compiler_params=_cparams(("arbitrary", "arbitrary")),
        name="attn_sample",
    )(page_table.reshape(-1), q, cache_ckv, cache_krope_t, ckv_new, kr_new)


def _uv_kernel(o_ref, w_ref, y_ref):
    y_ref[...] = _dot(o_ref[...], w_ref[...]).astype(BF16)


def uv_sample(o, wuv_h):
    t = o.shape[0]
    nh, kvl, vd = wuv_h.shape
    return pl.pallas_call(
        _uv_kernel,
        grid=(nh,),
        in_specs=[pl.BlockSpec((t, kvl), lambda h: (0, h)), pl.BlockSpec((None, kvl, vd), lambda h: (h, 0, 0))],
        out_specs=pl.BlockSpec((t, vd), lambda h: (0, h)),
        out_shape=jax.ShapeDtypeStruct((t, nh * vd), BF16),
        compiler_params=_cparams(("arbitrary",)),
        name="uv_sample",
    )(o, wuv_h)


def _merge_route_kernel(*refs, n_real, has_prev):
    if has_prev:
        refs = refs[:14] + refs[15:]
    hn_ref = refs[15]
    i = pl.program_id(0)

    @pl.when(i < n_real)
    def _():
        _merge_route_body(*refs)

    @pl.when(i >= n_real)
    def _():
        hn_ref[...] = jnp.zeros_like(hn_ref)


def _merge_route_body(x_ref, ya_ref, yb_ref, ga_ref, gb_ref, g1_ref, sc_ref, sh_ref, wo_ref, gn_ref,
                      wrh_ref, wrl_ref, br_ref, cin_ref, x1_ref, hn_ref, rt_ref, cnt_ref, rtt_ref):
    @pl.when(pl.program_id(0) == 0)
    def _():
        cnt_ref[...] = cin_ref[...]

    tm = x_ref.shape[0]
    parts = [slice(k * (tm // ROUTE_PARTS), (k + 1) * (tm // ROUTE_PARTS)) for k in range(ROUTE_PARTS)]
    mod = lambda ref, r: ref[r] if ref.shape[0] == tm else ref[...]
    sig = lambda v: 0.5 + 0.5 * jnp.tanh(0.5 * v.astype(F32))
    merged = [(sig(ga_ref[r]) * ya_ref[r].astype(F32) + sig(gb_ref[r]) * yb_ref[r].astype(F32)).astype(BF16)
              for r in parts]
    mix = [_dot(m, wo_ref[...]) for m in merged]
    x1 = [x_ref[r] + mod(g1_ref, r) * mx for r, mx in zip(parts, mix)]
    hn = [_rms(v, gn_ref[...]) * (1.0 + mod(sc_ref, r)) + mod(sh_ref, r) for r, v in zip(parts, x1)]
    for r, v, h in zip(parts, x1, hn):
        x1_ref[r] = v
        hn_ref[r] = h

    hh = [h.astype(BF16) for h in hn]
    hl = [(h - hb.astype(F32)).astype(BF16) for h, hb in zip(hn, hh)]
    logits = [_dot(a, wrh_ref[...]) + (_dot(a, wrl_ref[...]) + _dot(b, wrh_ref[...])) + br_ref[...]
              for a, b in zip(hh, hl)]
    pm = tm // ROUTE_PARTS
    lane = lax.broadcasted_iota(jnp.int32, (pm, ROUTE_LANES), 1)
    tri = jnp.where(lax.broadcasted_iota(jnp.int32, (pm, pm), 0) > lax.broadcasted_iota(jnp.int32, (pm, pm), 1),
                    1.0, 0.0).astype(BF16)

    def first_argmax(v):
        mx = jnp.max(v, axis=-1, keepdims=True)
        return mx, jnp.min(jnp.where(v == mx, lane, ROUTE_LANES), axis=-1, keepdims=True)

    cnt = cnt_ref[...]
    for r, lg in zip(parts, logits):
        gl = jnp.where(lane < N_GROUPS, lg, NEG_BIG)
        gmax, gidx = first_argmax(gl)
        p_top = 1.0 / jnp.sum(jnp.exp(gl - gmax), axis=-1, keepdims=True)
        lo = N_GROUPS + gidx * EXPERTS_PER_GROUP
        el = jnp.where((lane >= lo) & (lane < lo + EXPERTS_PER_GROUP), lg, NEG_BIG)
        m1, l1 = first_argmax(el)
        m2, l2 = first_argmax(jnp.where(lane == l1, NEG_BIG, el))
        e21 = jnp.exp(m2 - m1)
        w1 = p_top / (1.0 + e21)
        w2 = p_top * e21 / (1.0 + e21)
        e1, e2 = l1 - N_GROUPS, l2 - N_GROUPS

        oh1, oh2 = lane == e1, lane == e2
        ohs = jnp.where(oh1 | oh2, 1.0, 0.0)
        before = _dot(tri, ohs.astype(BF16)) + cnt
        r1 = jnp.sum(jnp.where(oh1, before, 0.0), axis=-1, keepdims=True)
        r2 = jnp.sum(jnp.where(oh2, before, 0.0), axis=-1, keepdims=True)
        cnt = cnt + jnp.sum(ohs, axis=0, keepdims=True)

        rt = jnp.where(lane == 0, e1.astype(F32), 0.0)
        rt = jnp.where(lane == 1, e2.astype(F32), rt)
        rt = jnp.where(lane == 2, w1, rt)
        rt = jnp.where(lane == 3, w2, rt)
        rt = jnp.where(lane == 4, r1, rt)
        rt = jnp.where(lane == 5, r2, rt)
        rt_ref[r] = rt
        rtt_ref[:, r] = rt.T[:8, :]
    cnt_ref[...] = cnt


def merge_route(x, ya, yb, pa, g1, sc2, sh2, wo, gn2, wrh, wrl, br, cnt_in, *, per_seq, tm, h_rows, h_row0, h_prev=None):
    t, d = x.shape
    n_real = t // tm
    assert h_row0 % tm == 0 and (h_rows - t) % tm == 0
    n_steps = n_real if h_prev is not None else h_rows // tm
    ri = lambda i: jnp.minimum(i, n_real - 1)
    if per_seq:
        tps = per_seq // tm
        mod_spec = pl.BlockSpec((None, 1, d), lambda i: (ri(i) // tps, 0, 0))
    else:
        mod_spec = pl.BlockSpec((tm, d), lambda i: (ri(i), 0))
    row = pl.BlockSpec((tm, d), lambda i: (ri(i), 0))
    full = lambda a: pl.BlockSpec(a.shape, lambda i: (0,) * a.ndim)
    args = [x, ya, yb, pa, pa, g1, sc2, sh2, wo, gn2, wrh, wrl, br, cnt_in]
    in_specs = [row, row, row, pl.BlockSpec((tm, d), lambda i: (ri(i), 2)), pl.BlockSpec((tm, d), lambda i: (ri(i), 3)),
                mod_spec, mod_spec, mod_spec, full(wo), full(gn2), full(wrh), full(wrl), full(br), full(cnt_in)]
    aliases = {}
    if h_prev is not None:
        args.append(h_prev)
        in_specs.append(pl.BlockSpec(memory_space=pl.ANY))
        aliases = {len(args) - 1: 1}
    return pl.pallas_call(
        functools.partial(_merge_route_kernel, n_real=n_real, has_prev=h_prev is not None),
        grid=(n_steps,),
        in_specs=in_specs,
        out_specs=[row, pl.BlockSpec((tm, d), lambda i: (i + h_row0 // tm, 0)),
                   pl.BlockSpec((tm, ROUTE_LANES), lambda i: (ri(i), 0)),
                   pl.BlockSpec((1, ROUTE_LANES), lambda i: (0, 0)),
                   pl.BlockSpec((8, tm), lambda i: (0, ri(i)))],
        out_shape=[jax.ShapeDtypeStruct((t, d), F32), jax.ShapeDtypeStruct((h_rows, d), F32),
                   jax.ShapeDtypeStruct((t, ROUTE_LANES), F32), jax.ShapeDtypeStruct((1, ROUTE_LANES), F32),
                   jax.ShapeDtypeStruct((8, t), F32)],
        input_output_aliases=aliases,
        compiler_params=_cparams(("arbitrary",)),
        name="merge_route",
    )(*args)


GATHER_UNROLL = 8


def _row_copy(src, dst, sem):
    return pltpu.make_async_copy(src, dst, sem)


def _dispatch_kernel(pos_ref, zf_ref, h_ref, xs_ref, zbuf, sem, zsem, *, tile):
    i = pl.program_id(0)
    tm = h_ref.shape[0]
    nrows = pl.num_programs(0) * tm

    @pl.when(i == 0)
    def _():
        zbuf[...] = jnp.zeros_like(zbuf)

        def zero_tile(wait):
            def body(t, c):
                @pl.when(zf_ref[t] == 1)
                def _():
                    cp = _row_copy(zbuf, xs_ref.at[pl.ds(pl.multiple_of(t * tile, tile), tile), :], zsem)
                    cp.wait() if wait else cp.start()
                return c
            return body

        lax.fori_loop(0, zf_ref.shape[0], zero_tile(False), 0)
        lax.fori_loop(0, zf_ref.shape[0], zero_tile(True), 0)

    def issue(rb, c):
        for u in range(GATHER_UNROLL):
            r = rb * GATHER_UNROLL + u
            for k in range(2):
                p = pos_ref[k * nrows + i * tm + r]
                _row_copy(h_ref.at[pl.ds(r, 1), :], xs_ref.at[pl.ds(p, 1), :], sem).start(priority=k)
        return c

    lax.fori_loop(0, tm // GATHER_UNROLL, issue, 0)
    for k in range(2):
        _row_copy(h_ref, xs_ref.at[pl.ds(0, tm), :], sem).wait()


def dispatch(pos, zero_flags, h_all, *, tm, tile):
    t, d = h_all.shape
    grid_spec = pltpu.PrefetchScalarGridSpec(
        num_scalar_prefetch=2,
        grid=(t // tm,),
        in_specs=[pl.BlockSpec((tm, d), lambda i, pos, zf: (i, 0))],
        out_specs=pl.BlockSpec(memory_space=pl.ANY),
        scratch_shapes=[pltpu.VMEM((tile, d), h_all.dtype), pltpu.SemaphoreType.DMA(()), pltpu.SemaphoreType.DMA(())],
    )
    return pl.pallas_call(
        functools.partial(_dispatch_kernel, tile=tile),
        grid_spec=grid_spec,
        out_shape=jax.ShapeDtypeStruct((zero_flags.shape[0] * tile, d), h_all.dtype),
        compiler_params=_cparams(("arbitrary",)),
        name="dispatch",
    )(pos, zero_flags, h_all)


def _gmm_kernel(te_ref, first_ref, nt_ref, x_ref, wg_ref, wu_ref, wd_ref, y_ref, wgb, wub, wdb):
    i = pl.program_id(0)

    @pl.when(i < nt_ref[0])
    def _():
        @pl.when(first_ref[i] == 1)
        def _():
            wgb[...] = wg_ref[...].astype(BF16)
            wub[...] = wu_ref[...].astype(BF16)
            wdb[...] = wd_ref[...].astype(BF16)

        x = x_ref[...].astype(BF16)
        g = _dot(x, wgb[...])
        u = _dot(x, wub[...])
        y_ref[...] = _dot((g * jax.nn.sigmoid(g) * u).astype(BF16), wdb[...])

    @pl.when(i >= nt_ref[0])
    def _():
        y_ref[...] = jnp.zeros_like(y_ref)


def expert_ffn(tile_e, tile_first, n_tiles, xs, w_gate, w_up, w_down, *, layer, tm):
    s, d = xs.shape
    de = w_gate.shape[3]
    last = lambda i, nt: jnp.minimum(i, nt[0] - 1)
    grid_spec = pltpu.PrefetchScalarGridSpec(
        num_scalar_prefetch=3,
        grid=(s // tm,),
        in_specs=[pl.BlockSpec((tm, d), lambda i, te, tf, nt: (last(i, nt), 0)),
                  pl.BlockSpec((None, None, d, de), lambda i, te, tf, nt: (layer, te[i], 0, 0)),
                  pl.BlockSpec((None, None, d, de), lambda i, te, tf, nt: (layer, te[i], 0, 0)),
                  pl.BlockSpec((None, None, de, d), lambda i, te, tf, nt: (layer, te[i], 0, 0))],
        out_specs=pl.BlockSpec((tm, d), lambda i, te, tf, nt: (i, 0)),
        scratch_shapes=[pltpu.VMEM((d, de), BF16), pltpu.VMEM((d, de), BF16), pltpu.VMEM((de, d), BF16)],
    )
    return pl.pallas_call(
        _gmm_kernel,
        grid_spec=grid_spec,
        out_shape=jax.ShapeDtypeStruct((s, d), F32),
        compiler_params=_cparams(("arbitrary",)),
        name="expert_ffn",
    )(tile_e, tile_first, n_tiles, xs, w_gate, w_up, w_down)


def _combine_kernel(pos_ref, x1_ref, g2_ref, rt_ref, gf_ref, ys_ref, o_ref, gbuf, sem):
    i = pl.program_id(0)
    tm = x1_ref.shape[0]
    nrows = pl.num_programs(0) * tm

    def issue(rb, c):
        for u in range(GATHER_UNROLL):
            r = rb * GATHER_UNROLL + u
            for k in range(2):
                p = pos_ref[k * nrows + i * tm + r]
                _row_copy(ys_ref.at[pl.ds(p, 1), :], gbuf.at[k, pl.ds(r, 1), :], sem).start(priority=k)
        return c

    lax.fori_loop(0, tm // GATHER_UNROLL, issue, 0)
    for k in range(2):
        _row_copy(ys_ref.at[pl.ds(0, tm), :], gbuf.at[k], sem).wait()
    rt = rt_ref[...]
    moe = rt[:, 2:3] * gbuf[0] + rt[:, 3:4] * gbuf[1]
    o_ref[...] = _rms(x1_ref[...] + g2_ref[...] * moe, gf_ref[...])


def combine_final(pos, x1, g2, rt, gf, ys, *, per_seq, tm=256):
    t, d = x1.shape
    if per_seq:
        tps = per_seq // tm
        mod_spec = pl.BlockSpec((None, 1, d), lambda i, pos: (i // tps, 0, 0))
    else:
        mod_spec = pl.BlockSpec((tm, d), lambda i, pos: (i, 0))
    grid_spec = pltpu.PrefetchScalarGridSpec(
        num_scalar_prefetch=1,
        grid=(t // tm,),
        in_specs=[pl.BlockSpec((tm, d), lambda i, pos: (i, 0)), mod_spec,
                  pl.BlockSpec((tm, ROUTE_LANES), lambda i, pos: (i, 0)),
                  pl.BlockSpec((1, d), lambda i, pos: (0, 0)), pl.BlockSpec(memory_space=pl.ANY)],
        out_specs=pl.BlockSpec((tm, d), lambda i, pos: (i, 0)),
        scratch_shapes=[pltpu.VMEM((2, tm, d), F32), pltpu.SemaphoreType.DMA(())],
    )
    return pl.pallas_call(
        _combine_kernel,
        grid_spec=grid_spec,
        out_shape=jax.ShapeDtypeStruct((t, d), F32),
        compiler_params=_cparams(("arbitrary",)),
        name="combine_final",
    )(pos, x1, g2, rt, gf, ys)


def _rope_tables(pos):
    half = QK_ROPE // 2
    freqs = ROPE_THETA ** (-jnp.arange(half, dtype=F32) / half)
    ang = pos[:, None] * freqs[None, :]
    cos, sin = jnp.cos(ang), jnp.sin(ang)
    return jnp.concatenate([cos, cos], axis=1), jnp.concatenate([sin, sin], axis=1)


def _rotate_half_cols(w):
    lead = w.shape[0]
    w3 = w.reshape(lead, -1, QK_ROPE)
    half = QK_ROPE // 2
    return jnp.concatenate([-w3[..., half:], w3[..., :half]], axis=-1).reshape(lead, -1)


def _pick_tile(n, pref):
    t = min(pref, n)
    while n % t:
        t //= 2
    return t


def kernel(x_prompt, x_sample, cache_ckv, cache_krope, state_conv, state_h, page_table, c_prompt, c_sample, w_ada, b_ada, g_norm1, g_norm2, w_in, w_conv, b_conv, w_rg_a, b_rg_a, w_rg_x, b_rg_x, rg_lambda, g_q, w_uq, g_kv, w_uk, w_uv, w_o, w_group, b_group, w_router, b_router, w_gate, w_up, w_down, g_final):
    nb, seq, d = x_prompt.shape
    bs, nt, _ = x_sample.shape
    depth = w_ada.shape[0]
    d_rnn = w_conv.shape[2]
    q_lora, nh, qk_dim = w_uq.shape[1:]
    kv_lora = w_uk.shape[1]
    v_dim = w_uv.shape[3]
    past = page_table.shape[1] * cache_ckv.shape[2]
    scale = float(qk_dim) ** -0.5
    tp, ts = nb * seq, bs * nt
    assert d_rnn == d and qk_dim == QK_NOPE + QK_ROPE and v_dim == QK_NOPE and nh * v_dim == d

    xp = x_prompt.reshape(tp, d)
    xs_tok = x_sample.reshape(ts, d)
    cos_p, sin_p = _rope_tables(jnp.arange(seq, dtype=F32))
    cos_s, sin_s = _rope_tables(jnp.tile(past + jnp.arange(nt, dtype=F32), bs))
    dbl = lambda a: jnp.concatenate([a, a], axis=1)
    c_all = jnp.concatenate([c_prompt, c_sample], axis=0)
    n_c = c_all.shape[0]
    c_all = jnp.pad(c_all, ((0, (-n_c) % 8), (0, 0)))

    outs_p, outs_s = [], []
    for l in range(depth):
        o1, o2, o3, o4, o5, o6 = np.cumsum([d_rnn, d_rnn, q_lora, kv_lora, QK_ROPE, d]).tolist()
        wi = w_in[l]
        wa = jnp.concatenate([wi[:, :o2], wi[:, o5:]], axis=1).astype(BF16)
        w_kr = wi[:, o4:o5]
        wb = jnp.concatenate([wi[:, o2:o4], w_kr, _rotate_half_cols(w_kr)], axis=1).astype(BF16)
        wrg = jnp.concatenate([w_rg_a[l], w_rg_x[l]], axis=2).astype(BF16)
        brg = jnp.concatenate([b_rg_a[l], b_rg_x[l]], axis=1)[:, None, :]
        wq = w_uq[l]
        wqn = wq[:, :, :QK_NOPE].reshape(q_lora, nh * QK_NOPE).astype(BF16)
        wqr_f = wq[:, :, QK_NOPE:].reshape(q_lora, nh * QK_ROPE)
        wqr, wqrot = wqr_f.astype(BF16), _rotate_half_cols(wqr_f).astype(BF16)
        wuk_flat = w_uk[l].reshape(kv_lora, nh * QK_NOPE).astype(BF16)
        wuv_flat = w_uv[l].reshape(kv_lora, nh * v_dim).astype(BF16)
        wukt = jnp.transpose(w_uk[l], (1, 2, 0)).astype(BF16)
        wuv_h = jnp.transpose(w_uv[l], (1, 0, 2)).astype(BF16)
        wo = w_o[l].astype(BF16)
        wr = jnp.concatenate([w_group[l], w_router[l]], axis=1)
        wr = jnp.pad(wr, ((0, 0), (0, ROUTE_LANES - wr.shape[1])))
        wrh = wr.astype(BF16)
        wrl = (wr - wrh.astype(F32)).astype(BF16)
        br = jnp.pad(jnp.concatenate([b_group[l], b_router[l]]), (0, ROUTE_LANES - N_GROUPS - N_EXPERTS))[None, :]
        row = lambda a: a[None, :]

        mod = ada_mod(c_all, w_ada[l], b_ada[l])
        mods_p = [m[:nb, None, :] for m in jnp.split(mod, 6, axis=1)]
        mods_s = [jnp.repeat(m[nb:nb + bs], nt, axis=0) for m in jnp.split(mod, 6, axis=1)]

        tm_p = _pick_tile(seq, 512)
        pa_p, cq_p, ckv_p, kr_p = in_proj(xp, mods_p[1], mods_p[0], row(g_norm1[l]), wa, wb, row(g_q[l]),
                                          row(g_kv[l]), cos_p, sin_p, per_seq=seq, tm=_pick_tile(seq, 1024))
        tm_s = _pick_tile(ts, 512)
        pa_s, cq_s, ckv_s, kr_s = in_proj(xs_tok, mods_s[1], mods_s[0], row(g_norm1[l]), wa, wb, row(g_q[l]),
                                          row(g_kv[l]), cos_s, sin_s, per_seq=0, tm=tm_s)

        ya_p, conv_p, h_p = rglru_seq(pa_p, nb, seq, w_conv[l], row(b_conv[l]), wrg, brg, row(rg_lambda[l]),
                                      tc=_pick_tile(seq, 1024), dc=_pick_tile(d_rnn, 2048))
        tmajor = lambda a: jnp.transpose(a.reshape(bs, nt, -1), (1, 0, 2))
        ya_s4, conv_s, h_s = rglru_step(tmajor(pa_s[:, :d_rnn]), tmajor(pa_s[:, d_rnn:2 * d_rnn]),
                                        jnp.transpose(state_conv[l], (1, 0, 2)), state_h[l],
                                        w_conv[l], row(b_conv[l]), wrg, brg, row(rg_lambda[l]))
        ya_s = jnp.transpose(ya_s4, (1, 0, 2)).reshape(ts, d)
        conv_s = jnp.transpose(conv_s, (1, 0, 2))

        qn_p, qr_p, kn_p, vt_p = qkv_prompt(cq_p, ckv_p, wqn, wqr, wqrot, wuk_flat, wuv_flat.T, dbl(cos_p), dbl(sin_p),
                                            seq=seq, scale=scale * LOG2_E, tm=_pick_tile(seq, 1024))
        yb_p = attn_prompt(qn_p, qr_p, kn_p, kr_p, vt_p, nb=nb, seq=seq, tq=tm_p, hb=16)
        q_s = q_sample(cq_s, wqn, wqr, wqrot, wukt, dbl(cos_s), dbl(sin_s), scale=scale)
        q_s = jnp.transpose(q_s, (1, 0, 2)).reshape(bs, nt * nh, kv_lora + QK_ROPE)
        o_s = attn_sample(q_s, cache_ckv, jnp.swapaxes(cache_krope, 2, 3), ckv_s.reshape(bs, nt, kv_lora),
                          kr_s.reshape(bs, nt, QK_ROPE), page_table, layer=l, heads=nh,
                          gs=_pick_tile(bs, 4), pg=_pick_tile(page_table.shape[1], 8), ring=4)
        yb_s = uv_sample(o_s.reshape(ts, nh * kv_lora), wuv_h)

        tm_r = _pick_tile(seq, 256)
        cnt0 = jnp.zeros((1, ROUTE_LANES), F32)
        tm_rs = _pick_tile(ts, tm_r)
        x1_p, h_all, rt_p, cnt1, rtt_p = merge_route(xp, ya_p, yb_p, pa_p, mods_p[2], mods_p[4], mods_p[3], wo,
                                              row(g_norm2[l]), wrh, wrl, br, cnt0, per_seq=seq, tm=tm_r,
                                              h_rows=tp + ts, h_row0=0)
        x1_s, h_all, rt_s, cnt2, rtt_s = merge_route(xs_tok, ya_s, yb_s, pa_s, mods_s[2], mods_s[4], mods_s[3], wo,
                                              row(g_norm2[l]), wrh, wrl, br, cnt1, per_seq=0, tm=tm_rs,
                                              h_rows=tp + ts, h_row0=tp, h_prev=h_all)

        tmg = 256
        counts = cnt2[0, :N_EXPERTS].astype(jnp.int32)
        tiles_e = (counts + tmg - 1) // tmg
        tile_end = jnp.cumsum(tiles_e)
        offs = (tile_end - tiles_e) * tmg
        nt_max = (2 * (tp + ts)) // tmg + N_EXPERTS
        tile_ids = jnp.arange(nt_max, dtype=jnp.int32)
        tile_e = jnp.minimum(jnp.sum((tile_ids[:, None] >= tile_end[None, :]).astype(jnp.int32), axis=1), N_EXPERTS - 1)
        n_tiles = tile_end[-1:].astype(jnp.int32)
        tile_first = jnp.concatenate([jnp.ones((1,), jnp.int32), (tile_e[1:] != tile_e[:-1]).astype(jnp.int32)])

        def slot_pos(rtt):
            e = rtt[0:2].astype(jnp.int32)
            hit = e[None] == jnp.arange(N_EXPERTS, dtype=jnp.int32)[:, None, None]
            return jnp.sum(jnp.where(hit, offs[:, None, None], 0), axis=0) + rtt[4:6].astype(jnp.int32)

        pos_p, pos_s = slot_pos(rtt_p), slot_pos(rtt_s)
        pos_all = jnp.concatenate([pos_p, pos_s], axis=1).reshape(-1)
        pos_p, pos_s = pos_p.reshape(-1), pos_s.reshape(-1)
        zero_flags = ((tile_ids >= n_tiles[0] - 1) | (tile_e != jnp.roll(tile_e, -1))).astype(jnp.int32)
        xs_buf = dispatch(pos_all, zero_flags, h_all, tm=tm_r, tile=tmg)
        ys_buf = expert_ffn(tile_e, tile_first, n_tiles, xs_buf, w_gate, w_up, w_down, layer=l, tm=tmg)

        last = l == depth - 1
        gf = row(g_final) if last else None
        assert last, "final norm is fused into the last layer's combine; DEPTH > 1 needs an un-normed variant"
        xp = combine_final(pos_p, x1_p, mods_p[5], rt_p, gf, ys_buf, per_seq=seq, tm=tm_r)
        xs_tok = combine_final(pos_s, x1_s, mods_s[5], rt_s, gf, ys_buf, per_seq=0, tm=tm_rs)

        outs_p.append((ckv_p.reshape(nb, seq, kv_lora), kr_p.reshape(nb, seq, QK_ROPE), conv_p, h_p.reshape(nb, d)))
        outs_s.append((ckv_s.reshape(bs, nt, kv_lora), kr_s.reshape(bs, nt, QK_ROPE), conv_s, h_s))

    stack = lambda outs, k: jnp.stack([o[k] for o in outs])
    return (xp.reshape(nb, seq, d), xs_tok.reshape(bs, nt, d),
            stack(outs_p, 0), stack(outs_p, 1), stack(outs_p, 2), stack(outs_p, 3),
            stack(outs_s, 0), stack(outs_s, 1), stack(outs_s, 2), stack(outs_s, 3))
```
